```python
import math
import numpy as np
import jax
import jax.numpy as jnp
from jax import lax

D_MODEL = 1024
BATCH = 2
SEQ = 8192
DEPTH = 2

DN_HEADS = 4
DN_HEAD_DIM = 128
DN_WIDTH = DN_HEADS * DN_HEAD_DIM
DN_CONV = 4
DN_CHUNK = 64
POOL_GROUPS = 4
POOL_GROUP_DIM = 128
POOL_WIDTH = POOL_GROUPS * POOL_GROUP_DIM
POOL_WINDOWS = (2, 4, 8, 16)
SG_GROUPS = 4
SG_GROUP_DIM = 128
SG_WIDTH = SG_GROUPS * SG_GROUP_DIM
SG_CHUNK = 128
RET_HEADS = 4
RET_HEAD_DIM = 128
RET_WIDTH = RET_HEADS * RET_HEAD_DIM
RET_CHUNK = 128
ROPE_BASE = 10000.0
N_BRANCH = 4
D_FF = 4 * D_MODEL
N_MOD = 6
EPS = 1e-6
SPLIT_SIZES = (DN_WIDTH, DN_WIDTH, DN_WIDTH, DN_WIDTH, DN_HEADS, DN_HEADS,
               POOL_WIDTH, SG_WIDTH, SG_WIDTH,
               RET_WIDTH, RET_WIDTH, RET_WIDTH, RET_WIDTH,
               N_BRANCH * D_MODEL)
IN_COLS = sum(SPLIT_SIZES)

kernel_name = 'hybrid_parallel_mixer_trunk'


def rmsnorm(x, g):
    xf = x.astype(jnp.float32)
    y = xf * lax.rsqrt(jnp.mean(xf * xf, axis=-1, keepdims=True) + EPS)
    return (y * g.astype(jnp.float32)).astype(x.dtype)


def layernorm(x, g, b):
    mu = jnp.mean(x, axis=-1, keepdims=True)
    var = jnp.mean(jnp.square(x - mu), axis=-1, keepdims=True)
    return (x - mu) * lax.rsqrt(var + EPS) * g + b


def head_groupnorm(x, g):
    mu = jnp.mean(x, axis=-1, keepdims=True)
    var = jnp.mean(jnp.square(x - mu), axis=-1, keepdims=True)
    return (x - mu) * lax.rsqrt(var + EPS) * g.reshape(x.shape[-2], x.shape[-1])


def l2norm(x):
    return x * lax.rsqrt(jnp.sum(x * x, axis=-1, keepdims=True) + EPS)


def causal_dwconv(x, w):
    k_size, ch = w.shape
    return lax.conv_general_dilated(x, w[:, None, :], window_strides=(1,),
                                    padding=[(k_size - 1, 0)],
                                    dimension_numbers=('NWC', 'WIO', 'NWC'),
                                    feature_group_count=ch)


def rope(x, positions):
    d = x.shape[-1]
    inv = ROPE_BASE ** (-jnp.arange(0, d, 2, dtype=jnp.float32) / d)
    ang = positions.astype(jnp.float32)[..., None] * inv
    cos = jnp.cos(ang)[:, :, None, :]
    sin = jnp.sin(ang)[:, :, None, :]
    x1, x2 = x[..., :d // 2], x[..., d // 2:]
    return jnp.concatenate([x1 * cos - x2 * sin, x1 * sin + x2 * cos], axis=-1)


def to_chunks(t, c):
    b, s, h, d = t.shape
    return t.reshape(b, s // c, c, h, d).transpose(0, 3, 1, 2, 4)


def from_chunks(t):
    b, h, n, c, d = t.shape
    return t.transpose(0, 2, 3, 1, 4).reshape(b, n * c, h, d)


def gated_delta_rule(q, k, v, g, beta):
    c = DN_CHUNK
    dk = q.shape[-1]
    q = to_chunks(q * dk ** -0.5, c)
    k = to_chunks(k, c)
    v = to_chunks(v, c)
    g = lax.cumsum(to_chunks(g[..., None], c)[..., 0], axis=3)
    beta = to_chunks(beta[..., None], c)[..., 0]
    k_beta = k * beta[..., None]
    causal = jnp.tril(jnp.ones((c, c), dtype=bool))
    decay = jnp.exp(jnp.where(causal, g[..., :, None] - g[..., None, :], -jnp.inf))
    kk = jnp.einsum('bhnid,bhnjd->bhnij', k_beta, k) * decay
    a_mat = jnp.eye(c, dtype=jnp.float32) + jnp.tril(kk, -1)
    u = lax.linalg.triangular_solve(a_mat, v * beta[..., None], left_side=True, lower=True)
    w = lax.linalg.triangular_solve(a_mat, k_beta * jnp.exp(g)[..., None], left_side=True, lower=True)
    qk = jnp.einsum('bhnid,bhnjd->bhnij', q, k) * decay
    g_last = g[..., -1:]
    k_state = k * jnp.exp(g_last - g)[..., None]
    q_state = q * jnp.exp(g)[..., None]

    def step(state, inp):
        qs, ks, u_i, w_i, qk_i, gl = inp
        v_new = u_i - jnp.einsum('bhcd,bhde->bhce', w_i, state)
        o = jnp.einsum('bhcd,bhde->bhce', qs, state) + jnp.einsum('bhij,bhje->bhie', qk_i, v_new)
        state = state * jnp.exp(gl)[..., None] + jnp.einsum('bhcd,bhce->bhde', ks, v_new)
        return state, o

    bsz, nh, _, _, dv = v.shape
    state0 = jnp.zeros((bsz, nh, dk, dv), jnp.float32)
    xs = tuple(jnp.moveaxis(t, 2, 0) for t in (q_state, k_state, u, w, qk, g_last))
    _, o = lax.scan(step, state0, xs)
    return from_chunks(jnp.moveaxis(o, 0, 2))


def retention(q, k, v):
    c = RET_CHUNK
    nh, dh = q.shape[2], q.shape[3]
    log_gamma = jnp.log1p(-jnp.exp2(-5.0 - jnp.arange(nh, dtype=jnp.float32)))
    q = to_chunks(q, c)
    k = to_chunks(k * dh ** -0.5, c)
    v = to_chunks(v, c)
    idx = jnp.arange(c, dtype=jnp.float32)
    rel = idx[:, None] - idx[None, :]
    decay = jnp.where(rel >= 0, jnp.exp(log_gamma[:, None, None] * jnp.maximum(rel, 0.0)), 0.0)
    scores = jnp.einsum('bhnid,bhnjd->bhnij', q, k) * decay[None, :, None]
    o_inner = jnp.einsum('bhnij,bhnje->bhnie', scores, v)
    zeta = jnp.exp(log_gamma[:, None] * (c - 1 - idx))
    kv = jnp.einsum('bhncd,bhnce->bhnde', k * zeta[None, :, None, :, None], v)
    gamma_c = jnp.exp(log_gamma * c)[None, :, None, None]

    def step(state, kv_i):
        return state * gamma_c + kv_i, state

    bsz = q.shape[0]
    _, prev = lax.scan(step, jnp.zeros((bsz, nh, dh, dh), jnp.float32), jnp.moveaxis(kv, 2, 0))
    prev = jnp.moveaxis(prev, 0, 2)
    xi = jnp.exp(log_gamma[:, None] * (idx + 1.0))
    o_cross = jnp.einsum('bhncd,bhnde->bhnce', q, prev) * xi[None, :, None, :, None]
    return from_chunks(o_inner + o_cross)


def multiscale_pool(p, pool_w, pool_scale):
    bsz, seq, _ = p.shape
    cs = lax.cumsum(p, axis=1)
    count_base = jnp.arange(1, seq + 1, dtype=jnp.float32)[:, None]
    outs = []
    for i, win in enumerate(POOL_WINDOWS):
        sl = slice(i * POOL_GROUP_DIM, (i + 1) * POOL_GROUP_DIM)
        cs_g = cs[..., sl]
        prev = jnp.pad(cs_g, ((0, 0), (win, 0), (0, 0)))[:, :seq]
        outs.append((cs_g - prev) / jnp.minimum(count_base, float(win)) - p[..., sl])
    pooled = jnp.stack(outs, axis=2)
    y = jnp.einsum('bsgc,gcd->bsgd', pooled, pool_w).reshape(bsz, seq, POOL_WIDTH)
    return y * pool_scale


def spatial_gating(su, sv, ln_g, ln_b, sg_w, sg_b):
    bsz, seq, _ = su.shape
    u = jax.nn.gelu(su, approximate=False)
    v = layernorm(jax.nn.gelu(sv, approximate=False), ln_g, ln_b)
    v = v.reshape(bsz, seq // SG_CHUNK, SG_CHUNK, SG_GROUPS, SG_GROUP_DIM)
    w_causal = jnp.tril(sg_w)
    mixed = jnp.einsum('gts,bnsgc->bntgc', w_causal, v) + sg_b.T[None, None, :, :, None]
    return u * mixed.reshape(bsz, seq, SG_WIDTH)


def hybrid_mixer(h, positions, w_in, dn_conv_w, dn_a_log, dn_dt_bias, dn_norm_g,
                 pool_w, pool_scale, sg_ln_g, sg_ln_b, sg_w, sg_b, ret_gn_g,
                 w_br_dn, w_br_pool, w_br_sg, w_br_ret, w_out):
    bsz, seq, _ = h.shape
    f32 = jnp.float32
    dt = h.dtype
    proj = h @ w_in
    cuts = np.cumsum(SPLIT_SIZES)[:-1].tolist()
    (dq, dk, dv, dz, db, da, pl, su, sv, rq, rk, rv, rg, gates) = jnp.split(proj, cuts, axis=-1)

    def heads(t, n, d):
        return t.astype(f32).reshape(bsz, seq, n, d)

    qkv = jax.nn.silu(causal_dwconv(jnp.concatenate([dq, dk, dv], axis=-1).astype(f32), dn_conv_w.astype(f32)))
    q_a, k_a, v_a = [heads(t, DN_HEADS, DN_HEAD_DIM) for t in jnp.split(qkv, 3, axis=-1)]
    beta = jax.nn.sigmoid(db.astype(f32))
    g_log = -jnp.exp(dn_a_log.astype(f32)) * jax.nn.softplus(da.astype(f32) + dn_dt_bias.astype(f32))
    o_a = gated_delta_rule(l2norm(q_a), l2norm(k_a), v_a, g_log, beta)
    o_a = rmsnorm(o_a, dn_norm_g) * jax.nn.silu(heads(dz, DN_HEADS, DN_HEAD_DIM))
    y_a = o_a.reshape(bsz, seq, DN_WIDTH)

    y_b = multiscale_pool(pl.astype(f32), pool_w.astype(f32), pool_scale.astype(f32))

    y_c = spatial_gating(su.astype(f32), sv.astype(f32), sg_ln_g.astype(f32), sg_ln_b.astype(f32),
                         sg_w.astype(f32), sg_b.astype(f32))

    q_r = rope(heads(rq, RET_HEADS, RET_HEAD_DIM), positions)
    k_r = rope(heads(rk, RET_HEADS, RET_HEAD_DIM), positions)
    o_r = head_groupnorm(retention(q_r, k_r, heads(rv, RET_HEADS, RET_HEAD_DIM)), ret_gn_g.astype(f32))
    y_d = jax.nn.silu(rg.astype(f32)) * o_r.reshape(bsz, seq, RET_WIDTH)

    gate = jax.nn.sigmoid(gates.astype(f32)).reshape(bsz, seq, N_BRANCH, D_MODEL).astype(dt)
    merged = (gate[:, :, 0] * (y_a.astype(dt) @ w_br_dn)
              + gate[:, :, 1] * (y_b.astype(dt) @ w_br_pool)
              + gate[:, :, 2] * (y_c.astype(dt) @ w_br_sg)
              + gate[:, :, 3] * (y_d.astype(dt) @ w_br_ret))
    return merged @ w_out


def setup_inputs(seed: int = 0) -> dict:
    key = jax.random.key(seed)
    keys = iter(jax.random.split(key, 32))
    f32 = jnp.float32
    L, D = DEPTH, D_MODEL

    def normal(shape, scale):
        return jax.random.normal(next(keys), shape, f32) * scale

    def gain(shape):
        return 1.0 + normal(shape, 0.02)

    x = normal((BATCH, SEQ, D), 1.0)
    c = normal((BATCH, D), 1.0)
    positions = (jnp.arange(SEQ, dtype=jnp.int32)[None, :]
                 + jax.random.randint(next(keys), (BATCH, 1), 0, 4096, dtype=jnp.int32))
    norm1_g = gain((L, D))
    norm2_g = gain((L, D))
    ada_w = normal((L, D, N_MOD * D), D ** -0.5)
    ada_b = normal((L, N_MOD * D), 0.02)
    w_in = normal((L, D, IN_COLS), D ** -0.5)
    dn_conv_w = normal((L, DN_CONV, 3 * DN_WIDTH), DN_CONV ** -0.5)
    dn_a_log = jnp.log(jax.random.uniform(next(keys), (L, DN_HEADS), f32, 1.0, 16.0))
    dt_init = jnp.exp(jax.random.uniform(next(keys), (L, DN_HEADS), f32, math.log(1e-3), math.log(1e-1)))
    dn_dt_bias = dt_init + jnp.log(-jnp.expm1(-dt_init))
    dn_norm_g = gain((L, DN_HEAD_DIM))
    pool_w = normal((L, POOL_GROUPS, POOL_GROUP_DIM, POOL_GROUP_DIM), POOL_GROUP_DIM ** -0.5)
    pool_scale = 1.0 + normal((L, POOL_WIDTH), 0.1)
    sg_ln_g = gain((L, SG_WIDTH))
    sg_ln_b = normal((L, SG_WIDTH), 0.02)
    sg_w = normal((L, SG_GROUPS, SG_CHUNK, SG_CHUNK), SG_CHUNK ** -0.5)
    sg_b = 1.0 + normal((L, SG_GROUPS, SG_CHUNK), 0.02)
    ret_gn_g = gain((L, RET_WIDTH))
    w_br_dn = normal((L, DN_WIDTH, D), DN_WIDTH ** -0.5)
    w_br_pool = normal((L, POOL_WIDTH, D), POOL_WIDTH ** -0.5)
    w_br_sg = normal((L, SG_WIDTH, D), SG_WIDTH ** -0.5)
    w_br_ret = normal((L, RET_WIDTH, D), RET_WIDTH ** -0.5)
    w_out = normal((L, D, D), D ** -0.5)
    mlp_w1 = normal((L, D, D_FF), D ** -0.5)
    mlp_w2 = normal((L, D_FF, D), D_FF ** -0.5)
    final_g = gain((D,))
    return {'x': x, 'c': c, 'positions': positions, 'norm1_g': norm1_g, 'norm2_g': norm2_g,
            'ada_w': ada_w, 'ada_b': ada_b, 'w_in': w_in, 'dn_conv_w': dn_conv_w,
            'dn_a_log': dn_a_log, 'dn_dt_bias': dn_dt_bias, 'dn_norm_g': dn_norm_g,
            'pool_w': pool_w, 'pool_scale': pool_scale, 'sg_ln_g': sg_ln_g, 'sg_ln_b': sg_ln_b,
            'sg_w': sg_w, 'sg_b': sg_b, 'ret_gn_g': ret_gn_g, 'w_br_dn': w_br_dn,
            'w_br_pool': w_br_pool, 'w_br_sg': w_br_sg, 'w_br_ret': w_br_ret, 'w_out': w_out,
            'mlp_w1': mlp_w1, 'mlp_w2': mlp_w2, 'final_g': final_g}


def reference(x, c, positions, norm1_g, norm2_g, ada_w, ada_b, w_in, dn_conv_w, dn_a_log,
              dn_dt_bias, dn_norm_g, pool_w, pool_scale, sg_ln_g, sg_ln_b, sg_w, sg_b, ret_gn_g,
              w_br_dn, w_br_pool, w_br_sg, w_br_ret, w_out, mlp_w1, mlp_w2, final_g):
    cond = jax.nn.silu(c)
    for l in range(DEPTH):
        mod = cond @ ada_w[l] + ada_b[l]
        shift1, scale1, gate1, shift2, scale2, gate2 = [m[:, None, :] for m in jnp.split(mod, N_MOD, axis=-1)]
        h = rmsnorm(x, norm1_g[l]) * (1.0 + scale1) + shift1
        x = x + gate1 * hybrid_mixer(h, positions, w_in[l], dn_conv_w[l], dn_a_log[l], dn_dt_bias[l],
                                     dn_norm_g[l], pool_w[l], pool_scale[l], sg_ln_g[l], sg_ln_b[l],
                                     sg_w[l], sg_b[l], ret_gn_g[l], w_br_dn[l], w_br_pool[l],
                                     w_br_sg[l], w_br_ret[l], w_out[l])
        h = rmsnorm(x, norm2_g[l]) * (1.0 + scale2) + shift2
        x = x + gate2 * (jnp.square(jax.nn.relu(h @ mlp_w1[l])) @ mlp_w2[l])
    return rmsnorm(x, final_g)
```

```python
import functools
import math

import jax
import jax.numpy as jnp
from jax import lax
from jax.experimental import pallas as pl
from jax.experimental.pallas import tpu as pltpu

F32 = jnp.float32
BF16 = jnp.bfloat16

EPS = 1e-6
N_HEADS = 4
HEAD_DIM = 128
WIDTH = N_HEADS * HEAD_DIM
N_BRANCH = 4
N_MOD = 6
DN_CONV = 4
DN_CHUNK = 64
CHUNK = 128
POOL_WINDOWS = (2, 4, 8, 16)
POOL_HALO = 16
ROPE_BASE = 10000.0
VMEM_LIMIT = 48 * 1024 * 1024

CB_GATES = 0
CB_DQ, CB_DK, CB_DV, CB_DZ, CB_PL, CB_SU, CB_SV, CB_RQ, CB_RK, CB_RV, CB_RG = range(8, 19)
N_CB = 19


def _sigmoid(x):
    return 1.0 / (1.0 + jnp.exp(-x))


def _silu(x):
    return x * _sigmoid(x)


def _softplus(x):
    return jnp.maximum(x, 0.0) + jnp.log1p(jnp.exp(-jnp.abs(x)))


def _gelu(x):
    return 0.5 * x * (1.0 + lax.erf(x * (1.0 / math.sqrt(2.0))))


def _dot(a, b):
    return jnp.dot(a, b, preferred_element_type=F32)


def _dot_nt(a, b):
    return lax.dot_general(a, b, (((1,), (1,)), ((), ())), preferred_element_type=F32)


def _dot_tn(a, b):
    return lax.dot_general(a, b, (((0,), (0,)), ((), ())), preferred_element_type=F32)


def _split(a):
    hi = a.astype(BF16)
    return hi, (a - hi.astype(F32)).astype(BF16)


def _dot_split(a_hi, a_lo, b_hi, b_lo):
    return _dot(a_hi, b_hi) + (_dot(a_hi, b_lo) + _dot(a_lo, b_hi))


def _params(n_axes):
    return pltpu.CompilerParams(dimension_semantics=("arbitrary",) * n_axes,
                                vmem_limit_bytes=VMEM_LIMIT)


def _mod_kernel(c_ref, w_ref, b_ref, o_ref):
    cond = _silu(c_ref[...])
    o_ref[...] = _dot(cond.astype(BF16), w_ref[...].astype(BF16)) + b_ref[...]


def _modulation(c, ada_w, ada_b):
    n_layers, d, n_out = ada_w.shape
    bsz = c.shape[0]
    rows = 8
    c_pad = jnp.zeros((rows, d), F32).at[:bsz].set(c)
    tn = n_out // 4
    out = pl.pallas_call(
        _mod_kernel,
        out_shape=jax.ShapeDtypeStruct((n_layers, rows, n_out), F32),
        grid=(n_layers, n_out // tn),
        in_specs=[pl.BlockSpec((rows, d), lambda l, j: (0, 0)),
                  pl.BlockSpec((None, d, tn), lambda l, j: (l, 0, j)),
                  pl.BlockSpec((None, 1, tn), lambda l, j: (l, 0, j))],
        out_specs=pl.BlockSpec((None, rows, tn), lambda l, j: (l, 0, j)),
        compiler_params=_params(2),
        name="adaln_mod",
    )(c_pad, ada_w, ada_b.reshape(n_layers, 1, n_out))
    return out[:, :bsz]


def _rope_kernel(pos_ref, inv_ref, sgn_ref, cos_ref, sin_ref):
    ang = pos_ref[...].astype(F32) * inv_ref[...]
    cos_ref[...] = jnp.cos(ang)
    sin_ref[...] = jnp.sin(ang) * sgn_ref[...]


def _rope_tables(positions):
    bsz, seq = positions.shape
    half = HEAD_DIM // 2
    inv = ROPE_BASE ** (-jnp.arange(0, HEAD_DIM, 2, dtype=F32) / HEAD_DIM)
    inv2 = jnp.concatenate([inv, inv]).reshape(1, HEAD_DIM)
    sgn = jnp.concatenate([-jnp.ones((half,), F32), jnp.ones((half,), F32)]).reshape(1, HEAD_DIM)
    n_tok = bsz * seq
    ts = min(1024, n_tok)
    shp = jax.ShapeDtypeStruct((n_tok, HEAD_DIM), F32)
    cos2, sin2 = pl.pallas_call(
        _rope_kernel,
        out_shape=(shp, shp),
        grid=(n_tok // ts,),
        in_specs=[pl.BlockSpec((ts, 1), lambda i: (i, 0)),
                  pl.BlockSpec((1, HEAD_DIM), lambda i: (0, 0)),
                  pl.BlockSpec((1, HEAD_DIM), lambda i: (0, 0))],
        out_specs=(pl.BlockSpec((ts, HEAD_DIM), lambda i: (i, 0)),
                   pl.BlockSpec((ts, HEAD_DIM), lambda i: (i, 0))),
        compiler_params=_params(1),
        name="rope_tables",
    )(positions.reshape(n_tok, 1), inv2, sgn)
    return cos2.reshape(bsz, seq, HEAD_DIM), sin2.reshape(bsz, seq, HEAD_DIM)


def _inproj_kernel(x_ref, mod_ref, g_ref, w_ref, ws_ref, o_ref, os_ref, h_ref, *, d):
    @pl.when(pl.program_id(1) == 0)
    def _():
        x = x_ref[...]
        y = x * lax.rsqrt(jnp.mean(x * x, axis=-1, keepdims=True) + EPS)
        h = (y * g_ref[...]) * (1.0 + mod_ref[:, d:2 * d]) + mod_ref[:, 0:d]
        hb = h.astype(BF16)
        h_ref[...] = hb
        os_ref[...] = _dot(hb, ws_ref[...])

    o_ref[...] = _dot(h_ref[...], w_ref[...]).astype(BF16)


def _input_projection(x2, mod_l, norm_g, w_main, w_small, seq):
    n_tok, d = x2.shape
    ts = min(1024, seq)
    tn = WIDTH
    per_seq = seq // ts
    proj, small = pl.pallas_call(
        functools.partial(_inproj_kernel, d=d),
        out_shape=(jax.ShapeDtypeStruct((n_tok, N_CB * WIDTH), BF16),
                   jax.ShapeDtypeStruct((n_tok, HEAD_DIM), F32)),
        grid=(n_tok // ts, N_CB),
        in_specs=[pl.BlockSpec((ts, d), lambda i, j: (i, 0)),
                  pl.BlockSpec((None, 1, N_MOD * d), lambda i, j: (i // per_seq, 0, 0)),
                  pl.BlockSpec((1, d), lambda i, j: (0, 0)),
                  pl.BlockSpec((d, tn), lambda i, j: (0, j)),
                  pl.BlockSpec((d, HEAD_DIM), lambda i, j: (0, 0))],
        out_specs=(pl.BlockSpec((ts, tn), lambda i, j: (i, j)),
                   pl.BlockSpec((ts, HEAD_DIM), lambda i, j: (i, 0))),
        scratch_shapes=[pltpu.VMEM((ts, d), BF16)],
        compiler_params=_params(2),
        name="in_proj",
    )(x2, mod_l, norm_g.reshape(1, d), w_main, w_small)
    return proj, small


def _deltanet_kernel(q_ref, k_ref, v_ref, z_ref, sm_ref, cw_ref, ab_ref, ng_ref, o_ref,
                     cbuf, s_ref, *, bsz):
    ts = CHUNK
    cw = 3 * WIDTH

    @pl.when(pl.program_id(0) == 0)
    def _():
        cbuf[:, 0:8, :] = jnp.zeros((bsz, 8, cw), F32)
        s_ref[...] = jnp.zeros_like(s_ref)

    row = lax.broadcasted_iota(jnp.int32, (ts, ts), 0)
    col = lax.broadcasted_iota(jnp.int32, (ts, ts), 1)
    same_chunk = (row // DN_CHUNK) == (col // DN_CHUNK)
    mask_incl = jnp.logical_and(same_chunk, row >= col)
    mask_strict = jnp.logical_and(same_chunk, row > col)
    eye = jnp.where(row == col, 1.0, 0.0).astype(F32)
    row_in_chunk = row % DN_CHUNK
    first_chunk = row < DN_CHUNK

    for b in range(bsz):
        cbuf[b, 8:8 + ts, 0:WIDTH] = q_ref[b].astype(F32)
        cbuf[b, 8:8 + ts, WIDTH:2 * WIDTH] = k_ref[b].astype(F32)
        cbuf[b, 8:8 + ts, 2 * WIDTH:3 * WIDTH] = v_ref[b].astype(F32)
        acc = cbuf[b, 8:8 + ts, :] * cw_ref[DN_CONV - 1:DN_CONV, :]
        for kk in range(DN_CONV - 1):
            off = 8 - (DN_CONV - 1) + kk
            acc = acc + cbuf[b, off:off + ts, :] * cw_ref[kk:kk + 1, :]
        cbuf[b, 0:8, :] = cbuf[b, ts:ts + 8, :]
        qkv = _silu(acc)

        sm = sm_ref[b]
        beta_all = _sigmoid(sm)
        g_all = -jnp.exp(ab_ref[0:1, :]) * _softplus(sm + ab_ref[1:2, :])
        for s in (1, 2, 4, 8, 16, 32):
            g_all = g_all + jnp.where(row_in_chunk >= s, pltpu.roll(g_all, s, axis=0), 0.0)

        for h in range(N_HEADS):
            n = b * N_HEADS + h
            lanes = slice(h * HEAD_DIM, (h + 1) * HEAD_DIM)
            gb = jnp.broadcast_to(g_all[:, N_HEADS + h:N_HEADS + h + 1], (ts, ts))
            bb = jnp.broadcast_to(beta_all[:, h:h + 1], (ts, ts))
            gr = gb.T
            dec = jnp.exp(jnp.where(mask_incl, gb - gr, -jnp.inf))
            e_g = jnp.exp(gb)
            gl0 = gb[DN_CHUNK - 1:DN_CHUNK, :]
            gl1 = gb[2 * DN_CHUNK - 1:2 * DN_CHUNK, :]
            e_gl = jnp.exp(jnp.where(first_chunk, gl0, gl1) - gb)

            qh = qkv[:, h * HEAD_DIM:(h + 1) * HEAD_DIM]
            kh = qkv[:, WIDTH + h * HEAD_DIM:WIDTH + (h + 1) * HEAD_DIM]
            vh = qkv[:, 2 * WIDTH + h * HEAD_DIM:2 * WIDTH + (h + 1) * HEAD_DIM]
            qn = qh * (lax.rsqrt(jnp.sum(qh * qh, axis=-1, keepdims=True) + EPS) * HEAD_DIM ** -0.5)
            kn = kh * lax.rsqrt(jnp.sum(kh * kh, axis=-1, keepdims=True) + EPS)
            kb = kn * bb
            knb = kn.astype(BF16)

            lm = jnp.where(mask_strict, _dot_nt(kb.astype(BF16), knb) * dec, 0.0)
            qk = _dot_nt(qn.astype(BF16), knb) * dec
            tm = eye - lm
            mp = lm
            for _ in range(5):
                mp_hi, mp_lo = _split(mp)
                mp = _dot_split(mp_hi, mp_lo, mp_hi, mp_lo)
                tm_hi, tm_lo = _split(tm)
                mp_hi, mp_lo = _split(mp)
                tm = tm + _dot_split(tm_hi, tm_lo, mp_hi, mp_lo)
            rhs = jnp.concatenate([vh * bb, kb * e_g], axis=1).astype(BF16)
            uw = _dot(tm.astype(BF16), rhs)
            u = uw[:, 0:HEAD_DIM]
            w = uw[:, HEAD_DIM:2 * HEAD_DIM].astype(BF16)
            qs = (qn * e_g).astype(BF16)
            ks = (kn * e_gl).astype(BF16)

            state = s_ref[n]
            vns, outs = [], []
            for c, gl in enumerate((gl0, gl1)):
                rows = slice(c * DN_CHUNK, (c + 1) * DN_CHUNK)
                sb = state.astype(BF16)
                vn = u[rows] - _dot(w[rows], sb)
                outs.append(_dot(qs[rows], sb))
                state = state * jnp.exp(gl) + _dot_tn(ks[rows], vn.astype(BF16))
                vns.append(vn)
            s_ref[n] = state
            vn = jnp.concatenate(vns, axis=0)
            o = jnp.concatenate(outs, axis=0) + _dot(qk.astype(BF16), vn.astype(BF16))

            y = o * lax.rsqrt(jnp.mean(o * o, axis=-1, keepdims=True) + EPS) * ng_ref[...]
            y = y * _silu(z_ref[b, :, lanes].astype(F32))
            o_ref[b, :, lanes] = y.astype(BF16)


def _deltanet(proj3, small3, conv_w, a_log, dt_bias, norm_g):
    bsz, seq, _ = proj3.shape
    ts = CHUNK
    ab = jnp.zeros((2, HEAD_DIM), F32)
    ab = ab.at[0, N_HEADS:2 * N_HEADS].set(a_log).at[1, N_HEADS:2 * N_HEADS].set(dt_bias)

    def col(cb):
        return pl.BlockSpec((bsz, ts, WIDTH), lambda s: (0, s, cb))

    return pl.pallas_call(
        functools.partial(_deltanet_kernel, bsz=bsz),
        out_shape=jax.ShapeDtypeStruct((bsz, seq, WIDTH), BF16),
        grid=(seq // ts,),
        in_specs=[col(CB_DQ), col(CB_DK), col(CB_DV), col(CB_DZ),
                  pl.BlockSpec((bsz, ts, HEAD_DIM), lambda s: (0, s, 0)),
                  pl.BlockSpec((DN_CONV, 3 * WIDTH), lambda s: (0, 0)),
                  pl.BlockSpec((2, HEAD_DIM), lambda s: (0, 0)),
                  pl.BlockSpec((1, HEAD_DIM), lambda s: (0, 0))],
        out_specs=pl.BlockSpec((bsz, ts, WIDTH), lambda s: (0, s, 0)),
        scratch_shapes=[pltpu.VMEM((bsz, ts + 8, 3 * WIDTH), F32),
                        pltpu.VMEM((bsz * N_HEADS, HEAD_DIM, HEAD_DIM), F32)],
        compiler_params=_params(1),
        name="deltanet",
    )(proj3, proj3, proj3, proj3, small3, conv_w, ab, norm_g.reshape(1, HEAD_DIM))


def _pool_kernel(p_ref, w_ref, sc_ref, o_ref, pbuf):
    ts = p_ref.shape[0]
    t = pl.program_id(1)

    @pl.when(t == 0)
    def _():
        pbuf[0:POOL_HALO, :] = jnp.zeros((POOL_HALO, WIDTH), F32)

    p = p_ref[...].astype(F32)
    pbuf[POOL_HALO:POOL_HALO + ts, :] = p
    count = (t * ts + 1 + lax.broadcasted_iota(jnp.int32, (ts, HEAD_DIM), 0)).astype(F32)
    for g, win in enumerate(POOL_WINDOWS):
        lanes = slice(g * HEAD_DIM, (g + 1) * HEAD_DIM)
        acc = p[:, lanes]
        for dlt in range(1, win):
            acc = acc + pbuf[POOL_HALO - dlt:POOL_HALO - dlt + ts, lanes]
        pooled = acc / jnp.minimum(count, float(win)) - p[:, lanes]
        y = _dot(pooled.astype(BF16), w_ref[g]) * sc_ref[:, lanes]
        o_ref[:, lanes] = y.astype(BF16)
    pbuf[0:POOL_HALO, :] = pbuf[ts:ts + POOL_HALO, :]


def _pool(proj3, pool_w, pool_scale):
    bsz, seq, _ = proj3.shape
    ts = min(512, seq)
    return pl.pallas_call(
        _pool_kernel,
        out_shape=jax.ShapeDtypeStruct((bsz, seq, WIDTH), BF16),
        grid=(bsz, seq // ts),
        in_specs=[pl.BlockSpec((None, ts, WIDTH), lambda b, t: (b, t, CB_PL)),
                  pl.BlockSpec((N_HEADS, HEAD_DIM, HEAD_DIM), lambda b, t: (0, 0, 0)),
                  pl.BlockSpec((1, WIDTH), lambda b, t: (0, 0))],
        out_specs=pl.BlockSpec((None, ts, WIDTH), lambda b, t: (b, t, 0)),
        scratch_shapes=[pltpu.VMEM((ts + POOL_HALO, WIDTH), F32)],
        compiler_params=_params(2),
        name="pool",
    )(proj3, pool_w.astype(BF16), pool_scale.reshape(1, WIDTH))


def _sgu_kernel(u_ref, v_ref, lng_ref, lnb_ref, w_ref, bias_ref, o_ref):
    ts = u_ref.shape[0]
    u = _gelu(u_ref[...].astype(F32))
    v = _gelu(v_ref[...].astype(F32))
    mu = jnp.mean(v, axis=-1, keepdims=True)
    vc = v - mu
    var = jnp.mean(vc * vc, axis=-1, keepdims=True)
    vb = (vc * lax.rsqrt(var + EPS) * lng_ref[...] + lnb_ref[...]).astype(BF16)
    row = lax.broadcasted_iota(jnp.int32, (CHUNK, CHUNK), 0)
    col = lax.broadcasted_iota(jnp.int32, (CHUNK, CHUNK), 1)
    for g in range(N_HEADS):
        lanes = slice(g * HEAD_DIM, (g + 1) * HEAD_DIM)
        wg = jnp.where(row >= col, w_ref[g], 0.0).astype(BF16)
        for c in range(ts // CHUNK):
            rows = slice(c * CHUNK, (c + 1) * CHUNK)
            mixed = _dot(wg, vb[rows, lanes]) + bias_ref[:, lanes]
            o_ref[rows, lanes] = (u[rows, lanes] * mixed).astype(BF16)


def _sgu(proj2, ln_g, ln_b, sg_w, sg_b):
    n_tok = proj2.shape[0]
    ts = min(512, n_tok)
    bias = jnp.repeat(sg_b.T, HEAD_DIM, axis=1)
    return pl.pallas_call(
        _sgu_kernel,
        out_shape=jax.ShapeDtypeStruct((n_tok, WIDTH), BF16),
        grid=(n_tok // ts,),
        in_specs=[pl.BlockSpec((ts, WIDTH), lambda i: (i, CB_SU)),
                  pl.BlockSpec((ts, WIDTH), lambda i: (i, CB_SV)),
                  pl.BlockSpec((1, WIDTH), lambda i: (0, 0)),
                  pl.BlockSpec((1, WIDTH), lambda i: (0, 0)),
                  pl.BlockSpec((N_HEADS, CHUNK, CHUNK), lambda i: (0, 0, 0)),
                  pl.BlockSpec((CHUNK, WIDTH), lambda i: (0, 0))],
        out_specs=pl.BlockSpec((ts, WIDTH), lambda i: (i, 0)),
        compiler_params=_params(1),
        name="spatial_gating",
    )(proj2, proj2, ln_g.reshape(1, WIDTH), ln_b.reshape(1, WIDTH), sg_w, bias)


def _retention_kernel(q_ref, k_ref, v_ref, g_ref, cos_ref, sin_ref, gn_ref, o_ref, s_ref, *, bsz):
    ts = CHUNK

    @pl.when(pl.program_id(0) == 0)
    def _():
        s_ref[...] = jnp.zeros_like(s_ref)

    row = lax.broadcasted_iota(jnp.int32, (ts, ts), 0)
    col = lax.broadcasted_iota(jnp.int32, (ts, ts), 1)
    rel = (row - col).astype(F32)
    rowf = row.astype(F32)
    for b in range(bsz):
        cos2 = cos_ref[b]
        sin2 = sin_ref[b]
        for h in range(N_HEADS):
            n = b * N_HEADS + h
            lanes = slice(h * HEAD_DIM, (h + 1) * HEAD_DIM)
            log_gamma = math.log1p(-2.0 ** (-5.0 - h))
            decay = jnp.where(row >= col, jnp.exp(log_gamma * jnp.maximum(rel, 0.0)), 0.0)
            zeta = jnp.exp(log_gamma * (float(ts - 1) - rowf))
            xi = jnp.exp(log_gamma * (rowf + 1.0))
            gamma_c = math.exp(log_gamma * ts)

            q = q_ref[b, :, lanes].astype(F32)
            k = k_ref[b, :, lanes].astype(F32)
            q = q * cos2 + pltpu.roll(q, HEAD_DIM // 2, axis=1) * sin2
            k = (k * cos2 + pltpu.roll(k, HEAD_DIM // 2, axis=1) * sin2) * HEAD_DIM ** -0.5
            vb = v_ref[b, :, lanes]
            qb = q.astype(BF16)
            scores = _dot_nt(qb, k.astype(BF16)) * decay
            prev = s_ref[n]
            o = _dot(scores.astype(BF16), vb) + _dot(qb, prev.astype(BF16)) * xi
            s_ref[n] = prev * gamma_c + _dot_tn((k * zeta).astype(BF16), vb)

            mu = jnp.mean(o, axis=-1, keepdims=True)
            oc = o - mu
            var = jnp.mean(oc * oc, axis=-1, keepdims=True)
            y = oc * lax.rsqrt(var + EPS) * gn_ref[:, lanes]
            o_ref[b, :, lanes] = (_silu(g_ref[b, :, lanes].astype(F32)) * y).astype(BF16)


def _retention(proj3, cos2, sin2, gn_g):
    bsz, seq, _ = proj3.shape
    ts = CHUNK

    def col(cb):
        return pl.BlockSpec((bsz, ts, WIDTH), lambda s: (0, s, cb))

    tab = pl.BlockSpec((bsz, ts, HEAD_DIM), lambda s: (0, s, 0))
    return pl.pallas_call(
        functools.partial(_retention_kernel, bsz=bsz),
        out_shape=jax.ShapeDtypeStruct((bsz, seq, WIDTH), BF16),
        grid=(seq // ts,),
        in_specs=[col(CB_RQ), col(CB_RK), col(CB_RV), col(CB_RG), tab, tab,
                  pl.BlockSpec((1, WIDTH), lambda s: (0, 0))],
        out_specs=pl.BlockSpec((bsz, ts, WIDTH), lambda s: (0, s, 0)),
        scratch_shapes=[pltpu.VMEM((bsz * N_HEADS, HEAD_DIM, HEAD_DIM), F32)],
        compiler_params=_params(1),
        name="retention",
    )(proj3, proj3, proj3, proj3, cos2, sin2, gn_g.reshape(1, WIDTH))


def _merge_kernel(ya_ref, yb_ref, yc_ref, yd_ref, gt_ref, wbr_ref, wo_ref, x_ref, mod_ref, o_ref, *, d):
    merged = None
    for i, y_ref in enumerate((ya_ref, yb_ref, yc_ref, yd_ref)):
        gate = _sigmoid(gt_ref[:, i * d:(i + 1) * d].astype(F32))
        term = gate * _dot(y_ref[...], wbr_ref[i])
        merged = term if merged is None else merged + term
    r = _dot(merged.astype(BF16), wo_ref[...])
    o_ref[...] = x_ref[...] + mod_ref[:, 2 * d:3 * d] * r


def _merge(ys, proj2, w_br, w_out, x2, mod_l, seq):
    n_tok, d = x2.shape
    ts = min(512, seq)
    per_seq = seq // ts
    yspec = pl.BlockSpec((ts, WIDTH), lambda i: (i, 0))
    return pl.pallas_call(
        functools.partial(_merge_kernel, d=d),
        out_shape=jax.ShapeDtypeStruct((n_tok, d), F32),
        grid=(n_tok // ts,),
        in_specs=[yspec, yspec, yspec, yspec,
                  pl.BlockSpec((ts, N_BRANCH * d), lambda i: (i, CB_GATES)),
                  pl.BlockSpec((N_BRANCH, WIDTH, d), lambda i: (0, 0, 0)),
                  pl.BlockSpec((d, d), lambda i: (0, 0)),
                  pl.BlockSpec((ts, d), lambda i: (i, 0)),
                  pl.BlockSpec((None, 1, N_MOD * d), lambda i: (i // per_seq, 0, 0))],
        out_specs=pl.BlockSpec((ts, d), lambda i: (i, 0)),
        compiler_params=_params(1),
        name="merge_out",
    )(*ys, proj2, w_br, w_out, x2, mod_l)


def _mlp_kernel(x_ref, mod_ref, g_ref, w1_ref, w2_ref, fg_ref, o_ref, h_ref, acc_ref, *, d, final_norm):
    f = pl.program_id(1)

    @pl.when(f == 0)
    def _():
        x = x_ref[...]
        y = x * lax.rsqrt(jnp.mean(x * x, axis=-1, keepdims=True) + EPS)
        h = (y * g_ref[...]) * (1.0 + mod_ref[:, 4 * d:5 * d]) + mod_ref[:, 3 * d:4 * d]
        h_ref[...] = h.astype(BF16)
        acc_ref[...] = jnp.zeros_like(acc_ref)

    a = jnp.maximum(_dot(h_ref[...], w1_ref[...]), 0.0)
    acc_ref[...] += _dot((a * a).astype(BF16), w2_ref[...])

    @pl.when(f == pl.num_programs(1) - 1)
    def _():
        out = x_ref[...] + mod_ref[:, 5 * d:6 * d] * acc_ref[...]
        if final_norm:
            out = out * lax.rsqrt(jnp.mean(out * out, axis=-1, keepdims=True) + EPS) * fg_ref[...]
        o_ref[...] = out


def _mlp(x2, mod_l, norm_g, w1, w2, final_g, seq, final_norm):
    n_tok, d = x2.shape
    d_ff = w1.shape[1]
    ts = min(1024, seq)
    tf = min(512, d_ff)
    per_seq = seq // ts
    return pl.pallas_call(
        functools.partial(_mlp_kernel, d=d, final_norm=final_norm),
        out_shape=jax.ShapeDtypeStruct((n_tok, d), F32),
        grid=(n_tok // ts, d_ff // tf),
        in_specs=[pl.BlockSpec((ts, d), lambda i, f: (i, 0)),
                  pl.BlockSpec((None, 1, N_MOD * d), lambda i, f: (i // per_seq, 0, 0)),
                  pl.BlockSpec((1, d), lambda i, f: (0, 0)),
                  pl.BlockSpec((d, tf), lambda i, f: (0, f)),
                  pl.BlockSpec((tf, d), lambda i, f: (f, 0)),
                  pl.BlockSpec((1, d), lambda i, f: (0, 0))],
        out_specs=pl.BlockSpec((ts, d), lambda i, f: (i, 0)),
        scratch_shapes=[pltpu.VMEM((ts, d), BF16), pltpu.VMEM((ts, d), F32)],
        compiler_params=_params(2),
        name="mlp",
    )(x2, mod_l, norm_g.reshape(1, d), w1, w2, final_g.reshape(1, d))


def _permute_w_in(w_in_l):
    d = w_in_l.shape[0]
    w = WIDTH
    o_small = 4 * w
    o_rest = o_small + 2 * N_HEADS
    o_gates = o_rest + 7 * w
    main = jnp.concatenate([w_in_l[:, o_gates:], w_in_l[:, :o_small], w_in_l[:, o_rest:o_gates]], axis=1)
    small = jnp.zeros((d, HEAD_DIM), F32).at[:, :2 * N_HEADS].set(w_in_l[:, o_small:o_rest])
    return main.astype(BF16), small.astype(BF16)


def kernel(x, c, positions, norm1_g, norm2_g, ada_w, ada_b, w_in, dn_conv_w, dn_a_log, dn_dt_bias, dn_norm_g, pool_w, pool_scale, sg_ln_g, sg_ln_b, sg_w, sg_b, ret_gn_g, w_br_dn, w_br_pool, w_br_sg, w_br_ret, w_out, mlp_w1, mlp_w2, final_g):
    bsz, seq, d = x.shape
    n_layers = w_in.shape[0]
    n_tok = bsz * seq
    mod = _modulation(c, ada_w, ada_b)
    cos2, sin2 = _rope_tables(positions)
    x2 = x.reshape(n_tok, d)
    for l in range(n_layers):
        mod_l = mod[l].reshape(bsz, 1, N_MOD * d)
        w_main, w_small = _permute_w_in(w_in[l])
        proj2, small2 = _input_projection(x2, mod_l, norm1_g[l], w_main, w_small, seq)
        proj3 = proj2.reshape(bsz, seq, N_CB * WIDTH)
        small3 = small2.reshape(bsz, seq, HEAD_DIM)
        y_a = _deltanet(proj3, small3, dn_conv_w[l], dn_a_log[l], dn_dt_bias[l], dn_norm_g[l])
        y_b = _pool(proj3, pool_w[l], pool_scale[l])
        y_c = _sgu(proj2, sg_ln_g[l], sg_ln_b[l], sg_w[l], sg_b[l])
        y_d = _retention(proj3, cos2, sin2, ret_gn_g[l])
        ys = [y.reshape(n_tok, WIDTH) for y in (y_a, y_b, y_c, y_d)]
        w_br = jnp.stack([w_br_dn[l], w_br_pool[l], w_br_sg[l], w_br_ret[l]]).astype(BF16)
        x2 = _merge(ys, proj2, w_br, w_out[l].astype(BF16), x2, mod_l, seq)
        x2 = _mlp(x2, mod_l, norm2_g[l], mlp_w1[l].astype(BF16), mlp_w2[l].astype(BF16), final_g, seq,
                  final_norm=(l == n_layers - 1))
    return x2.reshape(bsz, seq, d)
```

```python
import functools
import math

import jax
import jax.numpy as jnp
from jax import lax
from jax.experimental import pallas as pl
from jax.experimental.pallas import tpu as pltpu

F32 = jnp.float32
BF16 = jnp.bfloat16

EPS = 1e-6
N_HEADS = 4
HEAD_DIM = 128
WIDTH = N_HEADS * HEAD_DIM
N_BRANCH = 4
N_MOD = 6
DN_CONV = 4
DN_CHUNK = 64
CHUNK = 128
POOL_WINDOWS = (2, 4, 8, 16)
POOL_HALO = 16
ROPE_BASE = 10000.0
VMEM_LIMIT = 48 * 1024 * 1024

CB_GATES = 0
CB_DQ, CB_DK, CB_DV, CB_DZ, CB_PL, CB_SU, CB_SV, CB_RQ, CB_RK, CB_RV, CB_RG = range(8, 19)
N_CB = 19


def _sigmoid(x):
    return 1.0 / (1.0 + jnp.exp(-x))


def _silu(x):
    return x * _sigmoid(x)


def _softplus(x):
    return jnp.maximum(x, 0.0) + jnp.log1p(jnp.exp(-jnp.abs(x)))


def _gelu(x):
    return 0.5 * x * (1.0 + lax.erf(x * (1.0 / math.sqrt(2.0))))


def _dot(a, b):
    return jnp.dot(a, b, preferred_element_type=F32)


def _dot_nt(a, b):
    return lax.dot_general(a, b, (((1,), (1,)), ((), ())), preferred_element_type=F32)


def _dot_tn(a, b):
    return lax.dot_general(a, b, (((0,), (0,)), ((), ())), preferred_element_type=F32)


def _params(n_axes):
    return pltpu.CompilerParams(dimension_semantics=("arbitrary",) * n_axes,
                                vmem_limit_bytes=VMEM_LIMIT)


def _mod_kernel(c_ref, w_ref, b_ref, o_ref):
    cond = _silu(c_ref[...])
    o_ref[...] = _dot(cond.astype(BF16), w_ref[...].astype(BF16)) + b_ref[...]


def _modulation(c, ada_w, ada_b):
    n_layers, d, n_out = ada_w.shape
    bsz = c.shape[0]
    rows = 8
    c_pad = jnp.zeros((rows, d), F32).at[:bsz].set(c)
    tn = n_out // 4
    out = pl.pallas_call(
        _mod_kernel,
        out_shape=jax.ShapeDtypeStruct((n_layers, rows, n_out), F32),
        grid=(n_layers, n_out // tn),
        in_specs=[pl.BlockSpec((rows, d), lambda l, j: (0, 0)),
                  pl.BlockSpec((None, d, tn), lambda l, j: (l, 0, j)),
                  pl.BlockSpec((None, 1, tn), lambda l, j: (l, 0, j))],
        out_specs=pl.BlockSpec((None, rows, tn), lambda l, j: (l, 0, j)),
        compiler_params=_params(2),
        name="adaln_mod",
    )(c_pad, ada_w, ada_b.reshape(n_layers, 1, n_out))
    return out[:, :bsz]


def _rope_kernel(pos_ref, inv_ref, sgn_ref, cos_ref, sin_ref):
    ang = pos_ref[...].astype(F32) * inv_ref[...]
    cos_ref[...] = jnp.cos(ang)
    sin_ref[...] = jnp.sin(ang) * sgn_ref[...]


def _rope_tables(positions):
    bsz, seq = positions.shape
    half = HEAD_DIM // 2
    inv = ROPE_BASE ** (-jnp.arange(0, HEAD_DIM, 2, dtype=F32) / HEAD_DIM)
    inv2 = jnp.concatenate([inv, inv]).reshape(1, HEAD_DIM)
    sgn = jnp.concatenate([-jnp.ones((half,), F32), jnp.ones((half,), F32)]).reshape(1, HEAD_DIM)
    n_tok = bsz * seq
    ts = min(1024, n_tok)
    shp = jax.ShapeDtypeStruct((n_tok, HEAD_DIM), F32)
    cos2, sin2 = pl.pallas_call(
        _rope_kernel,
        out_shape=(shp, shp),
        grid=(n_tok // ts,),
        in_specs=[pl.BlockSpec((ts, 1), lambda i: (i, 0)),
                  pl.BlockSpec((1, HEAD_DIM), lambda i: (0, 0)),
                  pl.BlockSpec((1, HEAD_DIM), lambda i: (0, 0))],
        out_specs=(pl.BlockSpec((ts, HEAD_DIM), lambda i: (i, 0)),
                   pl.BlockSpec((ts, HEAD_DIM), lambda i: (i, 0))),
        compiler_params=_params(1),
        name="rope_tables",
    )(positions.reshape(n_tok, 1), inv2, sgn)
    return cos2.reshape(bsz, seq, HEAD_DIM), sin2.reshape(bsz, seq, HEAD_DIM)


def _inproj_kernel(x_ref, mod_ref, g_ref, w_ref, ws_ref, o_ref, os_ref, h_ref, *, d):
    @pl.when(pl.program_id(1) == 0)
    def _():
        x = x_ref[...]
        y = x * lax.rsqrt(jnp.mean(x * x, axis=-1, keepdims=True) + EPS)
        h = (y * g_ref[...]) * (1.0 + mod_ref[:, d:2 * d]) + mod_ref[:, 0:d]
        hb = h.astype(BF16)
        h_ref[...] = hb
        os_ref[...] = _dot(hb, ws_ref[...])

    o_ref[...] = _dot(h_ref[...], w_ref[...]).astype(BF16)


def _input_projection(x2, mod_l, norm_g, w_main, w_small, layer, seq):
    n_tok, d = x2.shape
    ts = min(2048, seq)
    tn = WIDTH
    per_seq = seq // ts
    proj, small = pl.pallas_call(
        functools.partial(_inproj_kernel, d=d),
        out_shape=(jax.ShapeDtypeStruct((n_tok, N_CB * WIDTH), BF16),
                   jax.ShapeDtypeStruct((n_tok, HEAD_DIM), F32)),
        grid=(n_tok // ts, N_CB),
        in_specs=[pl.BlockSpec((ts, d), lambda i, j: (i, 0)),
                  pl.BlockSpec((None, 1, N_MOD * d), lambda i, j: (i // per_seq, 0, 0)),
                  pl.BlockSpec((1, d), lambda i, j: (0, 0)),
                  pl.BlockSpec((None, d, tn), lambda i, j: (layer, 0, j)),
                  pl.BlockSpec((None, d, HEAD_DIM), lambda i, j: (layer, 0, 0))],
        out_specs=(pl.BlockSpec((ts, tn), lambda i, j: (i, j)),
                   pl.BlockSpec((ts, HEAD_DIM), lambda i, j: (i, 0))),
        scratch_shapes=[pltpu.VMEM((ts, d), BF16)],
        compiler_params=_params(2),
        name="in_proj",
    )(x2, mod_l, norm_g.reshape(1, d), w_main, w_small)
    return proj, small


def _deltanet_kernel(q_ref, k_ref, v_ref, z_ref, sm_ref, cw_ref, ab_ref, ng_ref, o_ref,
                     cbuf, s_ref, *, bsz):
    ts = CHUNK
    cw = 3 * WIDTH
    inst = [(b, h) for b in range(bsz) for h in range(N_HEADS)]

    @pl.when(pl.program_id(0) == 0)
    def _():
        cbuf[:, 0:8, :] = jnp.zeros((bsz, 8, cw), F32)
        s_ref[...] = jnp.zeros_like(s_ref)

    row = lax.broadcasted_iota(jnp.int32, (ts, ts), 0)
    col = lax.broadcasted_iota(jnp.int32, (ts, ts), 1)

    def same_block(size):
        return (row // size) == (col // size)

    mask_incl = jnp.logical_and(same_block(DN_CHUNK), row >= col)
    mask_strict = jnp.logical_and(same_block(DN_CHUNK), row > col)
    eye = jnp.where(row == col, 1.0, 0.0).astype(F32)
    row_in_chunk = row % DN_CHUNK
    first_chunk = row < DN_CHUNK

    qkv, g_all, beta_all = [], [], []
    for b in range(bsz):
        cbuf[b, 8:8 + ts, 0:WIDTH] = q_ref[b].astype(F32)
        cbuf[b, 8:8 + ts, WIDTH:2 * WIDTH] = k_ref[b].astype(F32)
        cbuf[b, 8:8 + ts, 2 * WIDTH:3 * WIDTH] = v_ref[b].astype(F32)
        acc = cbuf[b, 8:8 + ts, :] * cw_ref[DN_CONV - 1:DN_CONV, :]
        for kk in range(DN_CONV - 1):
            off = 8 - (DN_CONV - 1) + kk
            acc = acc + cbuf[b, off:off + ts, :] * cw_ref[kk:kk + 1, :]
        cbuf[b, 0:8, :] = cbuf[b, ts:ts + 8, :]
        qkv.append(_silu(acc))

        sm = sm_ref[b]
        beta_all.append(_sigmoid(sm))
        g = -jnp.exp(ab_ref[0:1, :]) * _softplus(sm + ab_ref[1:2, :])
        for s in (1, 2, 4, 8, 16, 32):
            g = g + jnp.where(row_in_chunk >= s, pltpu.roll(g, s, axis=0), 0.0)
        g_all.append(g)

    gb = [jnp.broadcast_to(g_all[b][:, N_HEADS + h:N_HEADS + h + 1], (ts, ts)) for b, h in inst]
    bb = [jnp.broadcast_to(beta_all[b][:, h:h + 1], (ts, ts)) for b, h in inst]
    dec = [jnp.exp(jnp.where(mask_incl, g - g.T, -jnp.inf)) for g in gb]
    e_g = [jnp.exp(g) for g in gb]
    gl = [(g[DN_CHUNK - 1:DN_CHUNK, :], g[2 * DN_CHUNK - 1:2 * DN_CHUNK, :]) for g in gb]
    e_gl = [jnp.exp(jnp.where(first_chunk, l0, l1) - g) for g, (l0, l1) in zip(gb, gl)]

    def head(b, h, part):
        lo = part * WIDTH + h * HEAD_DIM
        return qkv[b][:, lo:lo + HEAD_DIM]

    def l2n(t):
        return t * lax.rsqrt(jnp.sum(t * t, axis=-1, keepdims=True) + EPS)

    qn = [l2n(head(b, h, 0)) * HEAD_DIM ** -0.5 for b, h in inst]
    kn = [l2n(head(b, h, 1)) for b, h in inst]
    kb = [k * bt for k, bt in zip(kn, bb)]
    knb = [k.astype(BF16) for k in kn]
    lm = [jnp.where(mask_strict, _dot_nt(a.astype(BF16), k) * d, 0.0) for a, k, d in zip(kb, knb, dec)]
    qk = [(_dot_nt(q.astype(BF16), k) * d).astype(BF16) for q, k, d in zip(qn, knb, dec)]

    l8 = [jnp.where(same_block(8), m, 0.0) for m in lm]
    l8b = [m.astype(BF16) for m in l8]
    p1b = [_dot(m, m).astype(BF16) for m in l8b]
    tm = [eye - m for m in l8]
    tm = [t + _dot(t.astype(BF16), p) for t, p in zip(tm, p1b)]
    p2b = [_dot(p, p).astype(BF16) for p in p1b]
    tm = [t + _dot(t.astype(BF16), p) for t, p in zip(tm, p2b)]
    for size in (8, 16, 32):
        lower_left = jnp.logical_and(same_block(2 * size), jnp.logical_not(same_block(size)))
        cm = [jnp.where(lower_left, m, 0.0).astype(BF16) for m in lm]
        tb = [t.astype(BF16) for t in tm]
        xm = [_dot(c, t).astype(BF16) for c, t in zip(cm, tb)]
        tm = [t - _dot(t16, x) for t, t16, x in zip(tm, tb, xm)]

    rhs = [jnp.concatenate([head(b, h, 2) * bt, k * e], axis=1).astype(BF16)
           for (b, h), bt, k, e in zip(inst, bb, kb, e_g)]
    uw = [_dot(t.astype(BF16), r) for t, r in zip(tm, rhs)]
    u = [m[:, 0:HEAD_DIM] for m in uw]
    w = [m[:, HEAD_DIM:2 * HEAD_DIM].astype(BF16) for m in uw]
    qs = [(q * e).astype(BF16) for q, e in zip(qn, e_g)]
    ks = [(k * e).astype(BF16) for k, e in zip(kn, e_gl)]

    state = [s_ref[n] for n in range(len(inst))]
    vns, outs = [], []
    for c in range(2):
        rows = slice(c * DN_CHUNK, (c + 1) * DN_CHUNK)
        sb = [s.astype(BF16) for s in state]
        vn = [a[rows] - _dot(wc[rows], s) for a, wc, s in zip(u, w, sb)]
        outs.append([_dot(q[rows], s) for q, s in zip(qs, sb)])
        state = [s * jnp.exp(g[c]) + _dot_tn(k[rows], v.astype(BF16))
                 for s, g, k, v in zip(state, gl, ks, vn)]
        vns.append(vn)
    for n, s in enumerate(state):
        s_ref[n] = s

    for n, (b, h) in enumerate(inst):
        lanes = slice(h * HEAD_DIM, (h + 1) * HEAD_DIM)
        vn = jnp.concatenate([vns[0][n], vns[1][n]], axis=0).astype(BF16)
        o = jnp.concatenate([outs[0][n], outs[1][n]], axis=0) + _dot(qk[n], vn)
        y = o * lax.rsqrt(jnp.mean(o * o, axis=-1, keepdims=True) + EPS) * ng_ref[...]
        y = y * _silu(z_ref[b, :, lanes].astype(F32))
        o_ref[b, :, lanes] = y.astype(BF16)


def _deltanet(proj3, small3, conv_w, a_log, dt_bias, norm_g):
    bsz, seq, _ = proj3.shape
    ts = CHUNK
    ab = jnp.zeros((2, HEAD_DIM), F32)
    ab = ab.at[0, N_HEADS:2 * N_HEADS].set(a_log).at[1, N_HEADS:2 * N_HEADS].set(dt_bias)

    def col(cb):
        return pl.BlockSpec((bsz, ts, WIDTH), lambda s: (0, s, cb))

    return pl.pallas_call(
        functools.partial(_deltanet_kernel, bsz=bsz),
        out_shape=jax.ShapeDtypeStruct((bsz, seq, WIDTH), BF16),
        grid=(seq // ts,),
        in_specs=[col(CB_DQ), col(CB_DK), col(CB_DV), col(CB_DZ),
                  pl.BlockSpec((bsz, ts, HEAD_DIM), lambda s: (0, s, 0)),
                  pl.BlockSpec((DN_CONV, 3 * WIDTH), lambda s: (0, 0)),
                  pl.BlockSpec((2, HEAD_DIM), lambda s: (0, 0)),
                  pl.BlockSpec((1, HEAD_DIM), lambda s: (0, 0))],
        out_specs=pl.BlockSpec((bsz, ts, WIDTH), lambda s: (0, s, 0)),
        scratch_shapes=[pltpu.VMEM((bsz, ts + 8, 3 * WIDTH), F32),
                        pltpu.VMEM((bsz * N_HEADS, HEAD_DIM, HEAD_DIM), F32)],
        compiler_params=_params(1),
        name="deltanet",
    )(proj3, proj3, proj3, proj3, small3, conv_w, ab, norm_g.reshape(1, HEAD_DIM))


def _pool_kernel(p_ref, w_ref, sc_ref, o_ref, pbuf):
    ts = p_ref.shape[0]
    t = pl.program_id(1)

    @pl.when(t == 0)
    def _():
        pbuf[0:POOL_HALO, :] = jnp.zeros((POOL_HALO, WIDTH), F32)

    p = p_ref[...].astype(F32)
    pbuf[POOL_HALO:POOL_HALO + ts, :] = p
    count = (t * ts + 1 + lax.broadcasted_iota(jnp.int32, (ts, HEAD_DIM), 0)).astype(F32)
    for g, win in enumerate(POOL_WINDOWS):
        lanes = slice(g * HEAD_DIM, (g + 1) * HEAD_DIM)
        acc = p[:, lanes]
        for dlt in range(1, win):
            acc = acc + pbuf[POOL_HALO - dlt:POOL_HALO - dlt + ts, lanes]
        pooled = acc / jnp.minimum(count, float(win)) - p[:, lanes]
        y = _dot(pooled.astype(BF16), w_ref[g]) * sc_ref[:, lanes]
        o_ref[:, lanes] = y.astype(BF16)
    pbuf[0:POOL_HALO, :] = pbuf[ts:ts + POOL_HALO, :]


def _pool(proj3, pool_w, pool_scale):
    bsz, seq, _ = proj3.shape
    ts = min(512, seq)
    return pl.pallas_call(
        _pool_kernel,
        out_shape=jax.ShapeDtypeStruct((bsz, seq, WIDTH), BF16),
        grid=(bsz, seq // ts),
        in_specs=[pl.BlockSpec((None, ts, WIDTH), lambda b, t: (b, t, CB_PL)),
                  pl.BlockSpec((N_HEADS, HEAD_DIM, HEAD_DIM), lambda b, t: (0, 0, 0)),
                  pl.BlockSpec((1, WIDTH), lambda b, t: (0, 0))],
        out_specs=pl.BlockSpec((None, ts, WIDTH), lambda b, t: (b, t, 0)),
        scratch_shapes=[pltpu.VMEM((ts + POOL_HALO, WIDTH), F32)],
        compiler_params=_params(2),
        name="pool",
    )(proj3, pool_w.astype(BF16), pool_scale.reshape(1, WIDTH))


def _sgu_kernel(u_ref, v_ref, lng_ref, lnb_ref, w_ref, bias_ref, o_ref):
    ts = u_ref.shape[0]
    u = _gelu(u_ref[...].astype(F32))
    v = _gelu(v_ref[...].astype(F32))
    mu = jnp.mean(v, axis=-1, keepdims=True)
    vc = v - mu
    var = jnp.mean(vc * vc, axis=-1, keepdims=True)
    vb = (vc * lax.rsqrt(var + EPS) * lng_ref[...] + lnb_ref[...]).astype(BF16)
    row = lax.broadcasted_iota(jnp.int32, (CHUNK, CHUNK), 0)
    col = lax.broadcasted_iota(jnp.int32, (CHUNK, CHUNK), 1)
    for g in range(N_HEADS):
        lanes = slice(g * HEAD_DIM, (g + 1) * HEAD_DIM)
        wg = jnp.where(row >= col, w_ref[g], 0.0).astype(BF16)
        for c in range(ts // CHUNK):
            rows = slice(c * CHUNK, (c + 1) * CHUNK)
            mixed = _dot(wg, vb[rows, lanes]) + bias_ref[:, lanes]
            o_ref[rows, lanes] = (u[rows, lanes] * mixed).astype(BF16)


def _sgu(proj2, ln_g, ln_b, sg_w, sg_b):
    n_tok = proj2.shape[0]
    ts = min(512, n_tok)
    bias = jnp.repeat(sg_b.T, HEAD_DIM, axis=1)
    return pl.pallas_call(
        _sgu_kernel,
        out_shape=jax.ShapeDtypeStruct((n_tok, WIDTH), BF16),
        grid=(n_tok // ts,),
        in_specs=[pl.BlockSpec((ts, WIDTH), lambda i: (i, CB_SU)),
                  pl.BlockSpec((ts, WIDTH), lambda i: (i, CB_SV)),
                  pl.BlockSpec((1, WIDTH), lambda i: (0, 0)),
                  pl.BlockSpec((1, WIDTH), lambda i: (0, 0)),
                  pl.BlockSpec((N_HEADS, CHUNK, CHUNK), lambda i: (0, 0, 0)),
                  pl.BlockSpec((CHUNK, WIDTH), lambda i: (0, 0))],
        out_specs=pl.BlockSpec((ts, WIDTH), lambda i: (i, 0)),
        compiler_params=_params(1),
        name="spatial_gating",
    )(proj2, proj2, ln_g.reshape(1, WIDTH), ln_b.reshape(1, WIDTH), sg_w, bias)


def _retention_kernel(q_ref, k_ref, v_ref, g_ref, cos_ref, sin_ref, gn_ref, o_ref, s_ref, c_ref, *, bsz):
    ts = CHUNK
    nh = N_HEADS
    inst = [(b, h) for b in range(bsz) for h in range(nh)]
    log_gamma = [math.log1p(-2.0 ** (-5.0 - h)) for h in range(nh)]

    @pl.when(pl.program_id(0) == 0)
    def _():
        s_ref[...] = jnp.zeros_like(s_ref)
        row = lax.broadcasted_iota(jnp.int32, (ts, ts), 0)
        col = lax.broadcasted_iota(jnp.int32, (ts, ts), 1)
        rel = (row - col).astype(F32)
        rowf = row.astype(F32)
        for h in range(nh):
            c_ref[h] = jnp.where(row >= col, jnp.exp(log_gamma[h] * jnp.maximum(rel, 0.0)), 0.0)
            c_ref[nh + h] = jnp.exp(log_gamma[h] * (float(ts - 1) - rowf))
            c_ref[2 * nh + h] = jnp.exp(log_gamma[h] * (rowf + 1.0))

    def lanes(h):
        return slice(h * HEAD_DIM, (h + 1) * HEAD_DIM)

    def rotary(ref, b, h):
        t = ref[b, :, lanes(h)].astype(F32)
        return t * cos_ref[b] + pltpu.roll(t, HEAD_DIM // 2, axis=1) * sin_ref[b]

    q = [rotary(q_ref, b, h).astype(BF16) for b, h in inst]
    k = [rotary(k_ref, b, h) * HEAD_DIM ** -0.5 for b, h in inst]
    v = [v_ref[b, :, lanes(h)] for b, h in inst]
    scores = [(_dot_nt(qn, kn.astype(BF16)) * c_ref[h]).astype(BF16) for qn, kn, (b, h) in zip(q, k, inst)]
    prev = [s_ref[n] for n in range(len(inst))]
    o = [_dot(sc, vn) + _dot(qn, p.astype(BF16)) * c_ref[2 * nh + h]
         for sc, vn, qn, p, (b, h) in zip(scores, v, q, prev, inst)]
    for n, (b, h) in enumerate(inst):
        kv = _dot_tn((k[n] * c_ref[nh + h]).astype(BF16), v[n])
        s_ref[n] = prev[n] * math.exp(log_gamma[h] * ts) + kv

    for n, (b, h) in enumerate(inst):
        mu = jnp.mean(o[n], axis=-1, keepdims=True)
        oc = o[n] - mu
        var = jnp.mean(oc * oc, axis=-1, keepdims=True)
        y = oc * lax.rsqrt(var + EPS) * gn_ref[:, lanes(h)]
        o_ref[b, :, lanes(h)] = (_silu(g_ref[b, :, lanes(h)].astype(F32)) * y).astype(BF16)


def _retention(proj3, cos2, sin2, gn_g):
    bsz, seq, _ = proj3.shape
    ts = CHUNK

    def col(cb):
        return pl.BlockSpec((bsz, ts, WIDTH), lambda s: (0, s, cb))

    tab = pl.BlockSpec((bsz, ts, HEAD_DIM), lambda s: (0, s, 0))
    return pl.pallas_call(
        functools.partial(_retention_kernel, bsz=bsz),
        out_shape=jax.ShapeDtypeStruct((bsz, seq, WIDTH), BF16),
        grid=(seq // ts,),
        in_specs=[col(CB_RQ), col(CB_RK), col(CB_RV), col(CB_RG), tab, tab,
                  pl.BlockSpec((1, WIDTH), lambda s: (0, 0))],
        out_specs=pl.BlockSpec((bsz, ts, WIDTH), lambda s: (0, s, 0)),
        scratch_shapes=[pltpu.VMEM((bsz * N_HEADS, HEAD_DIM, HEAD_DIM), F32),
                        pltpu.VMEM((3 * N_HEADS, CHUNK, CHUNK), F32)],
        compiler_params=_params(1),
        name="retention",
    )(proj3, proj3, proj3, proj3, cos2, sin2, gn_g.reshape(1, WIDTH))


def _merge_kernel(ya_ref, yb_ref, yc_ref, yd_ref, gt_ref, wbr_ref, wo_ref, x_ref, mod_ref, o_ref, *, d):
    merged = None
    for i, y_ref in enumerate((ya_ref, yb_ref, yc_ref, yd_ref)):
        gate = _sigmoid(gt_ref[:, i * d:(i + 1) * d].astype(F32))
        term = gate * _dot(y_ref[...], wbr_ref[i])
        merged = term if merged is None else merged + term
    r = _dot(merged.astype(BF16), wo_ref[...])
    o_ref[...] = x_ref[...] + mod_ref[:, 2 * d:3 * d] * r


def _merge(ys, proj2, w_br, w_out, x2, mod_l, layer, seq):
    n_tok, d = x2.shape
    ts = min(512, seq)
    per_seq = seq // ts
    yspec = pl.BlockSpec((ts, WIDTH), lambda i: (i, 0))
    return pl.pallas_call(
        functools.partial(_merge_kernel, d=d),
        out_shape=jax.ShapeDtypeStruct((n_tok, d), F32),
        grid=(n_tok // ts,),
        in_specs=[yspec, yspec, yspec, yspec,
                  pl.BlockSpec((ts, N_BRANCH * d), lambda i: (i, CB_GATES)),
                  pl.BlockSpec((None, N_BRANCH, WIDTH, d), lambda i: (layer, 0, 0, 0)),
                  pl.BlockSpec((None, d, d), lambda i: (layer, 0, 0)),
                  pl.BlockSpec((ts, d), lambda i: (i, 0)),
                  pl.BlockSpec((None, 1, N_MOD * d), lambda i: (i // per_seq, 0, 0))],
        out_specs=pl.BlockSpec((ts, d), lambda i: (i, 0)),
        compiler_params=_params(1),
        name="merge_out",
    )(*ys, proj2, w_br, w_out, x2, mod_l)


def _mlp_kernel(x_ref, mod_ref, g_ref, w1_ref, w2_ref, fg_ref, o_ref, h_ref, acc_ref, *, d, final_norm):
    f = pl.program_id(1)

    @pl.when(f == 0)
    def _():
        x = x_ref[...]
        y = x * lax.rsqrt(jnp.mean(x * x, axis=-1, keepdims=True) + EPS)
        h = (y * g_ref[...]) * (1.0 + mod_ref[:, 4 * d:5 * d]) + mod_ref[:, 3 * d:4 * d]
        h_ref[...] = h.astype(BF16)
        acc_ref[...] = jnp.zeros_like(acc_ref)

    a = jnp.maximum(_dot(h_ref[...], w1_ref[...]), 0.0)
    acc_ref[...] += _dot((a * a).astype(BF16), w2_ref[...])

    @pl.when(f == pl.num_programs(1) - 1)
    def _():
        out = x_ref[...] + mod_ref[:, 5 * d:6 * d] * acc_ref[...]
        if final_norm:
            out = out * lax.rsqrt(jnp.mean(out * out, axis=-1, keepdims=True) + EPS) * fg_ref[...]
        o_ref[...] = out


def _mlp(x2, mod_l, norm_g, w1, w2, final_g, layer, seq, final_norm):
    n_tok, d = x2.shape
    d_ff = w1.shape[-1]
    ts = min(1024, seq)
    tf = min(1024, d_ff)
    per_seq = seq // ts
    return pl.pallas_call(
        functools.partial(_mlp_kernel, d=d, final_norm=final_norm),
        out_shape=jax.ShapeDtypeStruct((n_tok, d), F32),
        grid=(n_tok // ts, d_ff // tf),
        in_specs=[pl.BlockSpec((ts, d), lambda i, f: (i, 0)),
                  pl.BlockSpec((None, 1, N_MOD * d), lambda i, f: (i // per_seq, 0, 0)),
                  pl.BlockSpec((1, d), lambda i, f: (0, 0)),
                  pl.BlockSpec((None, d, tf), lambda i, f: (layer, 0, f)),
                  pl.BlockSpec((None, tf, d), lambda i, f: (layer, f, 0)),
                  pl.BlockSpec((1, d), lambda i, f: (0, 0))],
        out_specs=pl.BlockSpec((ts, d), lambda i, f: (i, 0)),
        scratch_shapes=[pltpu.VMEM((ts, d), BF16), pltpu.VMEM((ts, d), F32)],
        compiler_params=_params(2),
        name="mlp",
    )(x2, mod_l, norm_g.reshape(1, d), w1, w2, final_g.reshape(1, d))


def _permute_w_in(w_in):
    lead = w_in.shape[:-1]
    w = WIDTH
    o_small = 4 * w
    o_rest = o_small + 2 * N_HEADS
    o_gates = o_rest + 7 * w
    wb = w_in.astype(BF16)
    main = jnp.concatenate([wb[..., o_gates:], wb[..., :o_small], wb[..., o_rest:o_gates]], axis=-1)
    small = jnp.concatenate([wb[..., o_small:o_rest], jnp.zeros(lead + (HEAD_DIM - 2 * N_HEADS,), BF16)], axis=-1)
    return main, small


def kernel(x, c, positions, norm1_g, norm2_g, ada_w, ada_b, w_in, dn_conv_w, dn_a_log, dn_dt_bias, dn_norm_g, pool_w, pool_scale, sg_ln_g, sg_ln_b, sg_w, sg_b, ret_gn_g, w_br_dn, w_br_pool, w_br_sg, w_br_ret, w_out, mlp_w1, mlp_w2, final_g):
    bsz, seq, d = x.shape
    n_layers = w_in.shape[0]
    n_tok = bsz * seq
    mod = _modulation(c, ada_w, ada_b)
    cos2, sin2 = _rope_tables(positions)
    x2 = x.reshape(n_tok, d)
    w_main, w_small = _permute_w_in(w_in)
    w_br = jnp.stack([w_br_dn, w_br_pool, w_br_sg, w_br_ret], axis=1).astype(BF16)
    w_out_b, w1_b, w2_b = w_out.astype(BF16), mlp_w1.astype(BF16), mlp_w2.astype(BF16)
    for l in range(n_layers):
        mod_l = mod[l].reshape(bsz, 1, N_MOD * d)
        proj2, small2 = _input_projection(x2, mod_l, norm1_g[l], w_main, w_small, l, seq)
        proj3 = proj2.reshape(bsz, seq, N_CB * WIDTH)
        small3 = small2.reshape(bsz, seq, HEAD_DIM)
        y_a = _deltanet(proj3, small3, dn_conv_w[l], dn_a_log[l], dn_dt_bias[l], dn_norm_g[l])
        y_b = _pool(proj3, pool_w[l], pool_scale[l])
        y_c = _sgu(proj2, sg_ln_g[l], sg_ln_b[l], sg_w[l], sg_b[l])
        y_d = _retention(proj3, cos2, sin2, ret_gn_g[l])
        ys = [y.reshape(n_tok, WIDTH) for y in (y_a, y_b, y_c, y_d)]
        x2 = _merge(ys, proj2, w_br, w_out_b, x2, mod_l, l, seq)
        x2 = _mlp(x2, mod_l, norm2_g[l], w1_b, w2_b, final_g, l, seq, final_norm=(l == n_layers - 1))
    return x2.reshape(bsz, seq, d)
```

```python
import functools
import math

import jax
import jax.numpy as jnp
from jax import lax
from jax.experimental import pallas as pl
from jax.experimental.pallas import tpu as pltpu

F32 = jnp.float32
BF16 = jnp.bfloat16

EPS = 1e-6
N_HEADS = 4
HEAD_DIM = 128
WIDTH = N_HEADS * HEAD_DIM
N_BRANCH = 4
N_MOD = 6
DN_CONV = 4
DN_CHUNK = 64
CHUNK = 128
POOL_WINDOWS = (2, 4, 8, 16)
POOL_HALO = 16
ROPE_BASE = 10000.0
VMEM_LIMIT = 48 * 1024 * 1024

CB_DQ, CB_DK, CB_DV, CB_DZ, CB_PL, CB_SU, CB_SV, CB_RQ, CB_RK, CB_RV, CB_RG = range(11)
N_CB = 11


def _sigmoid(x):
    return 1.0 / (1.0 + jnp.exp(-x))


def _silu(x):
    return x * _sigmoid(x)


def _softplus(x):
    return jnp.maximum(x, 0.0) + jnp.log1p(jnp.exp(-jnp.abs(x)))


def _gelu(x):
    return 0.5 * x * (1.0 + lax.erf(x * (1.0 / math.sqrt(2.0))))


def _dot(a, b):
    return jnp.dot(a, b, preferred_element_type=F32)


def _dot_nt(a, b):
    return lax.dot_general(a, b, (((1,), (1,)), ((), ())), preferred_element_type=F32)


def _dot_tn(a, b):
    return lax.dot_general(a, b, (((0,), (0,)), ((), ())), preferred_element_type=F32)


def _params(n_axes):
    return pltpu.CompilerParams(dimension_semantics=("arbitrary",) * n_axes,
                                vmem_limit_bytes=VMEM_LIMIT)


def _mod_kernel(c_ref, w_ref, b_ref, o_ref):
    cond = _silu(c_ref[...])
    o_ref[...] = _dot(cond.astype(BF16), w_ref[...].astype(BF16)) + b_ref[...]


def _modulation(c, ada_w, ada_b):
    n_layers, d, n_out = ada_w.shape
    bsz = c.shape[0]
    rows = 8
    c_pad = jnp.zeros((rows, d), F32).at[:bsz].set(c)
    tn = n_out // 4
    out = pl.pallas_call(
        _mod_kernel,
        out_shape=jax.ShapeDtypeStruct((n_layers, rows, n_out), F32),
        grid=(n_layers, n_out // tn),
        in_specs=[pl.BlockSpec((rows, d), lambda l, j: (0, 0)),
                  pl.BlockSpec((None, d, tn), lambda l, j: (l, 0, j)),
                  pl.BlockSpec((None, 1, tn), lambda l, j: (l, 0, j))],
        out_specs=pl.BlockSpec((None, rows, tn), lambda l, j: (l, 0, j)),
        compiler_params=_params(2),
        name="adaln_mod",
    )(c_pad, ada_w, ada_b.reshape(n_layers, 1, n_out))
    return out[:, :bsz]


def _rope_kernel(pos_ref, inv_ref, sgn_ref, cos_ref, sin_ref):
    ang = pos_ref[...].astype(F32) * inv_ref[...]
    cos_ref[...] = jnp.cos(ang)
    sin_ref[...] = jnp.sin(ang) * sgn_ref[...]


def _rope_tables(positions):
    bsz, seq = positions.shape
    half = HEAD_DIM // 2
    inv = ROPE_BASE ** (-jnp.arange(0, HEAD_DIM, 2, dtype=F32) / HEAD_DIM)
    inv2 = jnp.concatenate([inv, inv]).reshape(1, HEAD_DIM)
    sgn = jnp.concatenate([-jnp.ones((half,), F32), jnp.ones((half,), F32)]).reshape(1, HEAD_DIM)
    n_tok = bsz * seq
    ts = min(1024, n_tok)
    shp = jax.ShapeDtypeStruct((n_tok, HEAD_DIM), F32)
    cos2, sin2 = pl.pallas_call(
        _rope_kernel,
        out_shape=(shp, shp),
        grid=(n_tok // ts,),
        in_specs=[pl.BlockSpec((ts, 1), lambda i: (i, 0)),
                  pl.BlockSpec((1, HEAD_DIM), lambda i: (0, 0)),
                  pl.BlockSpec((1, HEAD_DIM), lambda i: (0, 0))],
        out_specs=(pl.BlockSpec((ts, HEAD_DIM), lambda i: (i, 0)),
                   pl.BlockSpec((ts, HEAD_DIM), lambda i: (i, 0))),
        compiler_params=_params(1),
        name="rope_tables",
    )(positions.reshape(n_tok, 1), inv2, sgn)
    return cos2.reshape(bsz, seq, HEAD_DIM), sin2.reshape(bsz, seq, HEAD_DIM)


def _norm_modulate(x, gain, shift, scale):
    y = x * lax.rsqrt(jnp.mean(x * x, axis=-1, keepdims=True) + EPS)
    return (y * gain) * (1.0 + scale) + shift


def _inproj_kernel(x_ref, mod_ref, g_ref, w_ref, ws_ref, o_ref, os_ref, *, d):
    hb = _norm_modulate(x_ref[...], g_ref[...], mod_ref[:, 0:d], mod_ref[:, d:2 * d]).astype(BF16)
    os_ref[...] = _dot(hb, ws_ref[...])
    for j in range(N_CB):
        cols = slice(j * WIDTH, (j + 1) * WIDTH)
        o_ref[:, cols] = _dot(hb, w_ref[:, cols]).astype(BF16)


def _input_projection(x2, mod_l, norm_g, w_main, w_small, layer, seq):
    n_tok, d = x2.shape
    ts = min(512, seq)
    n_out = N_CB * WIDTH
    per_seq = seq // ts
    proj, small = pl.pallas_call(
        functools.partial(_inproj_kernel, d=d),
        out_shape=(jax.ShapeDtypeStruct((n_tok, n_out), BF16),
                   jax.ShapeDtypeStruct((n_tok, HEAD_DIM), F32)),
        grid=(n_tok // ts,),
        in_specs=[pl.BlockSpec((ts, d), lambda i: (i, 0)),
                  pl.BlockSpec((None, 1, N_MOD * d), lambda i: (i // per_seq, 0, 0)),
                  pl.BlockSpec((1, d), lambda i: (0, 0)),
                  pl.BlockSpec((None, d, n_out), lambda i: (layer, 0, 0)),
                  pl.BlockSpec((None, d, HEAD_DIM), lambda i: (layer, 0, 0))],
        out_specs=(pl.BlockSpec((ts, n_out), lambda i: (i, 0)),
                   pl.BlockSpec((ts, HEAD_DIM), lambda i: (i, 0))),
        compiler_params=_params(1),
        name="in_proj",
    )(x2, mod_l, norm_g.reshape(1, d), w_main, w_small)
    return proj, small


def _deltanet_kernel(q_ref, k_ref, v_ref, z_ref, sm_ref, cw_ref, ab_ref, ng_ref, o_ref,
                     cbuf, s_ref, *, bsz):
    ts = CHUNK
    cw = 3 * WIDTH
    inst = [(b, h) for b in range(bsz) for h in range(N_HEADS)]

    @pl.when(pl.program_id(0) == 0)
    def _():
        cbuf[:, 0:8, :] = jnp.zeros((bsz, 8, cw), F32)
        s_ref[...] = jnp.zeros_like(s_ref)

    row = lax.broadcasted_iota(jnp.int32, (ts, ts), 0)
    col = lax.broadcasted_iota(jnp.int32, (ts, ts), 1)

    def same_block(size):
        return (row // size) == (col // size)

    mask_incl = jnp.logical_and(same_block(DN_CHUNK), row >= col)
    mask_strict = jnp.logical_and(same_block(DN_CHUNK), row > col)
    eye = jnp.where(row == col, 1.0, 0.0).astype(F32)
    row_in_chunk = row % DN_CHUNK
    first_chunk = row < DN_CHUNK

    qkv, g_all, beta_all = [], [], []
    for b in range(bsz):
        cbuf[b, 8:8 + ts, 0:WIDTH] = q_ref[b].astype(F32)
        cbuf[b, 8:8 + ts, WIDTH:2 * WIDTH] = k_ref[b].astype(F32)
        cbuf[b, 8:8 + ts, 2 * WIDTH:3 * WIDTH] = v_ref[b].astype(F32)
        acc = cbuf[b, 8:8 + ts, :] * cw_ref[DN_CONV - 1:DN_CONV, :]
        for kk in range(DN_CONV - 1):
            off = 8 - (DN_CONV - 1) + kk
            acc = acc + cbuf[b, off:off + ts, :] * cw_ref[kk:kk + 1, :]
        cbuf[b, 0:8, :] = cbuf[b, ts:ts + 8, :]
        qkv.append(_silu(acc))

        sm = sm_ref[b]
        beta_all.append(_sigmoid(sm))
        g = -jnp.exp(ab_ref[0:1, :]) * _softplus(sm + ab_ref[1:2, :])
        for s in (1, 2, 4, 8, 16, 32):
            g = g + jnp.where(row_in_chunk >= s, pltpu.roll(g, s, axis=0), 0.0)
        g_all.append(g)

    gb = [jnp.broadcast_to(g_all[b][:, N_HEADS + h:N_HEADS + h + 1], (ts, ts)) for b, h in inst]
    bb = [jnp.broadcast_to(beta_all[b][:, h:h + 1], (ts, ts)) for b, h in inst]
    dec = [jnp.exp(jnp.where(mask_incl, g - g.T, -jnp.inf)) for g in gb]
    e_g = [jnp.exp(g) for g in gb]
    gl = [(g[DN_CHUNK - 1:DN_CHUNK, :], g[2 * DN_CHUNK - 1:2 * DN_CHUNK, :]) for g in gb]
    e_gl = [jnp.exp(jnp.where(first_chunk, l0, l1) - g) for g, (l0, l1) in zip(gb, gl)]

    def head(b, h, part):
        lo = part * WIDTH + h * HEAD_DIM
        return qkv[b][:, lo:lo + HEAD_DIM]

    def l2n(t):
        return t * lax.rsqrt(jnp.sum(t * t, axis=-1, keepdims=True) + EPS)

    qn = [l2n(head(b, h, 0)) * HEAD_DIM ** -0.5 for b, h in inst]
    kn = [l2n(head(b, h, 1)) for b, h in inst]
    kb = [k * bt for k, bt in zip(kn, bb)]
    knb = [k.astype(BF16) for k in kn]
    lm = [jnp.where(mask_strict, _dot_nt(a.astype(BF16), k) * d, 0.0) for a, k, d in zip(kb, knb, dec)]
    qk = [(_dot_nt(q.astype(BF16), k) * d).astype(BF16) for q, k, d in zip(qn, knb, dec)]

    l8 = [jnp.where(same_block(8), m, 0.0) for m in lm]
    l8b = [m.astype(BF16) for m in l8]
    p1b = [_dot(m, m).astype(BF16) for m in l8b]
    tm = [eye - m for m in l8]
    tm = [t + _dot(t.astype(BF16), p) for t, p in zip(tm, p1b)]
    p2b = [_dot(p, p).astype(BF16) for p in p1b]
    tm = [t + _dot(t.astype(BF16), p) for t, p in zip(tm, p2b)]
    for size in (8, 16, 32):
        lower_left = jnp.logical_and(same_block(2 * size), jnp.logical_not(same_block(size)))
        cm = [jnp.where(lower_left, m, 0.0).astype(BF16) for m in lm]
        tb = [t.astype(BF16) for t in tm]
        xm = [_dot(c, t).astype(BF16) for c, t in zip(cm, tb)]
        tm = [t - _dot(t16, x) for t, t16, x in zip(tm, tb, xm)]

    rhs = [jnp.concatenate([head(b, h, 2) * bt, k * e], axis=1).astype(BF16)
           for (b, h), bt, k, e in zip(inst, bb, kb, e_g)]
    uw = [_dot(t.astype(BF16), r) for t, r in zip(tm, rhs)]
    u = [m[:, 0:HEAD_DIM] for m in uw]
    w = [m[:, HEAD_DIM:2 * HEAD_DIM].astype(BF16) for m in uw]
    qs = [(q * e).astype(BF16) for q, e in zip(qn, e_g)]
    ks = [(k * e).astype(BF16) for k, e in zip(kn, e_gl)]

    state = [s_ref[n] for n in range(len(inst))]
    vns, outs = [], []
    for c in range(2):
        rows = slice(c * DN_CHUNK, (c + 1) * DN_CHUNK)
        sb = [s.astype(BF16) for s in state]
        vn = [a[rows] - _dot(wc[rows], s) for a, wc, s in zip(u, w, sb)]
        outs.append([_dot(q[rows], s) for q, s in zip(qs, sb)])
        state = [s * jnp.exp(g[c]) + _dot_tn(k[rows], v.astype(BF16))
                 for s, g, k, v in zip(state, gl, ks, vn)]
        vns.append(vn)
    for n, s in enumerate(state):
        s_ref[n] = s

    for n, (b, h) in enumerate(inst):
        lanes = slice(h * HEAD_DIM, (h + 1) * HEAD_DIM)
        vn = jnp.concatenate([vns[0][n], vns[1][n]], axis=0).astype(BF16)
        o = jnp.concatenate([outs[0][n], outs[1][n]], axis=0) + _dot(qk[n], vn)
        y = o * lax.rsqrt(jnp.mean(o * o, axis=-1, keepdims=True) + EPS) * ng_ref[...]
        y = y * _silu(z_ref[b, :, lanes].astype(F32))
        o_ref[b, :, lanes] = y.astype(BF16)


def _deltanet(proj3, small3, conv_w, a_log, dt_bias, norm_g):
    bsz, seq, _ = proj3.shape
    ts = CHUNK
    ab = jnp.zeros((2, HEAD_DIM), F32)
    ab = ab.at[0, N_HEADS:2 * N_HEADS].set(a_log).at[1, N_HEADS:2 * N_HEADS].set(dt_bias)

    def col(cb):
        return pl.BlockSpec((bsz, ts, WIDTH), lambda s: (0, s, cb))

    return pl.pallas_call(
        functools.partial(_deltanet_kernel, bsz=bsz),
        out_shape=jax.ShapeDtypeStruct((bsz, seq, WIDTH), BF16),
        grid=(seq // ts,),
        in_specs=[col(CB_DQ), col(CB_DK), col(CB_DV), col(CB_DZ),
                  pl.BlockSpec((bsz, ts, HEAD_DIM), lambda s: (0, s, 0)),
                  pl.BlockSpec((DN_CONV, 3 * WIDTH), lambda s: (0, 0)),
                  pl.BlockSpec((2, HEAD_DIM), lambda s: (0, 0)),
                  pl.BlockSpec((1, HEAD_DIM), lambda s: (0, 0))],
        out_specs=pl.BlockSpec((bsz, ts, WIDTH), lambda s: (0, s, 0)),
        scratch_shapes=[pltpu.VMEM((bsz, ts + 8, 3 * WIDTH), F32),
                        pltpu.VMEM((bsz * N_HEADS, HEAD_DIM, HEAD_DIM), F32)],
        compiler_params=_params(1),
        name="deltanet",
    )(proj3, proj3, proj3, proj3, small3, conv_w, ab, norm_g.reshape(1, HEAD_DIM))


def _pool_kernel(p_ref, w_ref, sc_ref, o_ref, pbuf):
    ts = p_ref.shape[0]
    t = pl.program_id(1)

    @pl.when(t == 0)
    def _():
        pbuf[0:POOL_HALO, :] = jnp.zeros((POOL_HALO, WIDTH), F32)

    p = p_ref[...].astype(F32)
    pbuf[POOL_HALO:POOL_HALO + ts, :] = p
    count = (t * ts + 1 + lax.broadcasted_iota(jnp.int32, (ts, HEAD_DIM), 0)).astype(F32)
    for g, win in enumerate(POOL_WINDOWS):
        lanes = slice(g * HEAD_DIM, (g + 1) * HEAD_DIM)
        acc = p[:, lanes]
        for dlt in range(1, win):
            acc = acc + pbuf[POOL_HALO - dlt:POOL_HALO - dlt + ts, lanes]
        pooled = acc / jnp.minimum(count, float(win)) - p[:, lanes]
        y = _dot(pooled.astype(BF16), w_ref[g]) * sc_ref[:, lanes]
        o_ref[:, lanes] = y.astype(BF16)
    pbuf[0:POOL_HALO, :] = pbuf[ts:ts + POOL_HALO, :]


def _pool(proj3, pool_w, pool_scale):
    bsz, seq, _ = proj3.shape
    ts = min(512, seq)
    return pl.pallas_call(
        _pool_kernel,
        out_shape=jax.ShapeDtypeStruct((bsz, seq, WIDTH), BF16),
        grid=(bsz, seq // ts),
        in_specs=[pl.BlockSpec((None, ts, WIDTH), lambda b, t: (b, t, CB_PL)),
                  pl.BlockSpec((N_HEADS, HEAD_DIM, HEAD_DIM), lambda b, t: (0, 0, 0)),
                  pl.BlockSpec((1, WIDTH), lambda b, t: (0, 0))],
        out_specs=pl.BlockSpec((None, ts, WIDTH), lambda b, t: (b, t, 0)),
        scratch_shapes=[pltpu.VMEM((ts + POOL_HALO, WIDTH), F32)],
        compiler_params=_params(2),
        name="pool",
    )(proj3, pool_w.astype(BF16), pool_scale.reshape(1, WIDTH))


def _sgu_kernel(u_ref, v_ref, lng_ref, lnb_ref, w_ref, bias_ref, o_ref):
    ts = u_ref.shape[0]
    u = _gelu(u_ref[...].astype(F32))
    v = _gelu(v_ref[...].astype(F32))
    mu = jnp.mean(v, axis=-1, keepdims=True)
    vc = v - mu
    var = jnp.mean(vc * vc, axis=-1, keepdims=True)
    vb = (vc * lax.rsqrt(var + EPS) * lng_ref[...] + lnb_ref[...]).astype(BF16)
    row = lax.broadcasted_iota(jnp.int32, (CHUNK, CHUNK), 0)
    col = lax.broadcasted_iota(jnp.int32, (CHUNK, CHUNK), 1)
    for g in range(N_HEADS):
        lanes = slice(g * HEAD_DIM, (g + 1) * HEAD_DIM)
        wg = jnp.where(row >= col, w_ref[g], 0.0).astype(BF16)
        for c in range(ts // CHUNK):
            rows = slice(c * CHUNK, (c + 1) * CHUNK)
            mixed = _dot(wg, vb[rows, lanes]) + bias_ref[:, lanes]
            o_ref[rows, lanes] = (u[rows, lanes] * mixed).astype(BF16)


def _sgu(proj2, ln_g, ln_b, sg_w, sg_b):
    n_tok = proj2.shape[0]
    ts = min(512, n_tok)
    bias = jnp.repeat(sg_b.T, HEAD_DIM, axis=1)
    return pl.pallas_call(
        _sgu_kernel,
        out_shape=jax.ShapeDtypeStruct((n_tok, WIDTH), BF16),
        grid=(n_tok // ts,),
        in_specs=[pl.BlockSpec((ts, WIDTH), lambda i: (i, CB_SU)),
                  pl.BlockSpec((ts, WIDTH), lambda i: (i, CB_SV)),
                  pl.BlockSpec((1, WIDTH), lambda i: (0, 0)),
                  pl.BlockSpec((1, WIDTH), lambda i: (0, 0)),
                  pl.BlockSpec((N_HEADS, CHUNK, CHUNK), lambda i: (0, 0, 0)),
                  pl.BlockSpec((CHUNK, WIDTH), lambda i: (0, 0))],
        out_specs=pl.BlockSpec((ts, WIDTH), lambda i: (i, 0)),
        compiler_params=_params(1),
        name="spatial_gating",
    )(proj2, proj2, ln_g.reshape(1, WIDTH), ln_b.reshape(1, WIDTH), sg_w, bias)


def _retention_kernel(q_ref, k_ref, v_ref, g_ref, cos_ref, sin_ref, gn_ref, o_ref, s_ref, c_ref, *, bsz):
    ts = CHUNK
    nh = N_HEADS
    inst = [(b, h) for b in range(bsz) for h in range(nh)]
    log_gamma = [math.log1p(-2.0 ** (-5.0 - h)) for h in range(nh)]

    @pl.when(pl.program_id(0) == 0)
    def _():
        s_ref[...] = jnp.zeros_like(s_ref)
        row = lax.broadcasted_iota(jnp.int32, (ts, ts), 0)
        col = lax.broadcasted_iota(jnp.int32, (ts, ts), 1)
        rel = (row - col).astype(F32)
        rowf = row.astype(F32)
        for h in range(nh):
            c_ref[h] = jnp.where(row >= col, jnp.exp(log_gamma[h] * jnp.maximum(rel, 0.0)), 0.0)
            c_ref[nh + h] = jnp.exp(log_gamma[h] * (float(ts - 1) - rowf))
            c_ref[2 * nh + h] = jnp.exp(log_gamma[h] * (rowf + 1.0))

    def lanes(h):
        return slice(h * HEAD_DIM, (h + 1) * HEAD_DIM)

    def rotary(ref, b, h):
        t = ref[b, :, lanes(h)].astype(F32)
        return t * cos_ref[b] + pltpu.roll(t, HEAD_DIM // 2, axis=1) * sin_ref[b]

    q = [rotary(q_ref, b, h).astype(BF16) for b, h in inst]
    k = [rotary(k_ref, b, h) * HEAD_DIM ** -0.5 for b, h in inst]
    v = [v_ref[b, :, lanes(h)] for b, h in inst]
    scores = [(_dot_nt(qn, kn.astype(BF16)) * c_ref[h]).astype(BF16) for qn, kn, (b, h) in zip(q, k, inst)]
    prev = [s_ref[n] for n in range(len(inst))]
    o = [_dot(sc, vn) + _dot(qn, p.astype(BF16)) * c_ref[2 * nh + h]
         for sc, vn, qn, p, (b, h) in zip(scores, v, q, prev, inst)]
    for n, (b, h) in enumerate(inst):
        kv = _dot_tn((k[n] * c_ref[nh + h]).astype(BF16), v[n])
        s_ref[n] = prev[n] * math.exp(log_gamma[h] * ts) + kv

    for n, (b, h) in enumerate(inst):
        mu = jnp.mean(o[n], axis=-1, keepdims=True)
        oc = o[n] - mu
        var = jnp.mean(oc * oc, axis=-1, keepdims=True)
        y = oc * lax.rsqrt(var + EPS) * gn_ref[:, lanes(h)]
        o_ref[b, :, lanes(h)] = (_silu(g_ref[b, :, lanes(h)].astype(F32)) * y).astype(BF16)


def _retention(proj3, cos2, sin2, gn_g):
    bsz, seq, _ = proj3.shape
    ts = CHUNK

    def col(cb):
        return pl.BlockSpec((bsz, ts, WIDTH), lambda s: (0, s, cb))

    tab = pl.BlockSpec((bsz, ts, HEAD_DIM), lambda s: (0, s, 0))
    return pl.pallas_call(
        functools.partial(_retention_kernel, bsz=bsz),
        out_shape=jax.ShapeDtypeStruct((bsz, seq, WIDTH), BF16),
        grid=(seq // ts,),
        in_specs=[col(CB_RQ), col(CB_RK), col(CB_RV), col(CB_RG), tab, tab,
                  pl.BlockSpec((1, WIDTH), lambda s: (0, 0))],
        out_specs=pl.BlockSpec((bsz, ts, WIDTH), lambda s: (0, s, 0)),
        scratch_shapes=[pltpu.VMEM((bsz * N_HEADS, HEAD_DIM, HEAD_DIM), F32),
                        pltpu.VMEM((3 * N_HEADS, CHUNK, CHUNK), F32)],
        compiler_params=_params(1),
        name="retention",
    )(proj3, proj3, proj3, proj3, cos2, sin2, gn_g.reshape(1, WIDTH))


def _merge_kernel(ya_ref, yb_ref, yc_ref, yd_ref, x_ref, mod_ref, g_ref, wg_ref, wbr_ref, wo_ref, o_ref, *, d):
    x = x_ref[...]
    hb = _norm_modulate(x, g_ref[...], mod_ref[:, 0:d], mod_ref[:, d:2 * d]).astype(BF16)
    merged = None
    for i, y_ref in enumerate((ya_ref, yb_ref, yc_ref, yd_ref)):
        gate = _sigmoid(_dot(hb, wg_ref[:, i * d:(i + 1) * d]))
        term = gate * _dot(y_ref[...], wbr_ref[i])
        merged = term if merged is None else merged + term
    r = _dot(merged.astype(BF16), wo_ref[...])
    o_ref[...] = x + mod_ref[:, 2 * d:3 * d] * r


def _merge(ys, x2, mod_l, norm_g, w_gates, w_br, w_out, layer, seq):
    n_tok, d = x2.shape
    ts = min(512, seq)
    per_seq = seq // ts
    yspec = pl.BlockSpec((ts, WIDTH), lambda i: (i, 0))
    return pl.pallas_call(
        functools.partial(_merge_kernel, d=d),
        out_shape=jax.ShapeDtypeStruct((n_tok, d), F32),
        grid=(n_tok // ts,),
        in_specs=[yspec, yspec, yspec, yspec,
                  pl.BlockSpec((ts, d), lambda i: (i, 0)),
                  pl.BlockSpec((None, 1, N_MOD * d), lambda i: (i // per_seq, 0, 0)),
                  pl.BlockSpec((1, d), lambda i: (0, 0)),
                  pl.BlockSpec((None, d, N_BRANCH * d), lambda i: (layer, 0, 0)),
                  pl.BlockSpec((None, N_BRANCH, WIDTH, d), lambda i: (layer, 0, 0, 0)),
                  pl.BlockSpec((None, d, d), lambda i: (layer, 0, 0))],
        out_specs=pl.BlockSpec((ts, d), lambda i: (i, 0)),
        compiler_params=_params(1),
        name="merge_out",
    )(*ys, x2, mod_l, norm_g.reshape(1, d), w_gates, w_br, w_out)


def _mlp_kernel(x_ref, mod_ref, g_ref, w1_ref, w2_ref, fg_ref, o_ref, *, d, tf, final_norm):
    x = x_ref[...]
    hb = _norm_modulate(x, g_ref[...], mod_ref[:, 3 * d:4 * d], mod_ref[:, 4 * d:5 * d]).astype(BF16)
    acc = None
    for f in range(w1_ref.shape[1] // tf):
        a = jnp.maximum(_dot(hb, w1_ref[:, f * tf:(f + 1) * tf]), 0.0)
        part = _dot((a * a).astype(BF16), w2_ref[f * tf:(f + 1) * tf, :])
        acc = part if acc is None else acc + part
    out = x + mod_ref[:, 5 * d:6 * d] * acc
    if final_norm:
        out = out * lax.rsqrt(jnp.mean(out * out, axis=-1, keepdims=True) + EPS) * fg_ref[...]
    o_ref[...] = out


def _mlp(x2, mod_l, norm_g, w1, w2, final_g, layer, seq, final_norm):
    n_tok, d = x2.shape
    d_ff = w1.shape[-1]
    ts = min(512, seq)
    per_seq = seq // ts
    return pl.pallas_call(
        functools.partial(_mlp_kernel, d=d, tf=min(1024, d_ff), final_norm=final_norm),
        out_shape=jax.ShapeDtypeStruct((n_tok, d), F32),
        grid=(n_tok // ts,),
        in_specs=[pl.BlockSpec((ts, d), lambda i: (i, 0)),
                  pl.BlockSpec((None, 1, N_MOD * d), lambda i: (i // per_seq, 0, 0)),
                  pl.BlockSpec((1, d), lambda i: (0, 0)),
                  pl.BlockSpec((None, d, d_ff), lambda i: (layer, 0, 0)),
                  pl.BlockSpec((None, d_ff, d), lambda i: (layer, 0, 0)),
                  pl.BlockSpec((1, d), lambda i: (0, 0))],
        out_specs=pl.BlockSpec((ts, d), lambda i: (i, 0)),
        compiler_params=_params(1),
        name="mlp",
    )(x2, mod_l, norm_g.reshape(1, d), w1, w2, final_g.reshape(1, d))


def _split_w_in(w_in):
    lead = w_in.shape[:-1]
    o_small = 4 * WIDTH
    o_rest = o_small + 2 * N_HEADS
    o_gates = o_rest + 7 * WIDTH
    main = jnp.concatenate([w_in[..., :o_small].astype(BF16), w_in[..., o_rest:o_gates].astype(BF16)], axis=-1)
    small = jnp.concatenate([w_in[..., o_small:o_rest].astype(BF16),
                             jnp.zeros(lead + (HEAD_DIM - 2 * N_HEADS,), BF16)], axis=-1)
    return main, small, w_in[..., o_gates:].astype(BF16)


def kernel(x, c, positions, norm1_g, norm2_g, ada_w, ada_b, w_in, dn_conv_w, dn_a_log, dn_dt_bias, dn_norm_g, pool_w, pool_scale, sg_ln_g, sg_ln_b, sg_w, sg_b, ret_gn_g, w_br_dn, w_br_pool, w_br_sg, w_br_ret, w_out, mlp_w1, mlp_w2, final_g):
    bsz, seq, d = x.shape
    n_layers = w_in.shape[0]
    n_tok = bsz * seq
    mod = _modulation(c, ada_w, ada_b)
    cos2, sin2 = _rope_tables(positions)
    x2 = x.reshape(n_tok, d)
    w_main, w_small, w_gates = _split_w_in(w_in)
    w_br = jnp.stack([w_br_dn, w_br_pool, w_br_sg, w_br_ret], axis=1).astype(BF16)
    w_out_b, w1_b, w2_b = w_out.astype(BF16), mlp_w1.astype(BF16), mlp_w2.astype(BF16)
    for l in range(n_layers):
        mod_l = mod[l].reshape(bsz, 1, N_MOD * d)
        proj2, small2 = _input_projection(x2, mod_l, norm1_g[l], w_main, w_small, l, seq)
        proj3 = proj2.reshape(bsz, seq, N_CB * WIDTH)
        small3 = small2.reshape(bsz, seq, HEAD_DIM)
        y_a = _deltanet(proj3, small3, dn_conv_w[l], dn_a_log[l], dn_dt_bias[l], dn_norm_g[l])
        y_b = _pool(proj3, pool_w[l], pool_scale[l])
        y_c = _sgu(proj2, sg_ln_g[l], sg_ln_b[l], sg_w[l], sg_b[l])
        y_d = _retention(proj3, cos2, sin2, ret_gn_g[l])
        ys = [y.reshape(n_tok, WIDTH) for y in (y_a, y_b, y_c, y_d)]
        x2 = _merge(ys, x2, mod_l, norm1_g[l], w_gates, w_br, w_out_b, l, seq)
        x2 = _mlp(x2, mod_l, norm2_g[l], w1_b, w2_b, final_g, l, seq, final_norm=(l == n_layers - 1))
    return x2.reshape(bsz, seq, d)
```

```python
import functools
import math

import jax
import jax.numpy as jnp
from jax import lax
from jax.experimental import pallas as pl
from jax.experimental.pallas import tpu as pltpu

F32 = jnp.float32
BF16 = jnp.bfloat16

EPS = 1e-6
N_HEADS = 4
HEAD_DIM = 128
WIDTH = N_HEADS * HEAD_DIM
N_BRANCH = 4
N_MOD = 6
DN_CONV = 4
DN_CHUNK = 64
CHUNK = 128
POOL_WINDOWS = (2, 4, 8, 16)
POOL_HALO = 16
ROPE_BASE = 10000.0
VMEM_LIMIT = 48 * 1024 * 1024

N_DN_CB = 4
CB_PL, CB_SU, CB_SV = range(3)
N_REST_CB = 3
N_RET_CB = 4


def _sigmoid(x):
    return 1.0 / (1.0 + jnp.exp(-x))


def _silu(x):
    return x * _sigmoid(x)


def _softplus(x):
    return jnp.maximum(x, 0.0) + jnp.log1p(jnp.exp(-jnp.abs(x)))


def _gelu(x):
    return 0.5 * x * (1.0 + lax.erf(x * (1.0 / math.sqrt(2.0))))


def _dot(a, b):
    return jnp.dot(a, b, preferred_element_type=F32)


def _dot_nt(a, b):
    return lax.dot_general(a, b, (((1,), (1,)), ((), ())), preferred_element_type=F32)


def _dot_tn(a, b):
    return lax.dot_general(a, b, (((0,), (0,)), ((), ())), preferred_element_type=F32)


def _params(n_axes):
    return pltpu.CompilerParams(dimension_semantics=("arbitrary",) * n_axes,
                                vmem_limit_bytes=VMEM_LIMIT)


def _mod_kernel(c_ref, w_ref, b_ref, o_ref):
    cond = _silu(c_ref[...])
    o_ref[...] = _dot(cond.astype(BF16), w_ref[...].astype(BF16)) + b_ref[...]


def _modulation(c, ada_w, ada_b):
    n_layers, d, n_out = ada_w.shape
    bsz = c.shape[0]
    rows = 8
    c_pad = jnp.zeros((rows, d), F32).at[:bsz].set(c)
    tn = n_out // 4
    out = pl.pallas_call(
        _mod_kernel,
        out_shape=jax.ShapeDtypeStruct((n_layers, rows, n_out), F32),
        grid=(n_layers, n_out // tn),
        in_specs=[pl.BlockSpec((rows, d), lambda l, j: (0, 0)),
                  pl.BlockSpec((None, d, tn), lambda l, j: (l, 0, j)),
                  pl.BlockSpec((None, 1, tn), lambda l, j: (l, 0, j))],
        out_specs=pl.BlockSpec((None, rows, tn), lambda l, j: (l, 0, j)),
        compiler_params=_params(2),
        name="adaln_mod",
    )(c_pad, ada_w, ada_b.reshape(n_layers, 1, n_out))
    return out[:, :bsz]


def _rope_kernel(pos_ref, inv_ref, sgn_ref, cos_ref, sin_ref):
    ang = pos_ref[...].astype(F32) * inv_ref[...]
    cos_ref[...] = jnp.cos(ang)
    sin_ref[...] = jnp.sin(ang) * sgn_ref[...]


def _rope_tables(positions):
    bsz, seq = positions.shape
    half = HEAD_DIM // 2
    inv = ROPE_BASE ** (-jnp.arange(0, HEAD_DIM, 2, dtype=F32) / HEAD_DIM)
    inv2 = jnp.concatenate([inv, inv]).reshape(1, HEAD_DIM)
    sgn = jnp.concatenate([-jnp.ones((half,), F32), jnp.ones((half,), F32)]).reshape(1, HEAD_DIM)
    n_tok = bsz * seq
    ts = min(1024, n_tok)
    shp = jax.ShapeDtypeStruct((n_tok, HEAD_DIM), F32)
    cos2, sin2 = pl.pallas_call(
        _rope_kernel,
        out_shape=(shp, shp),
        grid=(n_tok // ts,),
        in_specs=[pl.BlockSpec((ts, 1), lambda i: (i, 0)),
                  pl.BlockSpec((1, HEAD_DIM), lambda i: (0, 0)),
                  pl.BlockSpec((1, HEAD_DIM), lambda i: (0, 0))],
        out_specs=(pl.BlockSpec((ts, HEAD_DIM), lambda i: (i, 0)),
                   pl.BlockSpec((ts, HEAD_DIM), lambda i: (i, 0))),
        compiler_params=_params(1),
        name="rope_tables",
    )(positions.reshape(n_tok, 1), inv2, sgn)
    return cos2.reshape(bsz, seq, HEAD_DIM), sin2.reshape(bsz, seq, HEAD_DIM)


def _norm_modulate(x, gain, shift, scale):
    y = x * lax.rsqrt(jnp.mean(x * x, axis=-1, keepdims=True) + EPS)
    return (y * gain) * (1.0 + scale) + shift


def _proj_mixers_kernel(x_ref, mod_ref, g_ref, w_ref, ws_ref, cw_ref, ab_ref, ng_ref, cos_ref, sin_ref, gn_ref,
                        p_ref, o_ref, od_ref, cbuf, s_ref, rs_ref, rc_ref, *, bsz, d):
    ts = CHUNK
    cw = 3 * WIDTH
    nh = N_HEADS
    inst = [(b, h) for b in range(bsz) for h in range(nh)]
    log_gamma = [math.log1p(-2.0 ** (-5.0 - h)) for h in range(nh)]

    @pl.when(pl.program_id(0) == 0)
    def _():
        cbuf[:, 0:8, :] = jnp.zeros((bsz, 8, cw), F32)
        s_ref[...] = jnp.zeros_like(s_ref)
        rs_ref[...] = jnp.zeros_like(rs_ref)
        row0 = lax.broadcasted_iota(jnp.int32, (ts, ts), 0)
        col0 = lax.broadcasted_iota(jnp.int32, (ts, ts), 1)
        rel = (row0 - col0).astype(F32)
        rowf = row0.astype(F32)
        for h in range(nh):
            rc_ref[h] = jnp.where(row0 >= col0, jnp.exp(log_gamma[h] * jnp.maximum(rel, 0.0)), 0.0)
            rc_ref[nh + h] = jnp.exp(log_gamma[h] * (float(ts - 1) - rowf))
            rc_ref[2 * nh + h] = jnp.exp(log_gamma[h] * (rowf + 1.0))

    def lanes(h):
        return slice(h * HEAD_DIM, (h + 1) * HEAD_DIM)

    hb = jnp.concatenate(
        [_norm_modulate(x_ref[b], g_ref[...], mod_ref[b, :, 0:d], mod_ref[b, :, d:2 * d]).astype(BF16)
         for b in range(bsz)], axis=0)

    rest_issued = [0]

    def issue_rest(count):
        for _ in range(count):
            j = rest_issued[0]
            if j == N_REST_CB:
                return
            r = _dot(hb, w_ref[:, (N_DN_CB + j) * WIDTH:(N_DN_CB + j + 1) * WIDTH]).astype(BF16)
            for b in range(bsz):
                p_ref[b, :, j * WIDTH:(j + 1) * WIDTH] = r[b * ts:(b + 1) * ts]
            rest_issued[0] = j + 1

    ret = {}

    def ret_project():
        lo = (N_DN_CB + N_REST_CB) * WIDTH
        ret["in"] = [_dot(hb, w_ref[:, lo + j * WIDTH:lo + (j + 1) * WIDTH]) for j in range(N_RET_CB)]

    def ret_part(j, b, h):
        return ret["in"][j][b * ts:(b + 1) * ts, lanes(h)]

    def ret_scores():
        def rotary(t, b):
            return t * cos_ref[b] + pltpu.roll(t, HEAD_DIM // 2, axis=1) * sin_ref[b]

        ret["q"] = [rotary(ret_part(0, b, h), b).astype(BF16) for b, h in inst]
        ret["k"] = [rotary(ret_part(1, b, h), b) * HEAD_DIM ** -0.5 for b, h in inst]
        ret["v"] = [ret_part(2, b, h).astype(BF16) for b, h in inst]
        ret["scores"] = [(_dot_nt(qr, kr.astype(BF16)) * rc_ref[h]).astype(BF16)
                         for qr, kr, (b, h) in zip(ret["q"], ret["k"], inst)]

    def ret_state():
        prev = [rs_ref[n] for n in range(len(inst))]
        ret["o"] = [_dot(sc, vr) + _dot(qr, p.astype(BF16)) * rc_ref[2 * nh + h]
                    for sc, vr, qr, p, (b, h) in zip(ret["scores"], ret["v"], ret["q"], prev, inst)]
        for n, (b, h) in enumerate(inst):
            kv = _dot_tn((ret["k"][n] * rc_ref[nh + h]).astype(BF16), ret["v"][n])
            rs_ref[n] = prev[n] * math.exp(log_gamma[h] * ts) + kv

    def ret_store():
        for n, (b, h) in enumerate(inst):
            mu = jnp.mean(ret["o"][n], axis=-1, keepdims=True)
            oc = ret["o"][n] - mu
            var = jnp.mean(oc * oc, axis=-1, keepdims=True)
            y = oc * lax.rsqrt(var + EPS) * gn_ref[:, lanes(h)]
            od_ref[b, :, lanes(h)] = (_silu(ret_part(3, b, h)) * y).astype(BF16)

    ret_stages = [ret_scores, ret_state, ret_store]

    sm_all = _dot(hb, ws_ref[...])
    qkvz = [_dot(hb, w_ref[:, j * WIDTH:(j + 1) * WIDTH]) for j in range(N_DN_CB)]

    row = lax.broadcasted_iota(jnp.int32, (ts, ts), 0)
    col = lax.broadcasted_iota(jnp.int32, (ts, ts), 1)

    def same_block(size):
        return (row // size) == (col // size)

    mask_incl = jnp.logical_and(same_block(DN_CHUNK), row >= col)
    mask_strict = jnp.logical_and(same_block(DN_CHUNK), row > col)
    eye = jnp.where(row == col, 1.0, 0.0).astype(F32)
    row_in_chunk = row % DN_CHUNK
    first_chunk = row < DN_CHUNK

    qkv, g_all, beta_all = [], [], []
    for b in range(bsz):
        rows = slice(b * ts, (b + 1) * ts)
        for part in range(3):
            cbuf[b, 8:8 + ts, part * WIDTH:(part + 1) * WIDTH] = qkvz[part][rows]
        acc = cbuf[b, 8:8 + ts, :] * cw_ref[DN_CONV - 1:DN_CONV, :]
        for kk in range(DN_CONV - 1):
            off = 8 - (DN_CONV - 1) + kk
            acc = acc + cbuf[b, off:off + ts, :] * cw_ref[kk:kk + 1, :]
        cbuf[b, 0:8, :] = cbuf[b, ts:ts + 8, :]
        qkv.append(_silu(acc))

        sm = sm_all[rows]
        beta_all.append(_sigmoid(sm))
        g = -jnp.exp(ab_ref[0:1, :]) * _softplus(sm + ab_ref[1:2, :])
        for s in (1, 2, 4, 8, 16, 32):
            g = g + jnp.where(row_in_chunk >= s, pltpu.roll(g, s, axis=0), 0.0)
        g_all.append(g)
        issue_rest(1)

    gb = [jnp.broadcast_to(g_all[b][:, N_HEADS + h:N_HEADS + h + 1], (ts, ts)) for b, h in inst]
    bb = [jnp.broadcast_to(beta_all[b][:, h:h + 1], (ts, ts)) for b, h in inst]
    dec = [jnp.exp(jnp.where(mask_incl, g - g.T, -jnp.inf)) for g in gb]
    issue_rest(1)
    e_g = [jnp.exp(g) for g in gb]
    gl = [(g[DN_CHUNK - 1:DN_CHUNK, :], g[2 * DN_CHUNK - 1:2 * DN_CHUNK, :]) for g in gb]
    e_gl = [jnp.exp(jnp.where(first_chunk, l0, l1) - g) for g, (l0, l1) in zip(gb, gl)]

    def head(b, h, part):
        lo = part * WIDTH + h * HEAD_DIM
        return qkv[b][:, lo:lo + HEAD_DIM]

    def l2n(t):
        return t * lax.rsqrt(jnp.sum(t * t, axis=-1, keepdims=True) + EPS)

    qn = [l2n(head(b, h, 0)) * HEAD_DIM ** -0.5 for b, h in inst]
    kn = [l2n(head(b, h, 1)) for b, h in inst]
    kb = [k * bt for k, bt in zip(kn, bb)]
    knb = [k.astype(BF16) for k in kn]
    lm = [jnp.where(mask_strict, _dot_nt(a.astype(BF16), k) * dc, 0.0) for a, k, dc in zip(kb, knb, dec)]
    qk = [(_dot_nt(q.astype(BF16), k) * dc).astype(BF16) for q, k, dc in zip(qn, knb, dec)]

    l8 = [jnp.where(same_block(8), m, 0.0) for m in lm]
    l8b = [m.astype(BF16) for m in l8]
    p1b = [_dot(m, m).astype(BF16) for m in l8b]
    tm = [eye - m for m in l8]
    tm = [t + _dot(t.astype(BF16), p) for t, p in zip(tm, p1b)]
    p2b = [_dot(p, p).astype(BF16) for p in p1b]
    tm = [t + _dot(t.astype(BF16), p) for t, p in zip(tm, p2b)]
    ret_project()
    for size, ret_stage in zip((8, 16, 32), ret_stages):
        lower_left = jnp.logical_and(same_block(2 * size), jnp.logical_not(same_block(size)))
        cm = [jnp.where(lower_left, m, 0.0).astype(BF16) for m in lm]
        tb = [t.astype(BF16) for t in tm]
        xm = [_dot(c, t).astype(BF16) for c, t in zip(cm, tb)]
        tm = [t - _dot(t16, x) for t, t16, x in zip(tm, tb, xm)]
        ret_stage()

    rhs = [jnp.concatenate([head(b, h, 2) * bt, k * e], axis=1).astype(BF16)
           for (b, h), bt, k, e in zip(inst, bb, kb, e_g)]
    uw = [_dot(t.astype(BF16), r) for t, r in zip(tm, rhs)]
    u = [m[:, 0:HEAD_DIM] for m in uw]
    w = [m[:, HEAD_DIM:2 * HEAD_DIM].astype(BF16) for m in uw]
    qs = [(q * e).astype(BF16) for q, e in zip(qn, e_g)]
    ks = [(k * e).astype(BF16) for k, e in zip(kn, e_gl)]

    state = [s_ref[n] for n in range(len(inst))]
    vns, outs = [], []
    for c in range(2):
        rows = slice(c * DN_CHUNK, (c + 1) * DN_CHUNK)
        sb = [s.astype(BF16) for s in state]
        vn = [a[rows] - _dot(wc[rows], s) for a, wc, s in zip(u, w, sb)]
        outs.append([_dot(q[rows], s) for q, s in zip(qs, sb)])
        state = [s * jnp.exp(g[c]) + _dot_tn(k[rows], v.astype(BF16))
                 for s, g, k, v in zip(state, gl, ks, vn)]
        vns.append(vn)
    for n, s in enumerate(state):
        s_ref[n] = s

    for n, (b, h) in enumerate(inst):
        vn = jnp.concatenate([vns[0][n], vns[1][n]], axis=0).astype(BF16)
        o = jnp.concatenate([outs[0][n], outs[1][n]], axis=0) + _dot(qk[n], vn)
        y = o * lax.rsqrt(jnp.mean(o * o, axis=-1, keepdims=True) + EPS) * ng_ref[...]
        y = y * _silu(qkvz[3][b * ts:(b + 1) * ts, lanes(h)])
        o_ref[b, :, lanes(h)] = y.astype(BF16)


def _proj_mixers(x3, mod_l, norm_g, w_main, w_small, layer, conv_w, a_log, dt_bias, dn_norm_g, cos2, sin2, gn_g):
    bsz, seq, d = x3.shape
    ts = CHUNK
    n_in = (N_DN_CB + N_REST_CB + N_RET_CB) * WIDTH
    tab = pl.BlockSpec((bsz, ts, HEAD_DIM), lambda s: (0, s, 0))
    branch = jax.ShapeDtypeStruct((bsz, seq, WIDTH), BF16)
    state = pltpu.VMEM((bsz * N_HEADS, HEAD_DIM, HEAD_DIM), F32)
    ab = jnp.zeros((2, HEAD_DIM), F32)
    ab = ab.at[0, N_HEADS:2 * N_HEADS].set(a_log).at[1, N_HEADS:2 * N_HEADS].set(dt_bias)
    return pl.pallas_call(
        functools.partial(_proj_mixers_kernel, bsz=bsz, d=d),
        out_shape=(jax.ShapeDtypeStruct((bsz, seq, N_REST_CB * WIDTH), BF16), branch, branch),
        grid=(seq // ts,),
        in_specs=[pl.BlockSpec((bsz, ts, d), lambda s: (0, s, 0)),
                  pl.BlockSpec((bsz, 1, N_MOD * d), lambda s: (0, 0, 0)),
                  pl.BlockSpec((1, d), lambda s: (0, 0)),
                  pl.BlockSpec((None, d, n_in), lambda s: (layer, 0, 0)),
                  pl.BlockSpec((None, d, HEAD_DIM), lambda s: (layer, 0, 0)),
                  pl.BlockSpec((DN_CONV, 3 * WIDTH), lambda s: (0, 0)),
                  pl.BlockSpec((2, HEAD_DIM), lambda s: (0, 0)),
                  pl.BlockSpec((1, HEAD_DIM), lambda s: (0, 0)),
                  tab, tab,
                  pl.BlockSpec((1, WIDTH), lambda s: (0, 0))],
        out_specs=(pl.BlockSpec((bsz, ts, N_REST_CB * WIDTH), lambda s: (0, s, 0)),
                   pl.BlockSpec((bsz, ts, WIDTH), lambda s: (0, s, 0)),
                   pl.BlockSpec((bsz, ts, WIDTH), lambda s: (0, s, 0))),
        scratch_shapes=[pltpu.VMEM((bsz, ts + 8, 3 * WIDTH), F32), state, state,
                        pltpu.VMEM((3 * N_HEADS, CHUNK, CHUNK), F32)],
        compiler_params=_params(1),
        name="proj_mixers",
    )(x3, mod_l, norm_g.reshape(1, d), w_main, w_small, conv_w, ab, dn_norm_g.reshape(1, HEAD_DIM),
      cos2, sin2, gn_g.reshape(1, WIDTH))


def _pool_kernel(p_ref, w_ref, sc_ref, o_ref, pbuf):
    ts = p_ref.shape[0]
    t = pl.program_id(1)

    @pl.when(t == 0)
    def _():
        pbuf[0:POOL_HALO, :] = jnp.zeros((POOL_HALO, WIDTH), F32)

    p = p_ref[...].astype(F32)
    pbuf[POOL_HALO:POOL_HALO + ts, :] = p
    count = (t * ts + 1 + lax.broadcasted_iota(jnp.int32, (ts, HEAD_DIM), 0)).astype(F32)
    for g, win in enumerate(POOL_WINDOWS):
        lanes = slice(g * HEAD_DIM, (g + 1) * HEAD_DIM)
        acc = p[:, lanes]
        for dlt in range(1, win):
            acc = acc + pbuf[POOL_HALO - dlt:POOL_HALO - dlt + ts, lanes]
        pooled = acc / jnp.minimum(count, float(win)) - p[:, lanes]
        y = _dot(pooled.astype(BF16), w_ref[g]) * sc_ref[:, lanes]
        o_ref[:, lanes] = y.astype(BF16)
    pbuf[0:POOL_HALO, :] = pbuf[ts:ts + POOL_HALO, :]


def _pool(proj3, pool_w, pool_scale):
    bsz, seq, _ = proj3.shape
    ts = min(512, seq)
    return pl.pallas_call(
        _pool_kernel,
        out_shape=jax.ShapeDtypeStruct((bsz, seq, WIDTH), BF16),
        grid=(bsz, seq // ts),
        in_specs=[pl.BlockSpec((None, ts, WIDTH), lambda b, t: (b, t, CB_PL)),
                  pl.BlockSpec((N_HEADS, HEAD_DIM, HEAD_DIM), lambda b, t: (0, 0, 0)),
                  pl.BlockSpec((1, WIDTH), lambda b, t: (0, 0))],
        out_specs=pl.BlockSpec((None, ts, WIDTH), lambda b, t: (b, t, 0)),
        scratch_shapes=[pltpu.VMEM((ts + POOL_HALO, WIDTH), F32)],
        compiler_params=_params(2),
        name="pool",
    )(proj3, pool_w.astype(BF16), pool_scale.reshape(1, WIDTH))


def _sgu_kernel(u_ref, v_ref, lng_ref, lnb_ref, w_ref, bias_ref, o_ref):
    ts = u_ref.shape[0]
    u = _gelu(u_ref[...].astype(F32))
    v = _gelu(v_ref[...].astype(F32))
    mu = jnp.mean(v, axis=-1, keepdims=True)
    vc = v - mu
    var = jnp.mean(vc * vc, axis=-1, keepdims=True)
    vb = (vc * lax.rsqrt(var + EPS) * lng_ref[...] + lnb_ref[...]).astype(BF16)
    row = lax.broadcasted_iota(jnp.int32, (CHUNK, CHUNK), 0)
    col = lax.broadcasted_iota(jnp.int32, (CHUNK, CHUNK), 1)
    for g in range(N_HEADS):
        lanes = slice(g * HEAD_DIM, (g + 1) * HEAD_DIM)
        wg = jnp.where(row >= col, w_ref[g], 0.0).astype(BF16)
        for c in range(ts // CHUNK):
            rows = slice(c * CHUNK, (c + 1) * CHUNK)
            mixed = _dot(wg, vb[rows, lanes]) + bias_ref[:, lanes]
            o_ref[rows, lanes] = (u[rows, lanes] * mixed).astype(BF16)


def _sgu(proj2, ln_g, ln_b, sg_w, sg_b):
    n_tok = proj2.shape[0]
    ts = min(512, n_tok)
    bias = jnp.repeat(sg_b.T, HEAD_DIM, axis=1)
    return pl.pallas_call(
        _sgu_kernel,
        out_shape=jax.ShapeDtypeStruct((n_tok, WIDTH), BF16),
        grid=(n_tok // ts,),
        in_specs=[pl.BlockSpec((ts, WIDTH), lambda i: (i, CB_SU)),
                  pl.BlockSpec((ts, WIDTH), lambda i: (i, CB_SV)),
                  pl.BlockSpec((1, WIDTH), lambda i: (0, 0)),
                  pl.BlockSpec((1, WIDTH), lambda i: (0, 0)),
                  pl.BlockSpec((N_HEADS, CHUNK, CHUNK), lambda i: (0, 0, 0)),
                  pl.BlockSpec((CHUNK, WIDTH), lambda i: (0, 0))],
        out_specs=pl.BlockSpec((ts, WIDTH), lambda i: (i, 0)),
        compiler_params=_params(1),
        name="spatial_gating",
    )(proj2, proj2, ln_g.reshape(1, WIDTH), ln_b.reshape(1, WIDTH), sg_w, bias)


def _merge_kernel(ya_ref, yb_ref, yc_ref, yd_ref, x_ref, mod_ref, g_ref, wg_ref, wbr_ref, wo_ref, o_ref, *, d):
    x = x_ref[...]
    hb = _norm_modulate(x, g_ref[...], mod_ref[:, 0:d], mod_ref[:, d:2 * d]).astype(BF16)
    merged = None
    for i, y_ref in enumerate((ya_ref, yb_ref, yc_ref, yd_ref)):
        gate = _sigmoid(_dot(hb, wg_ref[:, i * d:(i + 1) * d]))
        term = gate * _dot(y_ref[...], wbr_ref[i])
        merged = term if merged is None else merged + term
    r = _dot(merged.astype(BF16), wo_ref[...])
    o_ref[...] = x + mod_ref[:, 2 * d:3 * d] * r


def _merge(ys, x2, mod_l, norm_g, w_gates, w_br, w_out, layer, seq):
    n_tok, d = x2.shape
    ts = min(512, seq)
    per_seq = seq // ts
    yspec = pl.BlockSpec((ts, WIDTH), lambda i: (i, 0))
    return pl.pallas_call(
        functools.partial(_merge_kernel, d=d),
        out_shape=jax.ShapeDtypeStruct((n_tok, d), F32),
        grid=(n_tok // ts,),
        in_specs=[yspec, yspec, yspec, yspec,
                  pl.BlockSpec((ts, d), lambda i: (i, 0)),
                  pl.BlockSpec((None, 1, N_MOD * d), lambda i: (i // per_seq, 0, 0)),
                  pl.BlockSpec((1, d), lambda i: (0, 0)),
                  pl.BlockSpec((None, d, N_BRANCH * d), lambda i: (layer, 0, 0)),
                  pl.BlockSpec((None, N_BRANCH, WIDTH, d), lambda i: (layer, 0, 0, 0)),
                  pl.BlockSpec((None, d, d), lambda i: (layer, 0, 0))],
        out_specs=pl.BlockSpec((ts, d), lambda i: (i, 0)),
        compiler_params=_params(1),
        name="merge_out",
    )(*ys, x2, mod_l, norm_g.reshape(1, d), w_gates, w_br, w_out)


def _mlp_kernel(x_ref, mod_ref, g_ref, w1_ref, w2_ref, fg_ref, o_ref, *, d, tf, final_norm):
    x = x_ref[...]
    hb = _norm_modulate(x, g_ref[...], mod_ref[:, 3 * d:4 * d], mod_ref[:, 4 * d:5 * d]).astype(BF16)
    acc = None
    for f in range(w1_ref.shape[1] // tf):
        a = jnp.maximum(_dot(hb, w1_ref[:, f * tf:(f + 1) * tf]), 0.0)
        part = _dot((a * a).astype(BF16), w2_ref[f * tf:(f + 1) * tf, :])
        acc = part if acc is None else acc + part
    out = x + mod_ref[:, 5 * d:6 * d] * acc
    if final_norm:
        out = out * lax.rsqrt(jnp.mean(out * out, axis=-1, keepdims=True) + EPS) * fg_ref[...]
    o_ref[...] = out


def _mlp(x2, mod_l, norm_g, w1, w2, final_g, layer, seq, final_norm):
    n_tok, d = x2.shape
    d_ff = w1.shape[-1]
    ts = min(512, seq)
    per_seq = seq // ts
    return pl.pallas_call(
        functools.partial(_mlp_kernel, d=d, tf=min(1024, d_ff), final_norm=final_norm),
        out_shape=jax.ShapeDtypeStruct((n_tok, d), F32),
        grid=(n_tok // ts,),
        in_specs=[pl.BlockSpec((ts, d), lambda i: (i, 0)),
                  pl.BlockSpec((None, 1, N_MOD * d), lambda i: (i // per_seq, 0, 0)),
                  pl.BlockSpec((1, d), lambda i: (0, 0)),
                  pl.BlockSpec((None, d, d_ff), lambda i: (layer, 0, 0)),
                  pl.BlockSpec((None, d_ff, d), lambda i: (layer, 0, 0)),
                  pl.BlockSpec((1, d), lambda i: (0, 0))],
        out_specs=pl.BlockSpec((ts, d), lambda i: (i, 0)),
        compiler_params=_params(1),
        name="mlp",
    )(x2, mod_l, norm_g.reshape(1, d), w1, w2, final_g.reshape(1, d))


def _split_w_in(w_in):
    lead = w_in.shape[:-1]
    o_small = 4 * WIDTH
    o_rest = o_small + 2 * N_HEADS
    o_gates = o_rest + 7 * WIDTH
    main = jnp.concatenate([w_in[..., :o_small].astype(BF16), w_in[..., o_rest:o_gates].astype(BF16)], axis=-1)
    small = jnp.concatenate([w_in[..., o_small:o_rest].astype(BF16),
                             jnp.zeros(lead + (HEAD_DIM - 2 * N_HEADS,), BF16)], axis=-1)
    return main, small, w_in[..., o_gates:].astype(BF16)


def kernel(x, c, positions, norm1_g, norm2_g, ada_w, ada_b, w_in, dn_conv_w, dn_a_log, dn_dt_bias, dn_norm_g, pool_w, pool_scale, sg_ln_g, sg_ln_b, sg_w, sg_b, ret_gn_g, w_br_dn, w_br_pool, w_br_sg, w_br_ret, w_out, mlp_w1, mlp_w2, final_g):
    bsz, seq, d = x.shape
    n_layers = w_in.shape[0]
    n_tok = bsz * seq
    mod = _modulation(c, ada_w, ada_b)
    cos2, sin2 = _rope_tables(positions)
    x2 = x.reshape(n_tok, d)
    w_main, w_small, w_gates = _split_w_in(w_in)
    w_br = jnp.stack([w_br_dn, w_br_pool, w_br_sg, w_br_ret], axis=1).astype(BF16)
    w_out_b, w1_b, w2_b = w_out.astype(BF16), mlp_w1.astype(BF16), mlp_w2.astype(BF16)
    for l in range(n_layers):
        mod_l = mod[l].reshape(bsz, 1, N_MOD * d)
        proj3, y_a, y_d = _proj_mixers(x2.reshape(bsz, seq, d), mod_l, norm1_g[l], w_main, w_small, l,
                                       dn_conv_w[l], dn_a_log[l], dn_dt_bias[l], dn_norm_g[l],
                                       cos2, sin2, ret_gn_g[l])
        proj2 = proj3.reshape(n_tok, N_REST_CB * WIDTH)
        y_b = _pool(proj3, pool_w[l], pool_scale[l])
        y_c = _sgu(proj2, sg_ln_g[l], sg_ln_b[l], sg_w[l], sg_b[l])
        ys = [y.reshape(n_tok, WIDTH) for y in (y_a, y_b, y_c, y_d)]
        x2 = _merge(ys, x2, mod_l, norm1_g[l], w_gates, w_br, w_out_b, l, seq)
        x2 = _mlp(x2, mod_l, norm2_g[l], w1_b, w2_b, final_g, l, seq, final_norm=(l == n_layers - 1))
    return x2.reshape(bsz, seq, d)
```

```python
import functools
import math

import jax
import jax.numpy as jnp
from jax import lax
from jax.experimental import pallas as pl
from jax.experimental.pallas import tpu as pltpu

F32 = jnp.float32
BF16 = jnp.bfloat16

EPS = 1e-6
N_HEADS = 4
HEAD_DIM = 128
WIDTH = N_HEADS * HEAD_DIM
N_BRANCH = 4
N_MOD = 6
DN_CONV = 4
DN_CHUNK = 64
CHUNK = 128
POOL_WINDOWS = (2, 4, 8, 16)
POOL_HALO = 16
ROPE_BASE = 10000.0
VMEM_LIMIT = 48 * 1024 * 1024

N_DN_CB = 4
CB_PL, CB_SU, CB_SV = range(3)
N_REST_CB = 3
N_RET_CB = 4


def _sigmoid(x):
    return 1.0 / (1.0 + jnp.exp(-x))


def _silu(x):
    return x * _sigmoid(x)


def _softplus(x):
    return jnp.maximum(x, 0.0) + jnp.log1p(jnp.exp(-jnp.abs(x)))


def _gelu(x):
    return 0.5 * x * (1.0 + lax.erf(x * (1.0 / math.sqrt(2.0))))


def _dot(a, b):
    return jnp.dot(a, b, preferred_element_type=F32)


def _dot_nt(a, b):
    return lax.dot_general(a, b, (((1,), (1,)), ((), ())), preferred_element_type=F32)


def _dot_tn(a, b):
    return lax.dot_general(a, b, (((0,), (0,)), ((), ())), preferred_element_type=F32)


def _params(n_axes):
    return pltpu.CompilerParams(dimension_semantics=("arbitrary",) * n_axes,
                                vmem_limit_bytes=VMEM_LIMIT)


def _mod_kernel(c_ref, w_ref, b_ref, o_ref):
    cond = _silu(c_ref[...])
    o_ref[...] = _dot(cond.astype(BF16), w_ref[...].astype(BF16)) + b_ref[...]


def _modulation(c, ada_w, ada_b):
    n_layers, d, n_out = ada_w.shape
    bsz = c.shape[0]
    rows = 8
    c_pad = jnp.zeros((rows, d), F32).at[:bsz].set(c)
    tn = n_out // 4
    out = pl.pallas_call(
        _mod_kernel,
        out_shape=jax.ShapeDtypeStruct((n_layers, rows, n_out), F32),
        grid=(n_layers, n_out // tn),
        in_specs=[pl.BlockSpec((rows, d), lambda l, j: (0, 0)),
                  pl.BlockSpec((None, d, tn), lambda l, j: (l, 0, j)),
                  pl.BlockSpec((None, 1, tn), lambda l, j: (l, 0, j))],
        out_specs=pl.BlockSpec((None, rows, tn), lambda l, j: (l, 0, j)),
        compiler_params=_params(2),
        name="adaln_mod",
    )(c_pad, ada_w, ada_b.reshape(n_layers, 1, n_out))
    return out[:, :bsz]


def _rope_kernel(pos_ref, inv_ref, sgn_ref, cos_ref, sin_ref):
    ang = pos_ref[...].astype(F32) * inv_ref[...]
    cos_ref[...] = jnp.cos(ang)
    sin_ref[...] = jnp.sin(ang) * sgn_ref[...]


def _rope_tables(positions):
    bsz, seq = positions.shape
    half = HEAD_DIM // 2
    inv = ROPE_BASE ** (-jnp.arange(0, HEAD_DIM, 2, dtype=F32) / HEAD_DIM)
    inv2 = jnp.concatenate([inv, inv]).reshape(1, HEAD_DIM)
    sgn = jnp.concatenate([-jnp.ones((half,), F32), jnp.ones((half,), F32)]).reshape(1, HEAD_DIM)
    n_tok = bsz * seq
    ts = min(1024, n_tok)
    shp = jax.ShapeDtypeStruct((n_tok, HEAD_DIM), F32)
    cos2, sin2 = pl.pallas_call(
        _rope_kernel,
        out_shape=(shp, shp),
        grid=(n_tok // ts,),
        in_specs=[pl.BlockSpec((ts, 1), lambda i: (i, 0)),
                  pl.BlockSpec((1, HEAD_DIM), lambda i: (0, 0)),
                  pl.BlockSpec((1, HEAD_DIM), lambda i: (0, 0))],
        out_specs=(pl.BlockSpec((ts, HEAD_DIM), lambda i: (i, 0)),
                   pl.BlockSpec((ts, HEAD_DIM), lambda i: (i, 0))),
        compiler_params=_params(1),
        name="rope_tables",
    )(positions.reshape(n_tok, 1), inv2, sgn)
    return cos2.reshape(bsz, seq, HEAD_DIM), sin2.reshape(bsz, seq, HEAD_DIM)


def _norm_modulate(x, gain, shift, scale):
    y = x * lax.rsqrt(jnp.mean(x * x, axis=-1, keepdims=True) + EPS)
    return (y * gain) * (1.0 + scale) + shift


def _proj_mixers_kernel(x_ref, mod_ref, g_ref, w_ref, ws_ref, cw_ref, ab_ref, ng_ref, cos_ref, sin_ref, gn_ref,
                        pw_ref, psc_ref, lng_ref, lnb_ref, sgw_ref, sgb_ref,
                        o_ref, ob_ref, oc_ref, od_ref, cbuf, s_ref, rs_ref, rc_ref, pbuf, *, bsz, d):
    ts = CHUNK
    cw = 3 * WIDTH
    nh = N_HEADS
    inst = [(b, h) for b in range(bsz) for h in range(nh)]
    log_gamma = [math.log1p(-2.0 ** (-5.0 - h)) for h in range(nh)]

    @pl.when(pl.program_id(0) == 0)
    def _():
        cbuf[:, 0:8, :] = jnp.zeros((bsz, 8, cw), F32)
        s_ref[...] = jnp.zeros_like(s_ref)
        rs_ref[...] = jnp.zeros_like(rs_ref)
        pbuf[:, 0:POOL_HALO, :] = jnp.zeros((bsz, POOL_HALO, WIDTH), F32)
        row0 = lax.broadcasted_iota(jnp.int32, (ts, ts), 0)
        col0 = lax.broadcasted_iota(jnp.int32, (ts, ts), 1)
        rel = (row0 - col0).astype(F32)
        rowf = row0.astype(F32)
        for h in range(nh):
            rc_ref[h] = jnp.where(row0 >= col0, jnp.exp(log_gamma[h] * jnp.maximum(rel, 0.0)), 0.0)
            rc_ref[nh + h] = jnp.exp(log_gamma[h] * (float(ts - 1) - rowf))
            rc_ref[2 * nh + h] = jnp.exp(log_gamma[h] * (rowf + 1.0))

    def lanes(h):
        return slice(h * HEAD_DIM, (h + 1) * HEAD_DIM)

    hb = jnp.concatenate(
        [_norm_modulate(x_ref[b], g_ref[...], mod_ref[b, :, 0:d], mod_ref[b, :, d:2 * d]).astype(BF16)
         for b in range(bsz)], axis=0)

    rest = {}

    def rest_project(j):
        if j not in rest:
            rest[j] = _dot(hb, w_ref[:, (N_DN_CB + j) * WIDTH:(N_DN_CB + j + 1) * WIDTH])

    def pool_stage():
        count = (pl.program_id(0) * ts + 1 + lax.broadcasted_iota(jnp.int32, (ts, HEAD_DIM), 0)).astype(F32)
        for b in range(bsz):
            p = rest[CB_PL][b * ts:(b + 1) * ts]
            pbuf[b, POOL_HALO:POOL_HALO + ts, :] = p
            for g, win in enumerate(POOL_WINDOWS):
                acc = p[:, lanes(g)]
                for dlt in range(1, win):
                    acc = acc + pbuf[b, POOL_HALO - dlt:POOL_HALO - dlt + ts, lanes(g)]
                pooled = acc / jnp.minimum(count, float(win)) - p[:, lanes(g)]
                y = _dot(pooled.astype(BF16), pw_ref[g]) * psc_ref[:, lanes(g)]
                ob_ref[b, :, lanes(g)] = y.astype(BF16)
            pbuf[b, 0:POOL_HALO, :] = pbuf[b, ts:ts + POOL_HALO, :]

    def sgu_stage():
        u = _gelu(rest[CB_SU])
        v = _gelu(rest[CB_SV])
        mu = jnp.mean(v, axis=-1, keepdims=True)
        vc = v - mu
        var = jnp.mean(vc * vc, axis=-1, keepdims=True)
        vb = (vc * lax.rsqrt(var + EPS) * lng_ref[...] + lnb_ref[...]).astype(BF16)
        keep = (lax.broadcasted_iota(jnp.int32, (ts, ts), 0) >= lax.broadcasted_iota(jnp.int32, (ts, ts), 1))
        for g in range(nh):
            wg = jnp.where(keep, sgw_ref[g], 0.0).astype(BF16)
            for b in range(bsz):
                rows = slice(b * ts, (b + 1) * ts)
                mixed = _dot(wg, vb[rows, lanes(g)]) + sgb_ref[:, lanes(g)]
                oc_ref[b, :, lanes(g)] = (u[rows, lanes(g)] * mixed).astype(BF16)

    ret = {}

    def ret_project():
        lo = (N_DN_CB + N_REST_CB) * WIDTH
        ret["in"] = [_dot(hb, w_ref[:, lo + j * WIDTH:lo + (j + 1) * WIDTH]) for j in range(N_RET_CB)]

    def ret_part(j, b, h):
        return ret["in"][j][b * ts:(b + 1) * ts, lanes(h)]

    def ret_scores():
        def rotary(t, b):
            return t * cos_ref[b] + pltpu.roll(t, HEAD_DIM // 2, axis=1) * sin_ref[b]

        ret["q"] = [rotary(ret_part(0, b, h), b).astype(BF16) for b, h in inst]
        ret["k"] = [rotary(ret_part(1, b, h), b) * HEAD_DIM ** -0.5 for b, h in inst]
        ret["v"] = [ret_part(2, b, h).astype(BF16) for b, h in inst]
        ret["scores"] = [(_dot_nt(qr, kr.astype(BF16)) * rc_ref[h]).astype(BF16)
                         for qr, kr, (b, h) in zip(ret["q"], ret["k"], inst)]

    def ret_state():
        prev = [rs_ref[n] for n in range(len(inst))]
        ret["o"] = [_dot(sc, vr) + _dot(qr, p.astype(BF16)) * rc_ref[2 * nh + h]
                    for sc, vr, qr, p, (b, h) in zip(ret["scores"], ret["v"], ret["q"], prev, inst)]
        for n, (b, h) in enumerate(inst):
            kv = _dot_tn((ret["k"][n] * rc_ref[nh + h]).astype(BF16), ret["v"][n])
            rs_ref[n] = prev[n] * math.exp(log_gamma[h] * ts) + kv

    def ret_store():
        for n, (b, h) in enumerate(inst):
            mu = jnp.mean(ret["o"][n], axis=-1, keepdims=True)
            oc = ret["o"][n] - mu
            var = jnp.mean(oc * oc, axis=-1, keepdims=True)
            y = oc * lax.rsqrt(var + EPS) * gn_ref[:, lanes(h)]
            od_ref[b, :, lanes(h)] = (_silu(ret_part(3, b, h)) * y).astype(BF16)

    ret_stages = [ret_scores, ret_state, ret_store]

    sm_all = _dot(hb, ws_ref[...])
    qkvz = [_dot(hb, w_ref[:, j * WIDTH:(j + 1) * WIDTH]) for j in range(N_DN_CB)]

    row = lax.broadcasted_iota(jnp.int32, (ts, ts), 0)
    col = lax.broadcasted_iota(jnp.int32, (ts, ts), 1)

    def same_block(size):
        return (row // size) == (col // size)

    mask_incl = jnp.logical_and(same_block(DN_CHUNK), row >= col)
    mask_strict = jnp.logical_and(same_block(DN_CHUNK), row > col)
    eye = jnp.where(row == col, 1.0, 0.0).astype(F32)
    row_in_chunk = row % DN_CHUNK
    first_chunk = row < DN_CHUNK

    qkv, g_all, beta_all = [], [], []
    for b in range(bsz):
        rows = slice(b * ts, (b + 1) * ts)
        for part in range(3):
            cbuf[b, 8:8 + ts, part * WIDTH:(part + 1) * WIDTH] = qkvz[part][rows]
        acc = cbuf[b, 8:8 + ts, :] * cw_ref[DN_CONV - 1:DN_CONV, :]
        for kk in range(DN_CONV - 1):
            off = 8 - (DN_CONV - 1) + kk
            acc = acc + cbuf[b, off:off + ts, :] * cw_ref[kk:kk + 1, :]
        cbuf[b, 0:8, :] = cbuf[b, ts:ts + 8, :]
        qkv.append(_silu(acc))

        sm = sm_all[rows]
        beta_all.append(_sigmoid(sm))
        g = -jnp.exp(ab_ref[0:1, :]) * _softplus(sm + ab_ref[1:2, :])
        for s in (1, 2, 4, 8, 16, 32):
            g = g + jnp.where(row_in_chunk >= s, pltpu.roll(g, s, axis=0), 0.0)
        g_all.append(g)
        rest_project(min(b, N_REST_CB - 1))

    gb = [jnp.broadcast_to(g_all[b][:, N_HEADS + h:N_HEADS + h + 1], (ts, ts)) for b, h in inst]
    bb = [jnp.broadcast_to(beta_all[b][:, h:h + 1], (ts, ts)) for b, h in inst]
    dec = [jnp.exp(jnp.where(mask_incl, g - g.T, -jnp.inf)) for g in gb]
    for j in range(N_REST_CB):
        rest_project(j)
    e_g = [jnp.exp(g) for g in gb]
    gl = [(g[DN_CHUNK - 1:DN_CHUNK, :], g[2 * DN_CHUNK - 1:2 * DN_CHUNK, :]) for g in gb]
    e_gl = [jnp.exp(jnp.where(first_chunk, l0, l1) - g) for g, (l0, l1) in zip(gb, gl)]

    def head(b, h, part):
        lo = part * WIDTH + h * HEAD_DIM
        return qkv[b][:, lo:lo + HEAD_DIM]

    def l2n(t):
        return t * lax.rsqrt(jnp.sum(t * t, axis=-1, keepdims=True) + EPS)

    qn = [l2n(head(b, h, 0)) * HEAD_DIM ** -0.5 for b, h in inst]
    kn = [l2n(head(b, h, 1)) for b, h in inst]
    kb = [k * bt for k, bt in zip(kn, bb)]
    knb = [k.astype(BF16) for k in kn]
    lm = [jnp.where(mask_strict, _dot_nt(a.astype(BF16), k) * dc, 0.0) for a, k, dc in zip(kb, knb, dec)]
    qk = [(_dot_nt(q.astype(BF16), k) * dc).astype(BF16) for q, k, dc in zip(qn, knb, dec)]

    l8 = [jnp.where(same_block(8), m, 0.0) for m in lm]
    l8b = [m.astype(BF16) for m in l8]
    p1b = [_dot(m, m).astype(BF16) for m in l8b]
    tm = [eye - m for m in l8]
    tm = [t + _dot(t.astype(BF16), p) for t, p in zip(tm, p1b)]
    p2b = [_dot(p, p).astype(BF16) for p in p1b]
    tm = [t + _dot(t.astype(BF16), p) for t, p in zip(tm, p2b)]
    ret_project()
    for size, ret_stage in zip((8, 16, 32), ret_stages):
        lower_left = jnp.logical_and(same_block(2 * size), jnp.logical_not(same_block(size)))
        cm = [jnp.where(lower_left, m, 0.0).astype(BF16) for m in lm]
        tb = [t.astype(BF16) for t in tm]
        xm = [_dot(c, t).astype(BF16) for c, t in zip(cm, tb)]
        tm = [t - _dot(t16, x) for t, t16, x in zip(tm, tb, xm)]
        ret_stage()

    rhs = [jnp.concatenate([head(b, h, 2) * bt, k * e], axis=1).astype(BF16)
           for (b, h), bt, k, e in zip(inst, bb, kb, e_g)]
    uw = [_dot(t.astype(BF16), r) for t, r in zip(tm, rhs)]
    pool_stage()
    u = [m[:, 0:HEAD_DIM] for m in uw]
    w = [m[:, HEAD_DIM:2 * HEAD_DIM].astype(BF16) for m in uw]
    qs = [(q * e).astype(BF16) for q, e in zip(qn, e_g)]
    ks = [(k * e).astype(BF16) for k, e in zip(kn, e_gl)]

    state = [s_ref[n] for n in range(len(inst))]
    vns, outs = [], []
    for c in range(2):
        rows = slice(c * DN_CHUNK, (c + 1) * DN_CHUNK)
        sb = [s.astype(BF16) for s in state]
        vn = [a[rows] - _dot(wc[rows], s) for a, wc, s in zip(u, w, sb)]
        outs.append([_dot(q[rows], s) for q, s in zip(qs, sb)])
        state = [s * jnp.exp(g[c]) + _dot_tn(k[rows], v.astype(BF16))
                 for s, g, k, v in zip(state, gl, ks, vn)]
        vns.append(vn)
    for n, s in enumerate(state):
        s_ref[n] = s
    sgu_stage()

    for n, (b, h) in enumerate(inst):
        vn = jnp.concatenate([vns[0][n], vns[1][n]], axis=0).astype(BF16)
        o = jnp.concatenate([outs[0][n], outs[1][n]], axis=0) + _dot(qk[n], vn)
        y = o * lax.rsqrt(jnp.mean(o * o, axis=-1, keepdims=True) + EPS) * ng_ref[...]
        y = y * _silu(qkvz[3][b * ts:(b + 1) * ts, lanes(h)])
        o_ref[b, :, lanes(h)] = y.astype(BF16)


def _proj_mixers(x3, mod_l, norm_g, w_main, w_small, layer, conv_w, a_log, dt_bias, dn_norm_g, cos2, sin2, gn_g,
                 pool_w, pool_scale, ln_g, ln_b, sg_w, sg_b):
    bsz, seq, d = x3.shape
    ts = CHUNK
    n_in = (N_DN_CB + N_REST_CB + N_RET_CB) * WIDTH
    tab = pl.BlockSpec((bsz, ts, HEAD_DIM), lambda s: (0, s, 0))
    branch = jax.ShapeDtypeStruct((bsz, seq, WIDTH), BF16)
    state = pltpu.VMEM((bsz * N_HEADS, HEAD_DIM, HEAD_DIM), F32)
    ab = jnp.zeros((2, HEAD_DIM), F32)
    ab = ab.at[0, N_HEADS:2 * N_HEADS].set(a_log).at[1, N_HEADS:2 * N_HEADS].set(dt_bias)
    sg_bias = jnp.repeat(sg_b.T, HEAD_DIM, axis=1)
    row_vec = pl.BlockSpec((1, WIDTH), lambda s: (0, 0))
    group_mat = pl.BlockSpec((N_HEADS, CHUNK, CHUNK), lambda s: (0, 0, 0))
    branch_spec = pl.BlockSpec((bsz, ts, WIDTH), lambda s: (0, s, 0))
    return pl.pallas_call(
        functools.partial(_proj_mixers_kernel, bsz=bsz, d=d),
        out_shape=(branch, branch, branch, branch),
        grid=(seq // ts,),
        in_specs=[pl.BlockSpec((bsz, ts, d), lambda s: (0, s, 0)),
                  pl.BlockSpec((bsz, 1, N_MOD * d), lambda s: (0, 0, 0)),
                  pl.BlockSpec((1, d), lambda s: (0, 0)),
                  pl.BlockSpec((None, d, n_in), lambda s: (layer, 0, 0)),
                  pl.BlockSpec((None, d, HEAD_DIM), lambda s: (layer, 0, 0)),
                  pl.BlockSpec((DN_CONV, 3 * WIDTH), lambda s: (0, 0)),
                  pl.BlockSpec((2, HEAD_DIM), lambda s: (0, 0)),
                  pl.BlockSpec((1, HEAD_DIM), lambda s: (0, 0)),
                  tab, tab, row_vec,
                  group_mat, row_vec, row_vec, row_vec, group_mat,
                  pl.BlockSpec((CHUNK, WIDTH), lambda s: (0, 0))],
        out_specs=(branch_spec, branch_spec, branch_spec, branch_spec),
        scratch_shapes=[pltpu.VMEM((bsz, ts + 8, 3 * WIDTH), F32), state, state,
                        pltpu.VMEM((3 * N_HEADS, CHUNK, CHUNK), F32),
                        pltpu.VMEM((bsz, ts + POOL_HALO, WIDTH), F32)],
        compiler_params=_params(1),
        name="proj_mixers",
    )(x3, mod_l, norm_g.reshape(1, d), w_main, w_small, conv_w, ab, dn_norm_g.reshape(1, HEAD_DIM),
      cos2, sin2, gn_g.reshape(1, WIDTH),
      pool_w.astype(BF16), pool_scale.reshape(1, WIDTH), ln_g.reshape(1, WIDTH), ln_b.reshape(1, WIDTH), sg_w, sg_bias)


def _merge_kernel(ya_ref, yb_ref, yc_ref, yd_ref, x_ref, mod_ref, g_ref, wg_ref, wa_ref, wb_ref, wc_ref, wd_ref,
                  wo_ref, o_ref, *, d):
    x = x_ref[...]
    hb = _norm_modulate(x, g_ref[...], mod_ref[:, 0:d], mod_ref[:, d:2 * d]).astype(BF16)
    merged = None
    branches = ((ya_ref, wa_ref), (yb_ref, wb_ref), (yc_ref, wc_ref), (yd_ref, wd_ref))
    for i, (y_ref, wbr_ref) in enumerate(branches):
        gate = _sigmoid(_dot(hb, wg_ref[:, i * d:(i + 1) * d]))
        term = gate * _dot(y_ref[...], wbr_ref[...])
        merged = term if merged is None else merged + term
    r = _dot(merged.astype(BF16), wo_ref[...])
    o_ref[...] = x + mod_ref[:, 2 * d:3 * d] * r


def _merge(ys, x2, mod_l, norm_g, w_gates, w_br, w_out, layer, seq):
    n_tok, d = x2.shape
    ts = min(512, seq)
    per_seq = seq // ts
    yspec = pl.BlockSpec((ts, WIDTH), lambda i: (i, 0))
    wspec = pl.BlockSpec((None, WIDTH, d), lambda i: (layer, 0, 0))
    return pl.pallas_call(
        functools.partial(_merge_kernel, d=d),
        out_shape=jax.ShapeDtypeStruct((n_tok, d), F32),
        grid=(n_tok // ts,),
        in_specs=[yspec, yspec, yspec, yspec,
                  pl.BlockSpec((ts, d), lambda i: (i, 0)),
                  pl.BlockSpec((None, 1, N_MOD * d), lambda i: (i // per_seq, 0, 0)),
                  pl.BlockSpec((1, d), lambda i: (0, 0)),
                  pl.BlockSpec((None, d, N_BRANCH * d), lambda i: (layer, 0, 0)),
                  wspec, wspec, wspec, wspec,
                  pl.BlockSpec((None, d, d), lambda i: (layer, 0, 0))],
        out_specs=pl.BlockSpec((ts, d), lambda i: (i, 0)),
        compiler_params=_params(1),
        name="merge_out",
    )(*ys, x2, mod_l, norm_g.reshape(1, d), w_gates, *w_br, w_out)


def _mlp_kernel(x_ref, mod_ref, g_ref, w1_ref, w2_ref, fg_ref, o_ref, *, d, tf, final_norm):
    x = x_ref[...]
    hb = _norm_modulate(x, g_ref[...], mod_ref[:, 3 * d:4 * d], mod_ref[:, 4 * d:5 * d]).astype(BF16)
    acc = None
    for f in range(w1_ref.shape[1] // tf):
        a = jnp.maximum(_dot(hb, w1_ref[:, f * tf:(f + 1) * tf]), 0.0)
        part = _dot((a * a).astype(BF16), w2_ref[f * tf:(f + 1) * tf, :])
        acc = part if acc is None else acc + part
    out = x + mod_ref[:, 5 * d:6 * d] * acc
    if final_norm:
        out = out * lax.rsqrt(jnp.mean(out * out, axis=-1, keepdims=True) + EPS) * fg_ref[...]
    o_ref[...] = out


def _mlp(x2, mod_l, norm_g, w1, w2, final_g, layer, seq, final_norm):
    n_tok, d = x2.shape
    d_ff = w1.shape[-1]
    ts = min(512, seq)
    per_seq = seq // ts
    return pl.pallas_call(
        functools.partial(_mlp_kernel, d=d, tf=min(1024, d_ff), final_norm=final_norm),
        out_shape=jax.ShapeDtypeStruct((n_tok, d), F32),
        grid=(n_tok // ts,),
        in_specs=[pl.BlockSpec((ts, d), lambda i: (i, 0)),
                  pl.BlockSpec((None, 1, N_MOD * d), lambda i: (i // per_seq, 0, 0)),
                  pl.BlockSpec((1, d), lambda i: (0, 0)),
                  pl.BlockSpec((None, d, d_ff), lambda i: (layer, 0, 0)),
                  pl.BlockSpec((None, d_ff, d), lambda i: (layer, 0, 0)),
                  pl.BlockSpec((1, d), lambda i: (0, 0))],
        out_specs=pl.BlockSpec((ts, d), lambda i: (i, 0)),
        compiler_params=_params(1),
        name="mlp",
    )(x2, mod_l, norm_g.reshape(1, d), w1, w2, final_g.reshape(1, d))


def _split_w_in_kernel(w_ref, main_ref, small_ref, gates_ref):
    o_small = N_DN_CB * WIDTH
    o_rest = o_small + 2 * N_HEADS
    o_gates = o_rest + (N_REST_CB + N_RET_CB) * WIDTH
    main_ref[:, 0:o_small] = w_ref[:, 0:o_small].astype(BF16)
    main_ref[:, o_small:] = w_ref[:, o_rest:o_gates].astype(BF16)
    lane = lax.broadcasted_iota(jnp.int32, small_ref.shape, 1)
    small_ref[...] = jnp.where(lane < 2 * N_HEADS, w_ref[:, o_small:o_small + HEAD_DIM], 0.0).astype(BF16)
    gates_ref[...] = w_ref[:, o_gates:].astype(BF16)


def _split_w_in(w_in):
    n_layers, d, n_cols = w_in.shape
    n_main = (N_DN_CB + N_REST_CB + N_RET_CB) * WIDTH
    n_gates = n_cols - n_main - 2 * N_HEADS
    tr = 128
    return pl.pallas_call(
        _split_w_in_kernel,
        out_shape=(jax.ShapeDtypeStruct((n_layers, d, n_main), BF16),
                   jax.ShapeDtypeStruct((n_layers, d, HEAD_DIM), BF16),
                   jax.ShapeDtypeStruct((n_layers, d, n_gates), BF16)),
        grid=(n_layers, d // tr),
        in_specs=[pl.BlockSpec((None, tr, n_cols), lambda l, i: (l, i, 0))],
        out_specs=(pl.BlockSpec((None, tr, n_main), lambda l, i: (l, i, 0)),
                   pl.BlockSpec((None, tr, HEAD_DIM), lambda l, i: (l, i, 0)),
                   pl.BlockSpec((None, tr, n_gates), lambda l, i: (l, i, 0))),
        compiler_params=_params(2),
        name="split_w_in",
    )(w_in)


def kernel(x, c, positions, norm1_g, norm2_g, ada_w, ada_b, w_in, dn_conv_w, dn_a_log, dn_dt_bias, dn_norm_g, pool_w, pool_scale, sg_ln_g, sg_ln_b, sg_w, sg_b, ret_gn_g, w_br_dn, w_br_pool, w_br_sg, w_br_ret, w_out, mlp_w1, mlp_w2, final_g):
    bsz, seq, d = x.shape
    n_layers = w_in.shape[0]
    n_tok = bsz * seq
    mod = _modulation(c, ada_w, ada_b)
    cos2, sin2 = _rope_tables(positions)
    x2 = x.reshape(n_tok, d)
    w_main, w_small, w_gates = _split_w_in(w_in)
    w_br = [w.astype(BF16) for w in (w_br_dn, w_br_pool, w_br_sg, w_br_ret)]
    w_out_b, w1_b, w2_b = w_out.astype(BF16), mlp_w1.astype(BF16), mlp_w2.astype(BF16)
    for l in range(n_layers):
        mod_l = mod[l].reshape(bsz, 1, N_MOD * d)
        ys = _proj_mixers(x2.reshape(bsz, seq, d), mod_l, norm1_g[l], w_main, w_small, l,
                          dn_conv_w[l], dn_a_log[l], dn_dt_bias[l], dn_norm_g[l], cos2, sin2, ret_gn_g[l],
                          pool_w[l], pool_scale[l], sg_ln_g[l], sg_ln_b[l], sg_w[l], sg_b[l])
        ys = [y.reshape(n_tok, WIDTH) for y in ys]
        x2 = _merge(ys, x2, mod_l, norm1_g[l], w_gates, w_br, w_out_b, l, seq)
        x2 = _mlp(x2, mod_l, norm2_g[l], w1_b, w2_b, final_g, l, seq, final_norm=(l == n_layers - 1))
    return x2.reshape(bsz, seq, d)
```

```python
import functools
import math

import jax
import jax.numpy as jnp
from jax import lax
from jax.experimental import pallas as pl
from jax.experimental.pallas import tpu as pltpu

F32 = jnp.float32
BF16 = jnp.bfloat16

EPS = 1e-6
N_HEADS = 4
HEAD_DIM = 128
WIDTH = N_HEADS * HEAD_DIM
N_BRANCH = 4
N_MOD = 6
DN_CONV = 4
DN_CHUNK = 64
CHUNK = 128
POOL_WINDOWS = (2, 4, 8, 16)
POOL_HALO = 16
ROPE_BASE = 10000.0
VMEM_LIMIT = 48 * 1024 * 1024

N_DN_CB = 4
CB_PL, CB_SU, CB_SV = range(3)
N_REST_CB = 3
N_RET_CB = 4


def _sigmoid(x):
    return 1.0 / (1.0 + jnp.exp(-x))


def _silu(x):
    return x * _sigmoid(x)


def _softplus(x):
    return jnp.maximum(x, 0.0) + jnp.log1p(jnp.exp(-jnp.abs(x)))


def _gelu(x):
    return 0.5 * x * (1.0 + lax.erf(x * (1.0 / math.sqrt(2.0))))


def _dot(a, b):
    return jnp.dot(a, b, preferred_element_type=F32)


def _dot_nt(a, b):
    return lax.dot_general(a, b, (((1,), (1,)), ((), ())), preferred_element_type=F32)


def _dot_tn(a, b):
    return lax.dot_general(a, b, (((0,), (0,)), ((), ())), preferred_element_type=F32)


def _params(n_axes):
    return pltpu.CompilerParams(dimension_semantics=("arbitrary",) * n_axes,
                                vmem_limit_bytes=VMEM_LIMIT)


def _mod_kernel(c_ref, w_ref, b_ref, o_ref):
    cond = _silu(c_ref[...])
    o_ref[...] = _dot(cond.astype(BF16), w_ref[...].astype(BF16)) + b_ref[...]


def _modulation(c, ada_w, ada_b):
    n_layers, d, n_out = ada_w.shape
    bsz = c.shape[0]
    rows = 8
    c_pad = jnp.zeros((rows, d), F32).at[:bsz].set(c)
    tn = n_out // 4
    out = pl.pallas_call(
        _mod_kernel,
        out_shape=jax.ShapeDtypeStruct((n_layers, rows, n_out), F32),
        grid=(n_layers, n_out // tn),
        in_specs=[pl.BlockSpec((rows, d), lambda l, j: (0, 0)),
                  pl.BlockSpec((None, d, tn), lambda l, j: (l, 0, j)),
                  pl.BlockSpec((None, 1, tn), lambda l, j: (l, 0, j))],
        out_specs=pl.BlockSpec((None, rows, tn), lambda l, j: (l, 0, j)),
        compiler_params=_params(2),
        name="adaln_mod",
    )(c_pad, ada_w, ada_b.reshape(n_layers, 1, n_out))
    return out[:, :bsz]


def _rope_kernel(pos_ref, inv_ref, sgn_ref, cos_ref, sin_ref):
    ang = pos_ref[...].astype(F32) * inv_ref[...]
    cos_ref[...] = jnp.cos(ang)
    sin_ref[...] = jnp.sin(ang) * sgn_ref[...]


def _rope_tables(positions):
    bsz, seq = positions.shape
    half = HEAD_DIM // 2
    inv = ROPE_BASE ** (-jnp.arange(0, HEAD_DIM, 2, dtype=F32) / HEAD_DIM)
    inv2 = jnp.concatenate([inv, inv]).reshape(1, HEAD_DIM)
    sgn = jnp.concatenate([-jnp.ones((half,), F32), jnp.ones((half,), F32)]).reshape(1, HEAD_DIM)
    n_tok = bsz * seq
    ts = min(1024, n_tok)
    shp = jax.ShapeDtypeStruct((n_tok, HEAD_DIM), F32)
    cos2, sin2 = pl.pallas_call(
        _rope_kernel,
        out_shape=(shp, shp),
        grid=(n_tok // ts,),
        in_specs=[pl.BlockSpec((ts, 1), lambda i: (i, 0)),
                  pl.BlockSpec((1, HEAD_DIM), lambda i: (0, 0)),
                  pl.BlockSpec((1, HEAD_DIM), lambda i: (0, 0))],
        out_specs=(pl.BlockSpec((ts, HEAD_DIM), lambda i: (i, 0)),
                   pl.BlockSpec((ts, HEAD_DIM), lambda i: (i, 0))),
        compiler_params=_params(1),
        name="rope_tables",
    )(positions.reshape(n_tok, 1), inv2, sgn)
    return cos2.reshape(bsz, seq, HEAD_DIM), sin2.reshape(bsz, seq, HEAD_DIM)


def _norm_modulate(x, gain, shift, scale):
    y = x * lax.rsqrt(jnp.mean(x * x, axis=-1, keepdims=True) + EPS)
    return (y * gain) * (1.0 + scale) + shift


def _proj_mixers_kernel(x_ref, mod_ref, g_ref, w_ref, ws_ref, cw_ref, ab_ref, ng_ref, cos_ref, sin_ref, gn_ref,
                        pw_ref, psc_ref, lng_ref, lnb_ref, sgw_ref, sgb_ref,
                        o_ref, ob_ref, oc_ref, od_ref, cbuf, s_ref, rs_ref, rc_ref, pbuf, *, bsz, d, n_sub):
    ts = n_sub * CHUNK
    cw = 3 * WIDTH
    nh = N_HEADS
    inst = [(b, c, h) for b in range(bsz) for c in range(n_sub) for h in range(nh)]
    seqs = [(b, h) for b in range(bsz) for h in range(nh)]
    log_gamma = [math.log1p(-2.0 ** (-5.0 - h)) for h in range(nh)]
    row = lax.broadcasted_iota(jnp.int32, (CHUNK, CHUNK), 0)
    col = lax.broadcasted_iota(jnp.int32, (CHUNK, CHUNK), 1)

    @pl.when(pl.program_id(0) == 0)
    def _():
        cbuf[:, 0:8, :] = jnp.zeros((bsz, 8, cw), F32)
        s_ref[...] = jnp.zeros_like(s_ref)
        rs_ref[...] = jnp.zeros_like(rs_ref)
        pbuf[:, 0:POOL_HALO, :] = jnp.zeros((bsz, POOL_HALO, WIDTH), F32)
        rel = (row - col).astype(F32)
        rowf = row.astype(F32)
        for h in range(nh):
            rc_ref[h] = jnp.where(row >= col, jnp.exp(log_gamma[h] * jnp.maximum(rel, 0.0)), 0.0)
            rc_ref[nh + h] = jnp.exp(log_gamma[h] * (float(CHUNK - 1) - rowf))
            rc_ref[2 * nh + h] = jnp.exp(log_gamma[h] * (rowf + 1.0))

    def lanes(h):
        return slice(h * HEAD_DIM, (h + 1) * HEAD_DIM)

    def sub(c):
        return slice(c * CHUNK, (c + 1) * CHUNK)

    def flat(b, c):
        return slice(b * ts + c * CHUNK, b * ts + (c + 1) * CHUNK)

    hb = jnp.concatenate(
        [_norm_modulate(x_ref[b], g_ref[...], mod_ref[b, :, 0:d], mod_ref[b, :, d:2 * d]).astype(BF16)
         for b in range(bsz)], axis=0)

    rest = {}

    def rest_project(j):
        if j not in rest:
            rest[j] = _dot(hb, w_ref[:, (N_DN_CB + j) * WIDTH:(N_DN_CB + j + 1) * WIDTH])

    def pool_stage():
        count = (pl.program_id(0) * ts + 1 + lax.broadcasted_iota(jnp.int32, (ts, HEAD_DIM), 0)).astype(F32)
        for b in range(bsz):
            p = rest[CB_PL][b * ts:(b + 1) * ts]
            pbuf[b, POOL_HALO:POOL_HALO + ts, :] = p
            for g, win in enumerate(POOL_WINDOWS):
                acc = p[:, lanes(g)]
                for dlt in range(1, win):
                    acc = acc + pbuf[b, POOL_HALO - dlt:POOL_HALO - dlt + ts, lanes(g)]
                pooled = acc / jnp.minimum(count, float(win)) - p[:, lanes(g)]
                y = _dot(pooled.astype(BF16), pw_ref[g]) * psc_ref[:, lanes(g)]
                ob_ref[b, :, lanes(g)] = y.astype(BF16)
            pbuf[b, 0:POOL_HALO, :] = pbuf[b, ts:ts + POOL_HALO, :]

    def sgu_stage():
        u = _gelu(rest[CB_SU])
        v = _gelu(rest[CB_SV])
        mu = jnp.mean(v, axis=-1, keepdims=True)
        vc = v - mu
        var = jnp.mean(vc * vc, axis=-1, keepdims=True)
        vb = (vc * lax.rsqrt(var + EPS) * lng_ref[...] + lnb_ref[...]).astype(BF16)
        for g in range(nh):
            wg = jnp.where(row >= col, sgw_ref[g], 0.0).astype(BF16)
            for b in range(bsz):
                for c in range(n_sub):
                    mixed = _dot(wg, vb[flat(b, c), lanes(g)]) + sgb_ref[:, lanes(g)]
                    oc_ref[b, sub(c), lanes(g)] = (u[flat(b, c), lanes(g)] * mixed).astype(BF16)

    ret = {}

    def ret_project():
        lo = (N_DN_CB + N_REST_CB) * WIDTH
        ret["in"] = [_dot(hb, w_ref[:, lo + j * WIDTH:lo + (j + 1) * WIDTH]) for j in range(N_RET_CB)]

    def ret_part(j, b, c, h):
        return ret["in"][j][flat(b, c), lanes(h)]

    def ret_scores():
        def rotary(t, b, c):
            return t * cos_ref[b, sub(c), :] + pltpu.roll(t, HEAD_DIM // 2, axis=1) * sin_ref[b, sub(c), :]

        ret["q"] = [rotary(ret_part(0, b, c, h), b, c).astype(BF16) for b, c, h in inst]
        ret["k"] = [rotary(ret_part(1, b, c, h), b, c) * HEAD_DIM ** -0.5 for b, c, h in inst]
        ret["v"] = [ret_part(2, b, c, h).astype(BF16) for b, c, h in inst]
        ret["scores"] = [(_dot_nt(qr, kr.astype(BF16)) * rc_ref[h]).astype(BF16)
                         for qr, kr, (b, c, h) in zip(ret["q"], ret["k"], inst)]

    def ret_state():
        state = [rs_ref[n] for n in range(len(seqs))]
        ret["o"] = {}
        for c in range(n_sub):
            for n, (b, h) in enumerate(seqs):
                i = inst.index((b, c, h))
                ret["o"][i] = (_dot(ret["scores"][i], ret["v"][i])
                               + _dot(ret["q"][i], state[n].astype(BF16)) * rc_ref[2 * nh + h])
                kv = _dot_tn((ret["k"][i] * rc_ref[nh + h]).astype(BF16), ret["v"][i])
                state[n] = state[n] * math.exp(log_gamma[h] * CHUNK) + kv
        for n, s in enumerate(state):
            rs_ref[n] = s

    def ret_store():
        for i, (b, c, h) in enumerate(inst):
            mu = jnp.mean(ret["o"][i], axis=-1, keepdims=True)
            oc = ret["o"][i] - mu
            var = jnp.mean(oc * oc, axis=-1, keepdims=True)
            y = oc * lax.rsqrt(var + EPS) * gn_ref[:, lanes(h)]
            od_ref[b, sub(c), lanes(h)] = (_silu(ret_part(3, b, c, h)) * y).astype(BF16)

    ret_stages = [ret_scores, ret_state, ret_store]

    sm_all = _dot(hb, ws_ref[...])
    qkvz = [_dot(hb, w_ref[:, j * WIDTH:(j + 1) * WIDTH]) for j in range(N_DN_CB)]

    def same_block(size):
        return (row // size) == (col // size)

    mask_incl = jnp.logical_and(same_block(DN_CHUNK), row >= col)
    mask_strict = jnp.logical_and(same_block(DN_CHUNK), row > col)
    eye = jnp.where(row == col, 1.0, 0.0).astype(F32)
    row_in_chunk = lax.broadcasted_iota(jnp.int32, (ts, HEAD_DIM), 0) % DN_CHUNK
    first_chunk = row < DN_CHUNK

    qkv, g_all, beta_all = [], [], []
    for b in range(bsz):
        rows = slice(b * ts, (b + 1) * ts)
        for part in range(3):
            cbuf[b, 8:8 + ts, part * WIDTH:(part + 1) * WIDTH] = qkvz[part][rows]
        acc = cbuf[b, 8:8 + ts, :] * cw_ref[DN_CONV - 1:DN_CONV, :]
        for kk in range(DN_CONV - 1):
            off = 8 - (DN_CONV - 1) + kk
            acc = acc + cbuf[b, off:off + ts, :] * cw_ref[kk:kk + 1, :]
        cbuf[b, 0:8, :] = cbuf[b, ts:ts + 8, :]
        qkv.append(_silu(acc))

        sm = sm_all[rows]
        beta_all.append(_sigmoid(sm))
        g = -jnp.exp(ab_ref[0:1, :]) * _softplus(sm + ab_ref[1:2, :])
        for s in (1, 2, 4, 8, 16, 32):
            g = g + jnp.where(row_in_chunk >= s, pltpu.roll(g, s, axis=0), 0.0)
        g_all.append(g)
        rest_project(min(b, N_REST_CB - 1))

    gb = [jnp.broadcast_to(g_all[b][sub(c), N_HEADS + h:N_HEADS + h + 1], (CHUNK, CHUNK))
          for b, c, h in inst]
    bb = [jnp.broadcast_to(beta_all[b][sub(c), h:h + 1], (CHUNK, CHUNK)) for b, c, h in inst]
    dec = [jnp.exp(jnp.where(mask_incl, g - g.T, -jnp.inf)) for g in gb]
    for j in range(N_REST_CB):
        rest_project(j)
    e_g = [jnp.exp(g) for g in gb]
    gl = [(g[DN_CHUNK - 1:DN_CHUNK, :], g[2 * DN_CHUNK - 1:2 * DN_CHUNK, :]) for g in gb]
    e_gl = [jnp.exp(jnp.where(first_chunk, l0, l1) - g) for g, (l0, l1) in zip(gb, gl)]

    def head(b, c, h, part):
        lo = part * WIDTH + h * HEAD_DIM
        return qkv[b][sub(c), lo:lo + HEAD_DIM]

    def l2n(t):
        return t * lax.rsqrt(jnp.sum(t * t, axis=-1, keepdims=True) + EPS)

    qn = [l2n(head(b, c, h, 0)) * HEAD_DIM ** -0.5 for b, c, h in inst]
    kn = [l2n(head(b, c, h, 1)) for b, c, h in inst]
    kb = [k * bt for k, bt in zip(kn, bb)]
    knb = [k.astype(BF16) for k in kn]
    lm = [jnp.where(mask_strict, _dot_nt(a.astype(BF16), k) * dc, 0.0) for a, k, dc in zip(kb, knb, dec)]
    qk = [(_dot_nt(q.astype(BF16), k) * dc).astype(BF16) for q, k, dc in zip(qn, knb, dec)]

    l8 = [jnp.where(same_block(8), m, 0.0) for m in lm]
    l8b = [m.astype(BF16) for m in l8]
    p1b = [_dot(m, m).astype(BF16) for m in l8b]
    tm = [eye - m for m in l8]
    tm = [t + _dot(t.astype(BF16), p) for t, p in zip(tm, p1b)]
    p2b = [_dot(p, p).astype(BF16) for p in p1b]
    tm = [t + _dot(t.astype(BF16), p) for t, p in zip(tm, p2b)]
    ret_project()
    for size, ret_stage in zip((8, 16, 32), ret_stages):
        lower_left = jnp.logical_and(same_block(2 * size), jnp.logical_not(same_block(size)))
        cm = [jnp.where(lower_left, m, 0.0).astype(BF16) for m in lm]
        tb = [t.astype(BF16) for t in tm]
        xm = [_dot(c_, t).astype(BF16) for c_, t in zip(cm, tb)]
        tm = [t - _dot(t16, x) for t, t16, x in zip(tm, tb, xm)]
        ret_stage()

    rhs = [jnp.concatenate([head(b, c, h, 2) * bt, k * e], axis=1).astype(BF16)
           for (b, c, h), bt, k, e in zip(inst, bb, kb, e_g)]
    uw = [_dot(t.astype(BF16), r) for t, r in zip(tm, rhs)]
    pool_stage()
    u = [m[:, 0:HEAD_DIM] for m in uw]
    w = [m[:, HEAD_DIM:2 * HEAD_DIM].astype(BF16) for m in uw]
    qs = [(q * e).astype(BF16) for q, e in zip(qn, e_g)]
    ks = [(k * e).astype(BF16) for k, e in zip(kn, e_gl)]

    state = [s_ref[n] for n in range(len(seqs))]
    vns = [[None, None] for _ in inst]
    outs = [[None, None] for _ in inst]
    for c in range(n_sub):
        ids = [inst.index((b, c, h)) for b, h in seqs]
        for half in range(2):
            rows = slice(half * DN_CHUNK, (half + 1) * DN_CHUNK)
            sb = [s.astype(BF16) for s in state]
            for n, i in enumerate(ids):
                vns[i][half] = u[i][rows] - _dot(w[i][rows], sb[n])
            for n, i in enumerate(ids):
                outs[i][half] = _dot(qs[i][rows], sb[n])
            state = [s * jnp.exp(gl[i][half]) + _dot_tn(ks[i][rows], vns[i][half].astype(BF16))
                     for s, i in zip(state, ids)]
    for n, s in enumerate(state):
        s_ref[n] = s
    sgu_stage()

    for i, (b, c, h) in enumerate(inst):
        vn = jnp.concatenate(vns[i], axis=0).astype(BF16)
        o = jnp.concatenate(outs[i], axis=0) + _dot(qk[i], vn)
        y = o * lax.rsqrt(jnp.mean(o * o, axis=-1, keepdims=True) + EPS) * ng_ref[...]
        y = y * _silu(qkvz[3][flat(b, c), lanes(h)])
        o_ref[b, sub(c), lanes(h)] = y.astype(BF16)


def _proj_mixers(x3, mod_l, norm_g, w_main, w_small, layer, conv_w, a_log, dt_bias, dn_norm_g, cos2, sin2, gn_g,
                 pool_w, pool_scale, ln_g, ln_b, sg_w, sg_b):
    bsz, seq, d = x3.shape
    n_sub = 2 if seq % (2 * CHUNK) == 0 else 1
    ts = n_sub * CHUNK
    n_in = (N_DN_CB + N_REST_CB + N_RET_CB) * WIDTH
    tab = pl.BlockSpec((bsz, ts, HEAD_DIM), lambda s: (0, s, 0))
    branch = jax.ShapeDtypeStruct((bsz, seq, WIDTH), BF16)
    state = pltpu.VMEM((bsz * N_HEADS, HEAD_DIM, HEAD_DIM), F32)
    ab = jnp.zeros((2, HEAD_DIM), F32)
    ab = ab.at[0, N_HEADS:2 * N_HEADS].set(a_log).at[1, N_HEADS:2 * N_HEADS].set(dt_bias)
    sg_bias = jnp.repeat(sg_b.T, HEAD_DIM, axis=1)
    row_vec = pl.BlockSpec((1, WIDTH), lambda s: (0, 0))
    group_mat = pl.BlockSpec((N_HEADS, CHUNK, CHUNK), lambda s: (0, 0, 0))
    branch_spec = pl.BlockSpec((bsz, ts, WIDTH), lambda s: (0, s, 0))
    return pl.pallas_call(
        functools.partial(_proj_mixers_kernel, bsz=bsz, d=d, n_sub=n_sub),
        out_shape=(branch, branch, branch, branch),
        grid=(seq // ts,),
        in_specs=[pl.BlockSpec((bsz, ts, d), lambda s: (0, s, 0)),
                  pl.BlockSpec((bsz, 1, N_MOD * d), lambda s: (0, 0, 0)),
                  pl.BlockSpec((1, d), lambda s: (0, 0)),
                  pl.BlockSpec((None, d, n_in), lambda s: (layer, 0, 0)),
                  pl.BlockSpec((None, d, HEAD_DIM), lambda s: (layer, 0, 0)),
                  pl.BlockSpec((DN_CONV, 3 * WIDTH), lambda s: (0, 0)),
                  pl.BlockSpec((2, HEAD_DIM), lambda s: (0, 0)),
                  pl.BlockSpec((1, HEAD_DIM), lambda s: (0, 0)),
                  tab, tab, row_vec,
                  group_mat, row_vec, row_vec, row_vec, group_mat,
                  pl.BlockSpec((CHUNK, WIDTH), lambda s: (0, 0))],
        out_specs=(branch_spec, branch_spec, branch_spec, branch_spec),
        scratch_shapes=[pltpu.VMEM((bsz, ts + 8, 3 * WIDTH), F32), state, state,
                        pltpu.VMEM((3 * N_HEADS, CHUNK, CHUNK), F32),
                        pltpu.VMEM((bsz, ts + POOL_HALO, WIDTH), F32)],
        compiler_params=_params(1),
        name="proj_mixers",
    )(x3, mod_l, norm_g.reshape(1, d), w_main, w_small, conv_w, ab, dn_norm_g.reshape(1, HEAD_DIM),
      cos2, sin2, gn_g.reshape(1, WIDTH),
      pool_w.astype(BF16), pool_scale.reshape(1, WIDTH), ln_g.reshape(1, WIDTH), ln_b.reshape(1, WIDTH), sg_w, sg_bias)


def _merge_kernel(ya_ref, yb_ref, yc_ref, yd_ref, x_ref, mod_ref, g_ref, wg_ref, wa_ref, wb_ref, wc_ref, wd_ref,
                  wo_ref, o_ref, *, d):
    x = x_ref[...]
    hb = _norm_modulate(x, g_ref[...], mod_ref[:, 0:d], mod_ref[:, d:2 * d]).astype(BF16)
    merged = None
    branches = ((ya_ref, wa_ref), (yb_ref, wb_ref), (yc_ref, wc_ref), (yd_ref, wd_ref))
    for i, (y_ref, wbr_ref) in enumerate(branches):
        gate = _sigmoid(_dot(hb, wg_ref[:, i * d:(i + 1) * d]))
        term = gate * _dot(y_ref[...], wbr_ref[...])
        merged = term if merged is None else merged + term
    r = _dot(merged.astype(BF16), wo_ref[...])
    o_ref[...] = x + mod_ref[:, 2 * d:3 * d] * r


def _merge(ys, x2, mod_l, norm_g, w_gates, w_br, w_out, layer, seq):
    n_tok, d = x2.shape
    ts = min(512, seq)
    per_seq = seq // ts
    yspec = pl.BlockSpec((ts, WIDTH), lambda i: (i, 0))
    wspec = pl.BlockSpec((None, WIDTH, d), lambda i: (layer, 0, 0))
    return pl.pallas_call(
        functools.partial(_merge_kernel, d=d),
        out_shape=jax.ShapeDtypeStruct((n_tok, d), F32),
        grid=(n_tok // ts,),
        in_specs=[yspec, yspec, yspec, yspec,
                  pl.BlockSpec((ts, d), lambda i: (i, 0)),
                  pl.BlockSpec((None, 1, N_MOD * d), lambda i: (i // per_seq, 0, 0)),
                  pl.BlockSpec((1, d), lambda i: (0, 0)),
                  pl.BlockSpec((None, d, N_BRANCH * d), lambda i: (layer, 0, 0)),
                  wspec, wspec, wspec, wspec,
                  pl.BlockSpec((None, d, d), lambda i: (layer, 0, 0))],
        out_specs=pl.BlockSpec((ts, d), lambda i: (i, 0)),
        compiler_params=_params(1),
        name="merge_out",
    )(*ys, x2, mod_l, norm_g.reshape(1, d), w_gates, *w_br, w_out)


def _mlp_kernel(x_ref, mod_ref, g_ref, w1_ref, w2_ref, fg_ref, o_ref, *, d, tf, final_norm):
    x = x_ref[...]
    hb = _norm_modulate(x, g_ref[...], mod_ref[:, 3 * d:4 * d], mod_ref[:, 4 * d:5 * d]).astype(BF16)
    acc = None
    for f in range(w1_ref.shape[1] // tf):
        a = jnp.maximum(_dot(hb, w1_ref[:, f * tf:(f + 1) * tf]), 0.0)
        part = _dot((a * a).astype(BF16), w2_ref[f * tf:(f + 1) * tf, :])
        acc = part if acc is None else acc + part
    out = x + mod_ref[:, 5 * d:6 * d] * acc
    if final_norm:
        out = out * lax.rsqrt(jnp.mean(out * out, axis=-1, keepdims=True) + EPS) * fg_ref[...]
    o_ref[...] = out


def _mlp(x2, mod_l, norm_g, w1, w2, final_g, layer, seq, final_norm):
    n_tok, d = x2.shape
    d_ff = w1.shape[-1]
    ts = min(512, seq)
    per_seq = seq // ts
    return pl.pallas_call(
        functools.partial(_mlp_kernel, d=d, tf=min(1024, d_ff), final_norm=final_norm),
        out_shape=jax.ShapeDtypeStruct((n_tok, d), F32),
        grid=(n_tok // ts,),
        in_specs=[pl.BlockSpec((ts, d), lambda i: (i, 0)),
                  pl.BlockSpec((None, 1, N_MOD * d), lambda i: (i // per_seq, 0, 0)),
                  pl.BlockSpec((1, d), lambda i: (0, 0)),
                  pl.BlockSpec((None, d, d_ff), lambda i: (layer, 0, 0)),
                  pl.BlockSpec((None, d_ff, d), lambda i: (layer, 0, 0)),
                  pl.BlockSpec((1, d), lambda i: (0, 0))],
        out_specs=pl.BlockSpec((ts, d), lambda i: (i, 0)),
        compiler_params=_params(1),
        name="mlp",
    )(x2, mod_l, norm_g.reshape(1, d), w1, w2, final_g.reshape(1, d))


def _split_w_in_kernel(wt_ref, main_ref, gates_ref, *, n_main_blocks):
    block = wt_ref[0].T.astype(BF16)
    j = pl.program_id(1)

    @pl.when(j < n_main_blocks)
    def _():
        main_ref[...] = block

    @pl.when(j >= n_main_blocks)
    def _():
        gates_ref[...] = block


def _split_w_in(w_in):
    n_layers, d, n_cols = w_in.shape
    n_main_blocks = N_DN_CB + N_REST_CB + N_RET_CB
    o_small = N_DN_CB * WIDTH
    o_rest = o_small + 2 * N_HEADS
    o_gates = o_rest + (N_REST_CB + N_RET_CB) * WIDTH
    n_gate_blocks = (n_cols - o_gates) // WIDTH
    w_t = jnp.swapaxes(w_in, 1, 2)

    def first_row(j):
        return jnp.where(j < N_DN_CB, j * WIDTH,
                         jnp.where(j < n_main_blocks, o_rest + (j - N_DN_CB) * WIDTH,
                                   o_gates + (j - n_main_blocks) * WIDTH))

    main, gates = pl.pallas_call(
        functools.partial(_split_w_in_kernel, n_main_blocks=n_main_blocks),
        out_shape=(jax.ShapeDtypeStruct((n_layers, d, n_main_blocks * WIDTH), BF16),
                   jax.ShapeDtypeStruct((n_layers, d, n_gate_blocks * WIDTH), BF16)),
        grid=(n_layers, n_main_blocks + n_gate_blocks),
        in_specs=[pl.BlockSpec((pl.Element(1), pl.Element(WIDTH), pl.Element(d)),
                               lambda l, j: (l, pl.multiple_of(first_row(j), 8), 0))],
        out_specs=(pl.BlockSpec((None, d, WIDTH), lambda l, j: (l, 0, jnp.minimum(j, n_main_blocks - 1))),
                   pl.BlockSpec((None, d, WIDTH), lambda l, j: (l, 0, jnp.maximum(j - n_main_blocks, 0)))),
        compiler_params=_params(2),
        name="split_w_in",
    )(w_t)
    small = jnp.swapaxes(w_t[:, o_small:o_rest, :], 1, 2).astype(BF16)
    small = jnp.concatenate([small, jnp.zeros((n_layers, d, HEAD_DIM - 2 * N_HEADS), BF16)], axis=-1)
    return main, small, gates


def kernel(x, c, positions, norm1_g, norm2_g, ada_w, ada_b, w_in, dn_conv_w, dn_a_log, dn_dt_bias, dn_norm_g, pool_w, pool_scale, sg_ln_g, sg_ln_b, sg_w, sg_b, ret_gn_g, w_br_dn, w_br_pool, w_br_sg, w_br_ret, w_out, mlp_w1, mlp_w2, final_g):
    bsz, seq, d = x.shape
    n_layers = w_in.shape[0]
    n_tok = bsz * seq
    mod = _modulation(c, ada_w, ada_b)
    cos2, sin2 = _rope_tables(positions)
    x2 = x.reshape(n_tok, d)
    w_main, w_small, w_gates = _split_w_in(w_in)
    w_br = [w.astype(BF16) for w in (w_br_dn, w_br_pool, w_br_sg, w_br_ret)]
    w_out_b, w1_b, w2_b = w_out.astype(BF16), mlp_w1.astype(BF16), mlp_w2.astype(BF16)
    for l in range(n_layers):
        mod_l = mod[l].reshape(bsz, 1, N_MOD * d)
        ys = _proj_mixers(x2.reshape(bsz, seq, d), mod_l, norm1_g[l], w_main, w_small, l,
                          dn_conv_w[l], dn_a_log[l], dn_dt_bias[l], dn_norm_g[l], cos2, sin2, ret_gn_g[l],
                          pool_w[l], pool_scale[l], sg_ln_g[l], sg_ln_b[l], sg_w[l], sg_b[l])
        ys = [y.reshape(n_tok, WIDTH) for y in ys]
        x2 = _merge(ys, x2, mod_l, norm1_g[l], w_gates, w_br, w_out_b, l, seq)
        x2 = _mlp(x2, mod_l, norm2_g[l], w1_b, w2_b, final_g, l, seq, final_norm=(l == n_layers - 1))
    return x2.reshape(bsz, seq, d)
```

```python
import functools
import math

import jax
import jax.numpy as jnp
from jax import lax
from jax.experimental import pallas as pl
from jax.experimental.pallas import tpu as pltpu

F32 = jnp.float32
BF16 = jnp.bfloat16

EPS = 1e-6
N_HEADS = 4
HEAD_DIM = 128
WIDTH = N_HEADS * HEAD_DIM
N_BRANCH = 4
N_MOD = 6
DN_CONV = 4
DN_CHUNK = 64
CHUNK = 128
POOL_WINDOWS = (2, 4, 8, 16)
POOL_HALO = 16
ROPE_BASE = 10000.0
VMEM_LIMIT = 48 * 1024 * 1024

N_DN_CB = 4
CB_PL, CB_SU, CB_SV = range(3)
N_REST_CB = 3
N_RET_CB = 4


def _sigmoid(x):
    return 1.0 / (1.0 + jnp.exp(-x))


def _silu(x):
    return x * _sigmoid(x)


def _softplus(x):
    return jnp.maximum(x, 0.0) + jnp.log1p(jnp.exp(-jnp.abs(x)))


def _gelu(x):
    return 0.5 * x * (1.0 + lax.erf(x * (1.0 / math.sqrt(2.0))))


def _dot(a, b):
    return jnp.dot(a, b, preferred_element_type=F32)


def _dot_nt(a, b):
    return lax.dot_general(a, b, (((1,), (1,)), ((), ())), preferred_element_type=F32)


def _dot_tn(a, b):
    return lax.dot_general(a, b, (((0,), (0,)), ((), ())), preferred_element_type=F32)


def _params(n_axes):
    return pltpu.CompilerParams(dimension_semantics=("arbitrary",) * n_axes,
                                vmem_limit_bytes=VMEM_LIMIT)


def _mod_kernel(c_ref, w_ref, b_ref, o_ref):
    cond = _silu(c_ref[...])
    o_ref[...] = _dot(cond.astype(BF16), w_ref[...].astype(BF16)) + b_ref[...]


def _modulation(c, ada_w, ada_b):
    n_layers, d, n_out = ada_w.shape
    bsz = c.shape[0]
    rows = 8
    c_pad = jnp.zeros((rows, d), F32).at[:bsz].set(c)
    tn = n_out // 4
    out = pl.pallas_call(
        _mod_kernel,
        out_shape=jax.ShapeDtypeStruct((n_layers, rows, n_out), F32),
        grid=(n_layers, n_out // tn),
        in_specs=[pl.BlockSpec((rows, d), lambda l, j: (0, 0)),
                  pl.BlockSpec((None, d, tn), lambda l, j: (l, 0, j)),
                  pl.BlockSpec((None, 1, tn), lambda l, j: (l, 0, j))],
        out_specs=pl.BlockSpec((None, rows, tn), lambda l, j: (l, 0, j)),
        compiler_params=_params(2),
        name="adaln_mod",
    )(c_pad, ada_w, ada_b.reshape(n_layers, 1, n_out))
    return out[:, :bsz]


def _rope_kernel(pos_ref, inv_ref, sgn_ref, cos_ref, sin_ref):
    ang = pos_ref[...].astype(F32) * inv_ref[...]
    cos_ref[...] = jnp.cos(ang)
    sin_ref[...] = jnp.sin(ang) * sgn_ref[...]


def _rope_tables(positions):
    bsz, seq = positions.shape
    half = HEAD_DIM // 2
    inv = ROPE_BASE ** (-jnp.arange(0, HEAD_DIM, 2, dtype=F32) / HEAD_DIM)
    inv2 = jnp.concatenate([inv, inv]).reshape(1, HEAD_DIM)
    sgn = jnp.concatenate([-jnp.ones((half,), F32), jnp.ones((half,), F32)]).reshape(1, HEAD_DIM)
    n_tok = bsz * seq
    ts = min(1024, n_tok)
    shp = jax.ShapeDtypeStruct((n_tok, HEAD_DIM), F32)
    cos2, sin2 = pl.pallas_call(
        _rope_kernel,
        out_shape=(shp, shp),
        grid=(n_tok // ts,),
        in_specs=[pl.BlockSpec((ts, 1), lambda i: (i, 0)),
                  pl.BlockSpec((1, HEAD_DIM), lambda i: (0, 0)),
                  pl.BlockSpec((1, HEAD_DIM), lambda i: (0, 0))],
        out_specs=(pl.BlockSpec((ts, HEAD_DIM), lambda i: (i, 0)),
                   pl.BlockSpec((ts, HEAD_DIM), lambda i: (i, 0))),
        compiler_params=_params(1),
        name="rope_tables",
    )(positions.reshape(n_tok, 1), inv2, sgn)
    return cos2.reshape(bsz, seq, HEAD_DIM), sin2.reshape(bsz, seq, HEAD_DIM)


def _norm_modulate(x, gain, shift, scale):
    y = x * lax.rsqrt(jnp.mean(x * x, axis=-1, keepdims=True) + EPS)
    return (y * gain) * (1.0 + scale) + shift


def _proj_mixers_kernel(x_ref, mod_ref, g_ref, w_ref, ws_ref, cw_ref, ab_ref, ng_ref, cos_ref, sin_ref, gn_ref,
                        pw_ref, psc_ref, lng_ref, lnb_ref, sgw_ref, sgb_ref,
                        o_ref, ob_ref, oc_ref, od_ref, cbuf, s_ref, rs_ref, rc_ref, pbuf, *, bsz, d, n_sub):
    ts = n_sub * CHUNK
    cw = 3 * WIDTH
    nh = N_HEADS
    inst = [(b, c, h) for b in range(bsz) for c in range(n_sub) for h in range(nh)]
    seqs = [(b, h) for b in range(bsz) for h in range(nh)]
    log_gamma = [math.log1p(-2.0 ** (-5.0 - h)) for h in range(nh)]
    row = lax.broadcasted_iota(jnp.int32, (CHUNK, CHUNK), 0)
    col = lax.broadcasted_iota(jnp.int32, (CHUNK, CHUNK), 1)

    @pl.when(pl.program_id(0) == 0)
    def _():
        cbuf[:, 0:8, :] = jnp.zeros((bsz, 8, cw), F32)
        s_ref[...] = jnp.zeros_like(s_ref)
        rs_ref[...] = jnp.zeros_like(rs_ref)
        pbuf[:, 0:POOL_HALO, :] = jnp.zeros((bsz, POOL_HALO, WIDTH), F32)
        rel = (row - col).astype(F32)
        rowf = row.astype(F32)
        for h in range(nh):
            rc_ref[h] = jnp.where(row >= col, jnp.exp(log_gamma[h] * jnp.maximum(rel, 0.0)), 0.0)
            rc_ref[nh + h] = jnp.exp(log_gamma[h] * (float(CHUNK - 1) - rowf))
            rc_ref[2 * nh + h] = jnp.exp(log_gamma[h] * (rowf + 1.0))

    def lanes(h):
        return slice(h * HEAD_DIM, (h + 1) * HEAD_DIM)

    def sub(c):
        return slice(c * CHUNK, (c + 1) * CHUNK)

    def flat(b, c):
        return slice(b * ts + c * CHUNK, b * ts + (c + 1) * CHUNK)

    hb = jnp.concatenate(
        [_norm_modulate(x_ref[b], g_ref[...], mod_ref[b, :, 0:d], mod_ref[b, :, d:2 * d]).astype(BF16)
         for b in range(bsz)], axis=0)

    rest = {}

    def rest_project(j):
        if j not in rest:
            rest[j] = _dot(hb, w_ref[:, (N_DN_CB + j) * WIDTH:(N_DN_CB + j + 1) * WIDTH])

    def pool_stage():
        count = (pl.program_id(0) * ts + 1 + lax.broadcasted_iota(jnp.int32, (ts, HEAD_DIM), 0)).astype(F32)
        for b in range(bsz):
            p = rest[CB_PL][b * ts:(b + 1) * ts]
            pbuf[b, POOL_HALO:POOL_HALO + ts, :] = p
            for g, win in enumerate(POOL_WINDOWS):
                acc = p[:, lanes(g)]
                for dlt in range(1, win):
                    acc = acc + pbuf[b, POOL_HALO - dlt:POOL_HALO - dlt + ts, lanes(g)]
                pooled = acc / jnp.minimum(count, float(win)) - p[:, lanes(g)]
                y = _dot(pooled.astype(BF16), pw_ref[g]) * psc_ref[:, lanes(g)]
                ob_ref[b, :, lanes(g)] = y.astype(BF16)
            pbuf[b, 0:POOL_HALO, :] = pbuf[b, ts:ts + POOL_HALO, :]

    def sgu_stage():
        u = _gelu(rest[CB_SU])
        v = _gelu(rest[CB_SV])
        mu = jnp.mean(v, axis=-1, keepdims=True)
        vc = v - mu
        var = jnp.mean(vc * vc, axis=-1, keepdims=True)
        vb = (vc * lax.rsqrt(var + EPS) * lng_ref[...] + lnb_ref[...]).astype(BF16)
        for g in range(nh):
            wg = jnp.where(row >= col, sgw_ref[g], 0.0).astype(BF16)
            for b in range(bsz):
                for c in range(n_sub):
                    mixed = _dot(wg, vb[flat(b, c), lanes(g)]) + sgb_ref[:, lanes(g)]
                    oc_ref[b, sub(c), lanes(g)] = (u[flat(b, c), lanes(g)] * mixed).astype(BF16)

    ret = {}

    def ret_project():
        lo = (N_DN_CB + N_REST_CB) * WIDTH
        ret["in"] = [_dot(hb, w_ref[:, lo + j * WIDTH:lo + (j + 1) * WIDTH]) for j in range(N_RET_CB)]

    def ret_part(j, b, c, h):
        return ret["in"][j][flat(b, c), lanes(h)]

    def ret_scores():
        def rotary(t, b, c):
            return t * cos_ref[b, sub(c), :] + pltpu.roll(t, HEAD_DIM // 2, axis=1) * sin_ref[b, sub(c), :]

        ret["q"] = [rotary(ret_part(0, b, c, h), b, c).astype(BF16) for b, c, h in inst]
        ret["k"] = [rotary(ret_part(1, b, c, h), b, c) * HEAD_DIM ** -0.5 for b, c, h in inst]
        ret["v"] = [ret_part(2, b, c, h).astype(BF16) for b, c, h in inst]
        ret["scores"] = [(_dot_nt(qr, kr.astype(BF16)) * rc_ref[h]).astype(BF16)
                         for qr, kr, (b, c, h) in zip(ret["q"], ret["k"], inst)]

    def ret_state():
        state = [rs_ref[n] for n in range(len(seqs))]
        ret["o"] = {}
        for c in range(n_sub):
            for n, (b, h) in enumerate(seqs):
                i = inst.index((b, c, h))
                ret["o"][i] = (_dot(ret["scores"][i], ret["v"][i])
                               + _dot(ret["q"][i], state[n].astype(BF16)) * rc_ref[2 * nh + h])
                kv = _dot_tn((ret["k"][i] * rc_ref[nh + h]).astype(BF16), ret["v"][i])
                state[n] = state[n] * math.exp(log_gamma[h] * CHUNK) + kv
        for n, s in enumerate(state):
            rs_ref[n] = s

    def ret_store():
        for i, (b, c, h) in enumerate(inst):
            mu = jnp.mean(ret["o"][i], axis=-1, keepdims=True)
            oc = ret["o"][i] - mu
            var = jnp.mean(oc * oc, axis=-1, keepdims=True)
            y = oc * lax.rsqrt(var + EPS) * gn_ref[:, lanes(h)]
            od_ref[b, sub(c), lanes(h)] = (_silu(ret_part(3, b, c, h)) * y).astype(BF16)

    ret_stages = [ret_scores, ret_state, ret_store]

    sm_all = _dot(hb, ws_ref[...])
    qkvz = [_dot(hb, w_ref[:, j * WIDTH:(j + 1) * WIDTH]) for j in range(N_DN_CB)]

    def same_block(size):
        return (row // size) == (col // size)

    mask_incl = jnp.logical_and(same_block(DN_CHUNK), row >= col)
    mask_strict = jnp.logical_and(same_block(DN_CHUNK), row > col)
    eye = jnp.where(row == col, 1.0, 0.0).astype(F32)
    row_in_chunk = lax.broadcasted_iota(jnp.int32, (ts, HEAD_DIM), 0) % DN_CHUNK
    first_chunk = row < DN_CHUNK

    qkv, g_all, beta_all = [], [], []
    for b in range(bsz):
        rows = slice(b * ts, (b + 1) * ts)
        for part in range(3):
            cbuf[b, 8:8 + ts, part * WIDTH:(part + 1) * WIDTH] = qkvz[part][rows]
        acc = cbuf[b, 8:8 + ts, :] * cw_ref[DN_CONV - 1:DN_CONV, :]
        for kk in range(DN_CONV - 1):
            off = 8 - (DN_CONV - 1) + kk
            acc = acc + cbuf[b, off:off + ts, :] * cw_ref[kk:kk + 1, :]
        cbuf[b, 0:8, :] = cbuf[b, ts:ts + 8, :]
        qkv.append(_silu(acc))

        sm = sm_all[rows]
        beta_all.append(_sigmoid(sm))
        g = -jnp.exp(ab_ref[0:1, :]) * _softplus(sm + ab_ref[1:2, :])
        for s in (1, 2, 4, 8, 16, 32):
            g = g + jnp.where(row_in_chunk >= s, pltpu.roll(g, s, axis=0), 0.0)
        g_all.append(g)
        rest_project(min(b, N_REST_CB - 1))

    gb = [jnp.broadcast_to(g_all[b][sub(c), N_HEADS + h:N_HEADS + h + 1], (CHUNK, CHUNK))
          for b, c, h in inst]
    bb = [jnp.broadcast_to(beta_all[b][sub(c), h:h + 1], (CHUNK, CHUNK)) for b, c, h in inst]
    dec = [jnp.exp(jnp.where(mask_incl, g - g.T, -jnp.inf)) for g in gb]
    for j in range(N_REST_CB):
        rest_project(j)
    e_g = [jnp.exp(g) for g in gb]
    gl = [(g[DN_CHUNK - 1:DN_CHUNK, :], g[2 * DN_CHUNK - 1:2 * DN_CHUNK, :]) for g in gb]
    e_gl = [jnp.exp(jnp.where(first_chunk, l0, l1) - g) for g, (l0, l1) in zip(gb, gl)]

    def head(b, c, h, part):
        lo = part * WIDTH + h * HEAD_DIM
        return qkv[b][sub(c), lo:lo + HEAD_DIM]

    def l2n(t):
        return t * lax.rsqrt(jnp.sum(t * t, axis=-1, keepdims=True) + EPS)

    qn = [l2n(head(b, c, h, 0)) * HEAD_DIM ** -0.5 for b, c, h in inst]
    kn = [l2n(head(b, c, h, 1)) for b, c, h in inst]
    kb = [k * bt for k, bt in zip(kn, bb)]
    knb = [k.astype(BF16) for k in kn]
    lm = [jnp.where(mask_strict, _dot_nt(a.astype(BF16), k) * dc, 0.0) for a, k, dc in zip(kb, knb, dec)]
    qk = [(_dot_nt(q.astype(BF16), k) * dc).astype(BF16) for q, k, dc in zip(qn, knb, dec)]

    l8 = [jnp.where(same_block(8), m, 0.0) for m in lm]
    l8b = [m.astype(BF16) for m in l8]
    p1b = [_dot(m, m).astype(BF16) for m in l8b]
    tm = [eye - m for m in l8]
    tm = [t + _dot(t.astype(BF16), p) for t, p in zip(tm, p1b)]
    p2b = [_dot(p, p).astype(BF16) for p in p1b]
    tm = [t + _dot(t.astype(BF16), p) for t, p in zip(tm, p2b)]
    ret_project()
    for size, ret_stage in zip((8, 16, 32), ret_stages):
        lower_left = jnp.logical_and(same_block(2 * size), jnp.logical_not(same_block(size)))
        cm = [jnp.where(lower_left, m, 0.0).astype(BF16) for m in lm]
        tb = [t.astype(BF16) for t in tm]
        xm = [_dot(c_, t).astype(BF16) for c_, t in zip(cm, tb)]
        tm = [t - _dot(t16, x) for t, t16, x in zip(tm, tb, xm)]
        ret_stage()

    rhs = [jnp.concatenate([head(b, c, h, 2) * bt, k * e], axis=1).astype(BF16)
           for (b, c, h), bt, k, e in zip(inst, bb, kb, e_g)]
    uw = [_dot(t.astype(BF16), r) for t, r in zip(tm, rhs)]
    pool_stage()
    u = [m[:, 0:HEAD_DIM] for m in uw]
    w = [m[:, HEAD_DIM:2 * HEAD_DIM].astype(BF16) for m in uw]
    qs = [(q * e).astype(BF16) for q, e in zip(qn, e_g)]
    ks = [(k * e).astype(BF16) for k, e in zip(kn, e_gl)]

    state = [s_ref[n] for n in range(len(seqs))]
    vns = [[None, None] for _ in inst]
    outs = [[None, None] for _ in inst]
    for c in range(n_sub):
        ids = [inst.index((b, c, h)) for b, h in seqs]
        for half in range(2):
            rows = slice(half * DN_CHUNK, (half + 1) * DN_CHUNK)
            sb = [s.astype(BF16) for s in state]
            for n, i in enumerate(ids):
                vns[i][half] = u[i][rows] - _dot(w[i][rows], sb[n])
            for n, i in enumerate(ids):
                outs[i][half] = _dot(qs[i][rows], sb[n])
            state = [s * jnp.exp(gl[i][half]) + _dot_tn(ks[i][rows], vns[i][half].astype(BF16))
                     for s, i in zip(state, ids)]
    for n, s in enumerate(state):
        s_ref[n] = s
    sgu_stage()

    for i, (b, c, h) in enumerate(inst):
        vn = jnp.concatenate(vns[i], axis=0).astype(BF16)
        o = jnp.concatenate(outs[i], axis=0) + _dot(qk[i], vn)
        y = o * lax.rsqrt(jnp.mean(o * o, axis=-1, keepdims=True) + EPS) * ng_ref[...]
        y = y * _silu(qkvz[3][flat(b, c), lanes(h)])
        o_ref[b, sub(c), lanes(h)] = y.astype(BF16)


def _proj_mixers(x3, mod_l, norm_g, w_main, w_small, layer, conv_w, a_log, dt_bias, dn_norm_g, cos2, sin2, gn_g,
                 pool_w, pool_scale, ln_g, ln_b, sg_w, sg_b):
    bsz, seq, d = x3.shape
    n_sub = 2 if seq % (2 * CHUNK) == 0 else 1
    ts = n_sub * CHUNK
    n_in = (N_DN_CB + N_REST_CB + N_RET_CB) * WIDTH
    tab = pl.BlockSpec((bsz, ts, HEAD_DIM), lambda s: (0, s, 0))
    branch = jax.ShapeDtypeStruct((bsz, seq, WIDTH), BF16)
    state = pltpu.VMEM((bsz * N_HEADS, HEAD_DIM, HEAD_DIM), F32)
    ab = jnp.zeros((2, HEAD_DIM), F32)
    ab = ab.at[0, N_HEADS:2 * N_HEADS].set(a_log).at[1, N_HEADS:2 * N_HEADS].set(dt_bias)
    sg_bias = jnp.repeat(sg_b.T, HEAD_DIM, axis=1)
    row_vec = pl.BlockSpec((1, WIDTH), lambda s: (0, 0))
    group_mat = pl.BlockSpec((N_HEADS, CHUNK, CHUNK), lambda s: (0, 0, 0))
    branch_spec = pl.BlockSpec((bsz, ts, WIDTH), lambda s: (0, s, 0))
    return pl.pallas_call(
        functools.partial(_proj_mixers_kernel, bsz=bsz, d=d, n_sub=n_sub),
        out_shape=(branch, branch, branch, branch),
        grid=(seq // ts,),
        in_specs=[pl.BlockSpec((bsz, ts, d), lambda s: (0, s, 0)),
                  pl.BlockSpec((bsz, 1, N_MOD * d), lambda s: (0, 0, 0)),
                  pl.BlockSpec((1, d), lambda s: (0, 0)),
                  pl.BlockSpec((None, d, n_in), lambda s: (layer, 0, 0)),
                  pl.BlockSpec((None, d, HEAD_DIM), lambda s: (layer, 0, 0)),
                  pl.BlockSpec((DN_CONV, 3 * WIDTH), lambda s: (0, 0)),
                  pl.BlockSpec((2, HEAD_DIM), lambda s: (0, 0)),
                  pl.BlockSpec((1, HEAD_DIM), lambda s: (0, 0)),
                  tab, tab, row_vec,
                  group_mat, row_vec, row_vec, row_vec, group_mat,
                  pl.BlockSpec((CHUNK, WIDTH), lambda s: (0, 0))],
        out_specs=(branch_spec, branch_spec, branch_spec, branch_spec),
        scratch_shapes=[pltpu.VMEM((bsz, ts + 8, 3 * WIDTH), F32), state, state,
                        pltpu.VMEM((3 * N_HEADS, CHUNK, CHUNK), F32),
                        pltpu.VMEM((bsz, ts + POOL_HALO, WIDTH), F32)],
        compiler_params=_params(1),
        name="proj_mixers",
    )(x3, mod_l, norm_g.reshape(1, d), w_main, w_small, conv_w, ab, dn_norm_g.reshape(1, HEAD_DIM),
      cos2, sin2, gn_g.reshape(1, WIDTH),
      pool_w.astype(BF16), pool_scale.reshape(1, WIDTH), ln_g.reshape(1, WIDTH), ln_b.reshape(1, WIDTH), sg_w, sg_bias)


def _merge_kernel(ya_ref, yb_ref, yc_ref, yd_ref, x_ref, mod_ref, g_ref, wg_ref, wa_ref, wb_ref, wc_ref, wd_ref,
                  wo_ref, o_ref, *, d):
    x = x_ref[...]
    hb = _norm_modulate(x, g_ref[...], mod_ref[:, 0:d], mod_ref[:, d:2 * d]).astype(BF16)
    merged = None
    branches = ((ya_ref, wa_ref), (yb_ref, wb_ref), (yc_ref, wc_ref), (yd_ref, wd_ref))
    for i, (y_ref, wbr_ref) in enumerate(branches):
        gate = _sigmoid(_dot(hb, wg_ref[:, i * d:(i + 1) * d]))
        term = gate * _dot(y_ref[...], wbr_ref[...])
        merged = term if merged is None else merged + term
    r = _dot(merged.astype(BF16), wo_ref[...])
    o_ref[...] = x + mod_ref[:, 2 * d:3 * d] * r


def _merge(ys, x2, mod_l, norm_g, w_gates, w_br, w_out, layer, seq):
    n_tok, d = x2.shape
    ts = min(512, seq)
    per_seq = seq // ts
    yspec = pl.BlockSpec((ts, WIDTH), lambda i: (i, 0))
    wspec = pl.BlockSpec((None, WIDTH, d), lambda i: (layer, 0, 0))
    return pl.pallas_call(
        functools.partial(_merge_kernel, d=d),
        out_shape=jax.ShapeDtypeStruct((n_tok, d), F32),
        grid=(n_tok // ts,),
        in_specs=[yspec, yspec, yspec, yspec,
                  pl.BlockSpec((ts, d), lambda i: (i, 0)),
                  pl.BlockSpec((None, 1, N_MOD * d), lambda i: (i // per_seq, 0, 0)),
                  pl.BlockSpec((1, d), lambda i: (0, 0)),
                  pl.BlockSpec((None, d, N_BRANCH * d), lambda i: (layer, 0, 0)),
                  wspec, wspec, wspec, wspec,
                  pl.BlockSpec((None, d, d), lambda i: (layer, 0, 0))],
        out_specs=pl.BlockSpec((ts, d), lambda i: (i, 0)),
        compiler_params=_params(1),
        name="merge_out",
    )(*ys, x2, mod_l, norm_g.reshape(1, d), w_gates, *w_br, w_out)


def _mlp_kernel(x_ref, mod_ref, g_ref, w1_ref, w2_ref, fg_ref, o_ref, *, d, tf, final_norm):
    x = x_ref[...]
    hb = _norm_modulate(x, g_ref[...], mod_ref[:, 3 * d:4 * d], mod_ref[:, 4 * d:5 * d]).astype(BF16)
    acc = None
    for f in range(w1_ref.shape[1] // tf):
        a = jnp.maximum(_dot(hb, w1_ref[:, f * tf:(f + 1) * tf]), 0.0)
        part = _dot((a * a).astype(BF16), w2_ref[f * tf:(f + 1) * tf, :])
        acc = part if acc is None else acc + part
    out = x + mod_ref[:, 5 * d:6 * d] * acc
    if final_norm:
        out = out * lax.rsqrt(jnp.mean(out * out, axis=-1, keepdims=True) + EPS) * fg_ref[...]
    o_ref[...] = out


def _mlp(x2, mod_l, norm_g, w1, w2, final_g, layer, seq, final_norm):
    n_tok, d = x2.shape
    d_ff = w1.shape[-1]
    ts = min(512, seq)
    per_seq = seq // ts
    return pl.pallas_call(
        functools.partial(_mlp_kernel, d=d, tf=min(1024, d_ff), final_norm=final_norm),
        out_shape=jax.ShapeDtypeStruct((n_tok, d), F32),
        grid=(n_tok // ts,),
        in_specs=[pl.BlockSpec((ts, d), lambda i: (i, 0)),
                  pl.BlockSpec((None, 1, N_MOD * d), lambda i: (i // per_seq, 0, 0)),
                  pl.BlockSpec((1, d), lambda i: (0, 0)),
                  pl.BlockSpec((None, d, d_ff), lambda i: (layer, 0, 0)),
                  pl.BlockSpec((None, d_ff, d), lambda i: (layer, 0, 0)),
                  pl.BlockSpec((1, d), lambda i: (0, 0))],
        out_specs=pl.BlockSpec((ts, d), lambda i: (i, 0)),
        compiler_params=_params(1),
        name="mlp",
    )(x2, mod_l, norm_g.reshape(1, d), w1, w2, final_g.reshape(1, d))


def _split_w_in_kernel(wt_ref, wts_ref, main_ref, gates_ref, small_ref, *, n_main_blocks):
    block = wt_ref[0].T.astype(BF16)
    j = pl.program_id(1)

    @pl.when(j < n_main_blocks)
    def _():
        main_ref[...] = block

    @pl.when(j >= n_main_blocks)
    def _():
        gates_ref[...] = block

    @pl.when(j == 0)
    def _():
        lane = lax.broadcasted_iota(jnp.int32, small_ref.shape, 1)
        small_ref[...] = jnp.where(lane < 2 * N_HEADS, wts_ref[0].T, 0.0).astype(BF16)


def _split_w_in(w_in):
    n_layers, d, n_cols = w_in.shape
    n_main_blocks = N_DN_CB + N_REST_CB + N_RET_CB
    o_small = N_DN_CB * WIDTH
    o_rest = o_small + 2 * N_HEADS
    o_gates = o_rest + (N_REST_CB + N_RET_CB) * WIDTH
    n_gate_blocks = (n_cols - o_gates) // WIDTH
    w_t = jnp.swapaxes(w_in, 1, 2)

    def first_row(j):
        return jnp.where(j < N_DN_CB, j * WIDTH,
                         jnp.where(j < n_main_blocks, o_rest + (j - N_DN_CB) * WIDTH,
                                   o_gates + (j - n_main_blocks) * WIDTH))

    main, gates, small = pl.pallas_call(
        functools.partial(_split_w_in_kernel, n_main_blocks=n_main_blocks),
        out_shape=(jax.ShapeDtypeStruct((n_layers, d, n_main_blocks * WIDTH), BF16),
                   jax.ShapeDtypeStruct((n_layers, d, n_gate_blocks * WIDTH), BF16),
                   jax.ShapeDtypeStruct((n_layers, d, HEAD_DIM), BF16)),
        grid=(n_layers, n_main_blocks + n_gate_blocks),
        in_specs=[pl.BlockSpec((pl.Element(1), pl.Element(WIDTH), pl.Element(d)),
                               lambda l, j: (l, pl.multiple_of(first_row(j), 8), 0)),
                  pl.BlockSpec((pl.Element(1), pl.Element(HEAD_DIM), pl.Element(d)), lambda l, j: (l, o_small, 0))],
        out_specs=(pl.BlockSpec((None, d, WIDTH), lambda l, j: (l, 0, jnp.minimum(j, n_main_blocks - 1))),
                   pl.BlockSpec((None, d, WIDTH), lambda l, j: (l, 0, jnp.maximum(j - n_main_blocks, 0))),
                   pl.BlockSpec((None, d, HEAD_DIM), lambda l, j: (l, 0, 0))),
        compiler_params=_params(2),
        name="split_w_in",
    )(w_t, w_t)
    return main, small, gates


def kernel(x, c, positions, norm1_g, norm2_g, ada_w, ada_b, w_in, dn_conv_w, dn_a_log, dn_dt_bias, dn_norm_g, pool_w, pool_scale, sg_ln_g, sg_ln_b, sg_w, sg_b, ret_gn_g, w_br_dn, w_br_pool, w_br_sg, w_br_ret, w_out, mlp_w1, mlp_w2, final_g):
    bsz, seq, d = x.shape
    n_layers = w_in.shape[0]
    n_tok = bsz * seq
    mod = _modulation(c, ada_w, ada_b)
    cos2, sin2 = _rope_tables(positions)
    x2 = x.reshape(n_tok, d)
    w_main, w_small, w_gates = _split_w_in(w_in)
    w_br = [w.astype(BF16) for w in (w_br_dn, w_br_pool, w_br_sg, w_br_ret)]
    w_out_b, w1_b, w2_b = w_out.astype(BF16), mlp_w1.astype(BF16), mlp_w2.astype(BF16)
    for l in range(n_layers):
        mod_l = mod[l].reshape(bsz, 1, N_MOD * d)
        ys = _proj_mixers(x2.reshape(bsz, seq, d), mod_l, norm1_g[l], w_main, w_small, l,
                          dn_conv_w[l], dn_a_log[l], dn_dt_bias[l], dn_norm_g[l], cos2, sin2, ret_gn_g[l],
                          pool_w[l], pool_scale[l], sg_ln_g[l], sg_ln_b[l], sg_w[l], sg_b[l])
        ys = [y.reshape(n_tok, WIDTH) for y in ys]
        x2 = _merge(ys, x2, mod_l, norm1_g[l], w_gates, w_br, w_out_b, l, seq)
        x2 = _mlp(x2, mod_l, norm2_g[l], w1_b, w2_b, final_g, l, seq, final_norm=(l == n_layers - 1))
    return x2.reshape(bsz, seq, d)
```

```python
import functools
import math

import jax
import jax.numpy as jnp
from jax import lax
from jax.experimental import pallas as pl
from jax.experimental.pallas import tpu as pltpu

F32 = jnp.float32
BF16 = jnp.bfloat16

EPS = 1e-6
N_HEADS = 4
HEAD_DIM = 128
WIDTH = N_HEADS * HEAD_DIM
N_BRANCH = 4
N_MOD = 6
DN_CONV = 4
DN_CHUNK = 64
CHUNK = 128
POOL_WINDOWS = (2, 4, 8, 16)
POOL_HALO = 16
ROPE_BASE = 10000.0
VMEM_LIMIT = 48 * 1024 * 1024

N_DN_CB = 4
CB_PL, CB_SU, CB_SV = range(3)
N_REST_CB = 3
N_RET_CB = 4


def _sigmoid(x):
    return 1.0 / (1.0 + jnp.exp(-x))


def _silu(x):
    return x * _sigmoid(x)


def _softplus(x):
    return jnp.maximum(x, 0.0) + jnp.log1p(jnp.exp(-jnp.abs(x)))


def _gelu(x):
    return 0.5 * x * (1.0 + lax.erf(x * (1.0 / math.sqrt(2.0))))


def _dot(a, b):
    return jnp.dot(a, b, preferred_element_type=F32)


def _dot_nt(a, b):
    return lax.dot_general(a, b, (((1,), (1,)), ((), ())), preferred_element_type=F32)


def _dot_tn(a, b):
    return lax.dot_general(a, b, (((0,), (0,)), ((), ())), preferred_element_type=F32)


def _params(n_axes):
    return pltpu.CompilerParams(dimension_semantics=("arbitrary",) * n_axes,
                                vmem_limit_bytes=VMEM_LIMIT)


def _mod_kernel(c_ref, w_ref, b_ref, o_ref):
    cond = _silu(c_ref[...])
    o_ref[...] = _dot(cond.astype(BF16), w_ref[...].astype(BF16)) + b_ref[...]


def _modulation(c, ada_w, ada_b):
    n_layers, d, n_out = ada_w.shape
    bsz = c.shape[0]
    rows = 8
    c_pad = jnp.zeros((rows, d), F32).at[:bsz].set(c)
    tn = n_out // 4
    out = pl.pallas_call(
        _mod_kernel,
        out_shape=jax.ShapeDtypeStruct((n_layers, rows, n_out), F32),
        grid=(n_layers, n_out // tn),
        in_specs=[pl.BlockSpec((rows, d), lambda l, j: (0, 0)),
                  pl.BlockSpec((None, d, tn), lambda l, j: (l, 0, j)),
                  pl.BlockSpec((None, 1, tn), lambda l, j: (l, 0, j))],
        out_specs=pl.BlockSpec((None, rows, tn), lambda l, j: (l, 0, j)),
        compiler_params=_params(2),
        name="adaln_mod",
    )(c_pad, ada_w, ada_b.reshape(n_layers, 1, n_out))
    return out[:, :bsz]


def _rope_kernel(pos_ref, inv_ref, sgn_ref, cos_ref, sin_ref):
    ang = pos_ref[...].astype(F32) * inv_ref[...]
    cos_ref[...] = jnp.cos(ang)
    sin_ref[...] = jnp.sin(ang) * sgn_ref[...]


def _rope_tables(positions):
    bsz, seq = positions.shape
    half = HEAD_DIM // 2
    inv = ROPE_BASE ** (-jnp.arange(0, HEAD_DIM, 2, dtype=F32) / HEAD_DIM)
    inv2 = jnp.concatenate([inv, inv]).reshape(1, HEAD_DIM)
    sgn = jnp.concatenate([-jnp.ones((half,), F32), jnp.ones((half,), F32)]).reshape(1, HEAD_DIM)
    n_tok = bsz * seq
    ts = min(1024, n_tok)
    shp = jax.ShapeDtypeStruct((n_tok, HEAD_DIM), F32)
    cos2, sin2 = pl.pallas_call(
        _rope_kernel,
        out_shape=(shp, shp),
        grid=(n_tok // ts,),
        in_specs=[pl.BlockSpec((ts, 1), lambda i: (i, 0)),
                  pl.BlockSpec((1, HEAD_DIM), lambda i: (0, 0)),
                  pl.BlockSpec((1, HEAD_DIM), lambda i: (0, 0))],
        out_specs=(pl.BlockSpec((ts, HEAD_DIM), lambda i: (i, 0)),
                   pl.BlockSpec((ts, HEAD_DIM), lambda i: (i, 0))),
        compiler_params=_params(1),
        name="rope_tables",
    )(positions.reshape(n_tok, 1), inv2, sgn)
    return cos2.reshape(bsz, seq, HEAD_DIM), sin2.reshape(bsz, seq, HEAD_DIM)


def _norm_modulate(x, gain, shift, scale):
    y = x * lax.rsqrt(jnp.mean(x * x, axis=-1, keepdims=True) + EPS)
    return (y * gain) * (1.0 + scale) + shift


def _proj_mixers_kernel(x_ref, mod_ref, g_ref, w_ref, ws_ref, cw_ref, ab_ref, ng_ref, cos_ref, sin_ref, gn_ref,
                        pw_ref, psc_ref, lng_ref, lnb_ref, sgw_ref, sgb_ref, w1_ref, w2_ref, wgt_ref,
                        o_ref, ob_ref, oc_ref, od_ref, w1b_ref, w2b_ref, wgb_ref,
                        cbuf, s_ref, rs_ref, rc_ref, pbuf, *, bsz, d, n_sub):
    w1b_ref[...] = w1_ref[...].astype(BF16)
    w2b_ref[...] = w2_ref[...].astype(BF16)
    wgb_ref[...] = wgt_ref[0].T.astype(BF16)
    ts = n_sub * CHUNK
    cw = 3 * WIDTH
    nh = N_HEADS
    inst = [(b, c, h) for b in range(bsz) for c in range(n_sub) for h in range(nh)]
    seqs = [(b, h) for b in range(bsz) for h in range(nh)]
    log_gamma = [math.log1p(-2.0 ** (-5.0 - h)) for h in range(nh)]
    row = lax.broadcasted_iota(jnp.int32, (CHUNK, CHUNK), 0)
    col = lax.broadcasted_iota(jnp.int32, (CHUNK, CHUNK), 1)

    @pl.when(pl.program_id(0) == 0)
    def _():
        cbuf[:, 0:8, :] = jnp.zeros((bsz, 8, cw), F32)
        s_ref[...] = jnp.zeros_like(s_ref)
        rs_ref[...] = jnp.zeros_like(rs_ref)
        pbuf[:, 0:POOL_HALO, :] = jnp.zeros((bsz, POOL_HALO, WIDTH), F32)
        rel = (row - col).astype(F32)
        rowf = row.astype(F32)
        for h in range(nh):
            rc_ref[h] = jnp.where(row >= col, jnp.exp(log_gamma[h] * jnp.maximum(rel, 0.0)), 0.0)
            rc_ref[nh + h] = jnp.exp(log_gamma[h] * (float(CHUNK - 1) - rowf))
            rc_ref[2 * nh + h] = jnp.exp(log_gamma[h] * (rowf + 1.0))

    def lanes(h):
        return slice(h * HEAD_DIM, (h + 1) * HEAD_DIM)

    def sub(c):
        return slice(c * CHUNK, (c + 1) * CHUNK)

    def flat(b, c):
        return slice(b * ts + c * CHUNK, b * ts + (c + 1) * CHUNK)

    hb = jnp.concatenate(
        [_norm_modulate(x_ref[b], g_ref[...], mod_ref[b, :, 0:d], mod_ref[b, :, d:2 * d]).astype(BF16)
         for b in range(bsz)], axis=0)

    rest = {}

    def rest_project(j):
        if j not in rest:
            rest[j] = _dot(hb, w_ref[:, (N_DN_CB + j) * WIDTH:(N_DN_CB + j + 1) * WIDTH])

    def pool_stage():
        count = (pl.program_id(0) * ts + 1 + lax.broadcasted_iota(jnp.int32, (ts, HEAD_DIM), 0)).astype(F32)
        for b in range(bsz):
            p = rest[CB_PL][b * ts:(b + 1) * ts]
            pbuf[b, POOL_HALO:POOL_HALO + ts, :] = p
            for g, win in enumerate(POOL_WINDOWS):
                acc = p[:, lanes(g)]
                for dlt in range(1, win):
                    acc = acc + pbuf[b, POOL_HALO - dlt:POOL_HALO - dlt + ts, lanes(g)]
                pooled = acc / jnp.minimum(count, float(win)) - p[:, lanes(g)]
                y = _dot(pooled.astype(BF16), pw_ref[g]) * psc_ref[:, lanes(g)]
                ob_ref[b, :, lanes(g)] = y.astype(BF16)
            pbuf[b, 0:POOL_HALO, :] = pbuf[b, ts:ts + POOL_HALO, :]

    def sgu_stage():
        u = _gelu(rest[CB_SU])
        v = _gelu(rest[CB_SV])
        mu = jnp.mean(v, axis=-1, keepdims=True)
        vc = v - mu
        var = jnp.mean(vc * vc, axis=-1, keepdims=True)
        vb = (vc * lax.rsqrt(var + EPS) * lng_ref[...] + lnb_ref[...]).astype(BF16)
        for g in range(nh):
            wg = jnp.where(row >= col, sgw_ref[g], 0.0).astype(BF16)
            for b in range(bsz):
                for c in range(n_sub):
                    mixed = _dot(wg, vb[flat(b, c), lanes(g)]) + sgb_ref[:, lanes(g)]
                    oc_ref[b, sub(c), lanes(g)] = (u[flat(b, c), lanes(g)] * mixed).astype(BF16)

    ret = {}

    def ret_project():
        lo = (N_DN_CB + N_REST_CB) * WIDTH
        ret["in"] = [_dot(hb, w_ref[:, lo + j * WIDTH:lo + (j + 1) * WIDTH]) for j in range(N_RET_CB)]

    def ret_part(j, b, c, h):
        return ret["in"][j][flat(b, c), lanes(h)]

    def ret_scores():
        def rotary(t, b, c):
            return t * cos_ref[b, sub(c), :] + pltpu.roll(t, HEAD_DIM // 2, axis=1) * sin_ref[b, sub(c), :]

        ret["q"] = [rotary(ret_part(0, b, c, h), b, c).astype(BF16) for b, c, h in inst]
        ret["k"] = [rotary(ret_part(1, b, c, h), b, c) * HEAD_DIM ** -0.5 for b, c, h in inst]
        ret["v"] = [ret_part(2, b, c, h).astype(BF16) for b, c, h in inst]
        ret["scores"] = [(_dot_nt(qr, kr.astype(BF16)) * rc_ref[h]).astype(BF16)
                         for qr, kr, (b, c, h) in zip(ret["q"], ret["k"], inst)]

    def ret_state():
        state = [rs_ref[n] for n in range(len(seqs))]
        ret["o"] = {}
        for c in range(n_sub):
            for n, (b, h) in enumerate(seqs):
                i = inst.index((b, c, h))
                ret["o"][i] = (_dot(ret["scores"][i], ret["v"][i])
                               + _dot(ret["q"][i], state[n].astype(BF16)) * rc_ref[2 * nh + h])
                kv = _dot_tn((ret["k"][i] * rc_ref[nh + h]).astype(BF16), ret["v"][i])
                state[n] = state[n] * math.exp(log_gamma[h] * CHUNK) + kv
        for n, s in enumerate(state):
            rs_ref[n] = s

    def ret_store():
        for i, (b, c, h) in enumerate(inst):
            mu = jnp.mean(ret["o"][i], axis=-1, keepdims=True)
            oc = ret["o"][i] - mu
            var = jnp.mean(oc * oc, axis=-1, keepdims=True)
            y = oc * lax.rsqrt(var + EPS) * gn_ref[:, lanes(h)]
            od_ref[b, sub(c), lanes(h)] = (_silu(ret_part(3, b, c, h)) * y).astype(BF16)

    ret_stages = [ret_scores, ret_state, ret_store]

    sm_all = _dot(hb, ws_ref[...])
    qkvz = [_dot(hb, w_ref[:, j * WIDTH:(j + 1) * WIDTH]) for j in range(N_DN_CB)]

    def same_block(size):
        return (row // size) == (col // size)

    mask_incl = jnp.logical_and(same_block(DN_CHUNK), row >= col)
    mask_strict = jnp.logical_and(same_block(DN_CHUNK), row > col)
    eye = jnp.where(row == col, 1.0, 0.0).astype(F32)
    row_in_chunk = lax.broadcasted_iota(jnp.int32, (ts, HEAD_DIM), 0) % DN_CHUNK
    first_chunk = row < DN_CHUNK

    qkv, g_all, beta_all = [], [], []
    for b in range(bsz):
        rows = slice(b * ts, (b + 1) * ts)
        for part in range(3):
            cbuf[b, 8:8 + ts, part * WIDTH:(part + 1) * WIDTH] = qkvz[part][rows]
        acc = cbuf[b, 8:8 + ts, :] * cw_ref[DN_CONV - 1:DN_CONV, :]
        for kk in range(DN_CONV - 1):
            off = 8 - (DN_CONV - 1) + kk
            acc = acc + cbuf[b, off:off + ts, :] * cw_ref[kk:kk + 1, :]
        cbuf[b, 0:8, :] = cbuf[b, ts:ts + 8, :]
        qkv.append(_silu(acc))

        sm = sm_all[rows]
        beta_all.append(_sigmoid(sm))
        g = -jnp.exp(ab_ref[0:1, :]) * _softplus(sm + ab_ref[1:2, :])
        for s in (1, 2, 4, 8, 16, 32):
            g = g + jnp.where(row_in_chunk >= s, pltpu.roll(g, s, axis=0), 0.0)
        g_all.append(g)
        rest_project(min(b, N_REST_CB - 1))

    gb = [jnp.broadcast_to(g_all[b][sub(c), N_HEADS + h:N_HEADS + h + 1], (CHUNK, CHUNK))
          for b, c, h in inst]
    bb = [jnp.broadcast_to(beta_all[b][sub(c), h:h + 1], (CHUNK, CHUNK)) for b, c, h in inst]
    dec = [jnp.exp(jnp.where(mask_incl, g - g.T, -jnp.inf)) for g in gb]
    for j in range(N_REST_CB):
        rest_project(j)
    e_g = [jnp.exp(g) for g in gb]
    gl = [(g[DN_CHUNK - 1:DN_CHUNK, :], g[2 * DN_CHUNK - 1:2 * DN_CHUNK, :]) for g in gb]
    e_gl = [jnp.exp(jnp.where(first_chunk, l0, l1) - g) for g, (l0, l1) in zip(gb, gl)]

    def head(b, c, h, part):
        lo = part * WIDTH + h * HEAD_DIM
        return qkv[b][sub(c), lo:lo + HEAD_DIM]

    def l2n(t):
        return t * lax.rsqrt(jnp.sum(t * t, axis=-1, keepdims=True) + EPS)

    qn = [l2n(head(b, c, h, 0)) * HEAD_DIM ** -0.5 for b, c, h in inst]
    kn = [l2n(head(b, c, h, 1)) for b, c, h in inst]
    kb = [k * bt for k, bt in zip(kn, bb)]
    knb = [k.astype(BF16) for k in kn]
    lm = [jnp.where(mask_strict, _dot_nt(a.astype(BF16), k) * dc, 0.0) for a, k, dc in zip(kb, knb, dec)]
    qk = [(_dot_nt(q.astype(BF16), k) * dc).astype(BF16) for q, k, dc in zip(qn, knb, dec)]

    l8 = [jnp.where(same_block(8), m, 0.0) for m in lm]
    l8b = [m.astype(BF16) for m in l8]
    p1b = [_dot(m, m).astype(BF16) for m in l8b]
    tm = [eye - m for m in l8]
    tm = [t + _dot(t.astype(BF16), p) for t, p in zip(tm, p1b)]
    p2b = [_dot(p, p).astype(BF16) for p in p1b]
    tm = [t + _dot(t.astype(BF16), p) for t, p in zip(tm, p2b)]
    ret_project()
    for size, ret_stage in zip((8, 16, 32), ret_stages):
        lower_left = jnp.logical_and(same_block(2 * size), jnp.logical_not(same_block(size)))
        cm = [jnp.where(lower_left, m, 0.0).astype(BF16) for m in lm]
        tb = [t.astype(BF16) for t in tm]
        xm = [_dot(c_, t).astype(BF16) for c_, t in zip(cm, tb)]
        tm = [t - _dot(t16, x) for t, t16, x in zip(tm, tb, xm)]
        ret_stage()

    rhs = [jnp.concatenate([head(b, c, h, 2) * bt, k * e], axis=1).astype(BF16)
           for (b, c, h), bt, k, e in zip(inst, bb, kb, e_g)]
    uw = [_dot(t.astype(BF16), r) for t, r in zip(tm, rhs)]
    pool_stage()
    u = [m[:, 0:HEAD_DIM] for m in uw]
    w = [m[:, HEAD_DIM:2 * HEAD_DIM].astype(BF16) for m in uw]
    qs = [(q * e).astype(BF16) for q, e in zip(qn, e_g)]
    ks = [(k * e).astype(BF16) for k, e in zip(kn, e_gl)]

    state = [s_ref[n] for n in range(len(seqs))]
    vns = [[None, None] for _ in inst]
    outs = [[None, None] for _ in inst]
    for c in range(n_sub):
        ids = [inst.index((b, c, h)) for b, h in seqs]
        for half in range(2):
            rows = slice(half * DN_CHUNK, (half + 1) * DN_CHUNK)
            sb = [s.astype(BF16) for s in state]
            for n, i in enumerate(ids):
                vns[i][half] = u[i][rows] - _dot(w[i][rows], sb[n])
            for n, i in enumerate(ids):
                outs[i][half] = _dot(qs[i][rows], sb[n])
            state = [s * jnp.exp(gl[i][half]) + _dot_tn(ks[i][rows], vns[i][half].astype(BF16))
                     for s, i in zip(state, ids)]
    for n, s in enumerate(state):
        s_ref[n] = s
    sgu_stage()

    for i, (b, c, h) in enumerate(inst):
        vn = jnp.concatenate(vns[i], axis=0).astype(BF16)
        o = jnp.concatenate(outs[i], axis=0) + _dot(qk[i], vn)
        y = o * lax.rsqrt(jnp.mean(o * o, axis=-1, keepdims=True) + EPS) * ng_ref[...]
        y = y * _silu(qkvz[3][flat(b, c), lanes(h)])
        o_ref[b, sub(c), lanes(h)] = y.astype(BF16)


def _proj_mixers(x3, mod_l, norm_g, w_main, w_small, layer, conv_w, a_log, dt_bias, dn_norm_g, cos2, sin2, gn_g,
                 pool_w, pool_scale, ln_g, ln_b, sg_w, sg_b, mlp_w1, mlp_w2, w_in_t):
    bsz, seq, d = x3.shape
    n_sub = 2 if seq % (2 * CHUNK) == 0 else 1
    ts = n_sub * CHUNK
    n_steps = seq // ts
    d_ff = mlp_w1.shape[-1]
    assert d % (16 * n_steps) == 0 and d_ff % (16 * n_steps) == 0, (d, d_ff, n_steps)
    n_gates = N_BRANCH * d
    o_gates = w_in_t.shape[1] - n_gates
    gate_slab = n_gates // n_steps
    assert gate_slab % HEAD_DIM == 0 and o_gates % 8 == 0, (gate_slab, o_gates)
    n_in = (N_DN_CB + N_REST_CB + N_RET_CB) * WIDTH
    tab = pl.BlockSpec((bsz, ts, HEAD_DIM), lambda s: (0, s, 0))
    branch = jax.ShapeDtypeStruct((bsz, seq, WIDTH), BF16)
    state = pltpu.VMEM((bsz * N_HEADS, HEAD_DIM, HEAD_DIM), F32)
    ab = jnp.zeros((2, HEAD_DIM), F32)
    ab = ab.at[0, N_HEADS:2 * N_HEADS].set(a_log).at[1, N_HEADS:2 * N_HEADS].set(dt_bias)
    sg_bias = jnp.repeat(sg_b.T, HEAD_DIM, axis=1)
    row_vec = pl.BlockSpec((1, WIDTH), lambda s: (0, 0))
    group_mat = pl.BlockSpec((N_HEADS, CHUNK, CHUNK), lambda s: (0, 0, 0))
    branch_spec = pl.BlockSpec((bsz, ts, WIDTH), lambda s: (0, s, 0))
    return pl.pallas_call(
        functools.partial(_proj_mixers_kernel, bsz=bsz, d=d, n_sub=n_sub),
        out_shape=(branch, branch, branch, branch,
                   jax.ShapeDtypeStruct((d, d_ff), BF16), jax.ShapeDtypeStruct((d_ff, d), BF16),
                   jax.ShapeDtypeStruct((d, n_gates), BF16)),
        grid=(n_steps,),
        in_specs=[pl.BlockSpec((bsz, ts, d), lambda s: (0, s, 0)),
                  pl.BlockSpec((bsz, 1, N_MOD * d), lambda s: (0, 0, 0)),
                  pl.BlockSpec((1, d), lambda s: (0, 0)),
                  pl.BlockSpec((None, d, n_in), lambda s: (layer, 0, 0)),
                  pl.BlockSpec((None, d, HEAD_DIM), lambda s: (layer, 0, 0)),
                  pl.BlockSpec((DN_CONV, 3 * WIDTH), lambda s: (0, 0)),
                  pl.BlockSpec((2, HEAD_DIM), lambda s: (0, 0)),
                  pl.BlockSpec((1, HEAD_DIM), lambda s: (0, 0)),
                  tab, tab, row_vec,
                  group_mat, row_vec, row_vec, row_vec, group_mat,
                  pl.BlockSpec((CHUNK, WIDTH), lambda s: (0, 0)),
                  pl.BlockSpec((None, d // n_steps, d_ff), lambda s: (layer, s, 0)),
                  pl.BlockSpec((None, d_ff // n_steps, d), lambda s: (layer, s, 0)),
                  pl.BlockSpec((pl.Element(1), pl.Element(gate_slab), pl.Element(d)),
                               lambda s: (layer, pl.multiple_of(o_gates + s * gate_slab, 8), 0))],
        out_specs=(branch_spec, branch_spec, branch_spec, branch_spec,
                   pl.BlockSpec((d // n_steps, d_ff), lambda s: (s, 0)),
                   pl.BlockSpec((d_ff // n_steps, d), lambda s: (s, 0)),
                   pl.BlockSpec((d, gate_slab), lambda s: (0, s))),
        scratch_shapes=[pltpu.VMEM((bsz, ts + 8, 3 * WIDTH), F32), state, state,
                        pltpu.VMEM((3 * N_HEADS, CHUNK, CHUNK), F32),
                        pltpu.VMEM((bsz, ts + POOL_HALO, WIDTH), F32)],
        compiler_params=_params(1),
        name="proj_mixers",
    )(x3, mod_l, norm_g.reshape(1, d), w_main, w_small, conv_w, ab, dn_norm_g.reshape(1, HEAD_DIM),
      cos2, sin2, gn_g.reshape(1, WIDTH),
      pool_w.astype(BF16), pool_scale.reshape(1, WIDTH), ln_g.reshape(1, WIDTH), ln_b.reshape(1, WIDTH), sg_w, sg_bias,
      mlp_w1, mlp_w2, w_in_t)


def _merge_kernel(ya_ref, yb_ref, yc_ref, yd_ref, x_ref, mod_ref, g_ref, wg_ref, wa_ref, wb_ref, wc_ref, wd_ref,
                  wo_ref, o_ref, *, d):
    x = x_ref[...]
    hb = _norm_modulate(x, g_ref[...], mod_ref[:, 0:d], mod_ref[:, d:2 * d]).astype(BF16)
    merged = None
    branches = ((ya_ref, wa_ref), (yb_ref, wb_ref), (yc_ref, wc_ref), (yd_ref, wd_ref))
    for i, (y_ref, wbr_ref) in enumerate(branches):
        gate = _sigmoid(_dot(hb, wg_ref[:, i * d:(i + 1) * d]))
        term = gate * _dot(y_ref[...], wbr_ref[...])
        merged = term if merged is None else merged + term
    r = _dot(merged.astype(BF16), wo_ref[...])
    o_ref[...] = x + mod_ref[:, 2 * d:3 * d] * r


def _merge(ys, x2, mod_l, norm_g, w_gates, w_br, w_out, layer, seq):
    n_tok, d = x2.shape
    ts = min(512, seq)
    per_seq = seq // ts
    yspec = pl.BlockSpec((ts, WIDTH), lambda i: (i, 0))
    wspec = pl.BlockSpec((None, WIDTH, d), lambda i: (layer, 0, 0))
    return pl.pallas_call(
        functools.partial(_merge_kernel, d=d),
        out_shape=jax.ShapeDtypeStruct((n_tok, d), F32),
        grid=(n_tok // ts,),
        in_specs=[yspec, yspec, yspec, yspec,
                  pl.BlockSpec((ts, d), lambda i: (i, 0)),
                  pl.BlockSpec((None, 1, N_MOD * d), lambda i: (i // per_seq, 0, 0)),
                  pl.BlockSpec((1, d), lambda i: (0, 0)),
                  pl.BlockSpec((d, N_BRANCH * d), lambda i: (0, 0)),
                  wspec, wspec, wspec, wspec,
                  pl.BlockSpec((None, d, d), lambda i: (layer, 0, 0))],
        out_specs=pl.BlockSpec((ts, d), lambda i: (i, 0)),
        compiler_params=_params(1),
        name="merge_out",
    )(*ys, x2, mod_l, norm_g.reshape(1, d), w_gates, *w_br, w_out)


def _mlp_kernel(x_ref, mod_ref, g_ref, w1_ref, w2_ref, fg_ref, o_ref, *, d, tf, final_norm):
    x = x_ref[...]
    hb = _norm_modulate(x, g_ref[...], mod_ref[:, 3 * d:4 * d], mod_ref[:, 4 * d:5 * d]).astype(BF16)
    acc = None
    for f in range(w1_ref.shape[1] // tf):
        a = jnp.maximum(_dot(hb, w1_ref[:, f * tf:(f + 1) * tf]), 0.0)
        part = _dot((a * a).astype(BF16), w2_ref[f * tf:(f + 1) * tf, :])
        acc = part if acc is None else acc + part
    out = x + mod_ref[:, 5 * d:6 * d] * acc
    if final_norm:
        out = out * lax.rsqrt(jnp.mean(out * out, axis=-1, keepdims=True) + EPS) * fg_ref[...]
    o_ref[...] = out


def _mlp(x2, mod_l, norm_g, w1, w2, final_g, seq, final_norm):
    n_tok, d = x2.shape
    d_ff = w1.shape[-1]
    ts = min(512, seq)
    per_seq = seq // ts
    return pl.pallas_call(
        functools.partial(_mlp_kernel, d=d, tf=min(1024, d_ff), final_norm=final_norm),
        out_shape=jax.ShapeDtypeStruct((n_tok, d), F32),
        grid=(n_tok // ts,),
        in_specs=[pl.BlockSpec((ts, d), lambda i: (i, 0)),
                  pl.BlockSpec((None, 1, N_MOD * d), lambda i: (i // per_seq, 0, 0)),
                  pl.BlockSpec((1, d), lambda i: (0, 0)),
                  pl.BlockSpec((d, d_ff), lambda i: (0, 0)),
                  pl.BlockSpec((d_ff, d), lambda i: (0, 0)),
                  pl.BlockSpec((1, d), lambda i: (0, 0))],
        out_specs=pl.BlockSpec((ts, d), lambda i: (i, 0)),
        compiler_params=_params(1),
        name="mlp",
    )(x2, mod_l, norm_g.reshape(1, d), w1, w2, final_g.reshape(1, d))


def _split_w_in_kernel(wt_ref, wts_ref, main_ref, small_ref):
    main_ref[...] = wt_ref[0].T.astype(BF16)

    @pl.when(pl.program_id(1) == 0)
    def _():
        lane = lax.broadcasted_iota(jnp.int32, small_ref.shape, 1)
        small_ref[...] = jnp.where(lane < 2 * N_HEADS, wts_ref[0].T, 0.0).astype(BF16)


def _split_w_in(w_t):
    n_layers, _, d = w_t.shape
    n_main_blocks = N_DN_CB + N_REST_CB + N_RET_CB
    o_small = N_DN_CB * WIDTH
    o_rest = o_small + 2 * N_HEADS

    def first_row(j):
        return jnp.where(j < N_DN_CB, j * WIDTH, o_rest + (j - N_DN_CB) * WIDTH)

    main, small = pl.pallas_call(
        _split_w_in_kernel,
        out_shape=(jax.ShapeDtypeStruct((n_layers, d, n_main_blocks * WIDTH), BF16),
                   jax.ShapeDtypeStruct((n_layers, d, HEAD_DIM), BF16)),
        grid=(n_layers, n_main_blocks),
        in_specs=[pl.BlockSpec((pl.Element(1), pl.Element(WIDTH), pl.Element(d)),
                               lambda l, j: (l, pl.multiple_of(first_row(j), 8), 0)),
                  pl.BlockSpec((pl.Element(1), pl.Element(HEAD_DIM), pl.Element(d)), lambda l, j: (l, o_small, 0))],
        out_specs=(pl.BlockSpec((None, d, WIDTH), lambda l, j: (l, 0, j)),
                   pl.BlockSpec((None, d, HEAD_DIM), lambda l, j: (l, 0, 0))),
        compiler_params=_params(2),
        name="split_w_in",
    )(w_t, w_t)
    return main, small


def kernel(x, c, positions, norm1_g, norm2_g, ada_w, ada_b, w_in, dn_conv_w, dn_a_log, dn_dt_bias, dn_norm_g, pool_w, pool_scale, sg_ln_g, sg_ln_b, sg_w, sg_b, ret_gn_g, w_br_dn, w_br_pool, w_br_sg, w_br_ret, w_out, mlp_w1, mlp_w2, final_g):
    bsz, seq, d = x.shape
    n_layers = w_in.shape[0]
    n_tok = bsz * seq
    mod = _modulation(c, ada_w, ada_b)
    cos2, sin2 = _rope_tables(positions)
    x2 = x.reshape(n_tok, d)
    w_in_t = jnp.swapaxes(w_in, 1, 2)
    w_main, w_small = _split_w_in(w_in_t)
    w_br = [w.astype(BF16) for w in (w_br_dn, w_br_pool, w_br_sg, w_br_ret)]
    w_out_b = w_out.astype(BF16)
    for l in range(n_layers):
        mod_l = mod[l].reshape(bsz, 1, N_MOD * d)
        *ys, w1_b, w2_b, w_gates = _proj_mixers(
            x2.reshape(bsz, seq, d), mod_l, norm1_g[l], w_main, w_small, l,
            dn_conv_w[l], dn_a_log[l], dn_dt_bias[l], dn_norm_g[l], cos2, sin2, ret_gn_g[l],
            pool_w[l], pool_scale[l], sg_ln_g[l], sg_ln_b[l], sg_w[l], sg_b[l], mlp_w1, mlp_w2, w_in_t)
        ys = [y.reshape(n_tok, WIDTH) for y in ys]
        x2 = _merge(ys, x2, mod_l, norm1_g[l], w_gates, w_br, w_out_b, l, seq)
        x2 = _mlp(x2, mod_l, norm2_g[l], w1_b, w2_b, final_g, seq, final_norm=(l == n_layers - 1))
    return x2.reshape(bsz, seq, d)
```

```python
import functools
import math

import jax
import jax.numpy as jnp
from jax import lax
from jax.experimental import pallas as pl
from jax.experimental.pallas import tpu as pltpu

F32 = jnp.float32
BF16 = jnp.bfloat16

EPS = 1e-6
N_HEADS = 4
HEAD_DIM = 128
WIDTH = N_HEADS * HEAD_DIM
N_BRANCH = 4
N_MOD = 6
DN_CONV = 4
DN_CHUNK = 64
CONV_HALO = 8
CHUNK = 128
POOL_WINDOWS = (2, 4, 8, 16)
POOL_HALO = 16
ROPE_BASE = 10000.0
VMEM_LIMIT = 48 * 1024 * 1024

N_DN_CB = 4
CB_PL, CB_SU, CB_SV = range(3)
N_REST_CB = 3
N_RET_CB = 4


def _sigmoid(x):
    return 1.0 / (1.0 + jnp.exp(-x))


def _silu(x):
    return x * _sigmoid(x)


def _softplus(x):
    return jnp.maximum(x, 0.0) + jnp.log1p(jnp.exp(-jnp.abs(x)))


def _gelu(x):
    return 0.5 * x * (1.0 + lax.erf(x * (1.0 / math.sqrt(2.0))))


def _dot(a, b):
    return jnp.dot(a, b, preferred_element_type=F32)


def _dot_nt(a, b):
    return lax.dot_general(a, b, (((1,), (1,)), ((), ())), preferred_element_type=F32)


def _dot_tn(a, b):
    return lax.dot_general(a, b, (((0,), (0,)), ((), ())), preferred_element_type=F32)


def _params(n_axes):
    return pltpu.CompilerParams(dimension_semantics=("arbitrary",) * n_axes,
                                vmem_limit_bytes=VMEM_LIMIT)


def _mod_kernel(c_ref, w_ref, b_ref, o_ref):
    cond = _silu(c_ref[...])
    o_ref[...] = _dot(cond.astype(BF16), w_ref[...].astype(BF16)) + b_ref[...]


def _modulation(c, ada_w, ada_b):
    n_layers, d, n_out = ada_w.shape
    bsz = c.shape[0]
    rows = 8
    c_pad = jnp.zeros((rows, d), F32).at[:bsz].set(c)
    tn = n_out // 4
    out = pl.pallas_call(
        _mod_kernel,
        out_shape=jax.ShapeDtypeStruct((n_layers, rows, n_out), F32),
        grid=(n_layers, n_out // tn),
        in_specs=[pl.BlockSpec((rows, d), lambda l, j: (0, 0)),
                  pl.BlockSpec((None, d, tn), lambda l, j: (l, 0, j)),
                  pl.BlockSpec((None, 1, tn), lambda l, j: (l, 0, j))],
        out_specs=pl.BlockSpec((None, rows, tn), lambda l, j: (l, 0, j)),
        compiler_params=_params(2),
        name="adaln_mod",
    )(c_pad, ada_w, ada_b.reshape(n_layers, 1, n_out))
    return out[:, :bsz]


def _rope_kernel(pos_ref, inv_ref, cos_ref, sin_ref, *, bsz):
    half = HEAD_DIM // 2
    ts = pos_ref.shape[0]
    low = lax.broadcasted_iota(jnp.int32, (ts, HEAD_DIM), 1) < half
    pos = pos_ref[...].astype(F32)
    for b0 in range(0, bsz, 2):
        b1 = min(b0 + 1, bsz - 1)
        ang = jnp.where(low, pos[:, b0:b0 + 1], pos[:, b1:b1 + 1]) * inv_ref[...]
        c = jnp.cos(ang)
        sn = jnp.sin(ang)
        c_swapped = pltpu.roll(c, half, axis=1)
        s_swapped = pltpu.roll(sn, half, axis=1)
        cos_ref[b0] = jnp.where(low, c, c_swapped)
        sin_ref[b0] = jnp.where(low, -sn, s_swapped)
        if b1 != b0:
            cos_ref[b1] = jnp.where(low, c_swapped, c)
            sin_ref[b1] = jnp.where(low, -s_swapped, sn)


def _rope_tables(positions):
    bsz, seq = positions.shape
    inv = ROPE_BASE ** (-jnp.arange(0, HEAD_DIM, 2, dtype=F32) / HEAD_DIM)
    inv2 = jnp.concatenate([inv, inv]).reshape(1, HEAD_DIM)
    ts = min(1024, seq)
    shp = jax.ShapeDtypeStruct((bsz, seq, HEAD_DIM), F32)
    return pl.pallas_call(
        functools.partial(_rope_kernel, bsz=bsz),
        out_shape=(shp, shp),
        grid=(seq // ts,),
        in_specs=[pl.BlockSpec((ts, bsz), lambda i: (i, 0)),
                  pl.BlockSpec((1, HEAD_DIM), lambda i: (0, 0))],
        out_specs=(pl.BlockSpec((bsz, ts, HEAD_DIM), lambda i: (0, i, 0)),
                   pl.BlockSpec((bsz, ts, HEAD_DIM), lambda i: (0, i, 0))),
        compiler_params=_params(1),
        name="rope_tables",
    )(positions.T, inv2)


def _norm_modulate(x, gain, shift, scale):
    y = x * lax.rsqrt(jnp.mean(x * x, axis=-1, keepdims=True) + EPS)
    return (y * gain) * (1.0 + scale) + shift


def _proj_mixers_kernel(x_ref, mod_ref, g_ref, w_ref, ws_ref, cw_ref, ab_ref, ng_ref, cos_ref, sin_ref, gn_ref,
                        pw_ref, psc_ref, lng_ref, lnb_ref, sgw_ref, sgb_ref, w1_ref, w2_ref, wgt_ref,
                        o_ref, ob_ref, oc_ref, od_ref, w1b_ref, w2b_ref, wgb_ref,
                        cbuf, s_ref, rs_ref, rc_ref, pbuf, *, bsz, d, n_sub):
    w1b_ref[...] = w1_ref[...].astype(BF16)
    w2b_ref[...] = w2_ref[...].astype(BF16)
    wgb_ref[...] = wgt_ref[0].T.astype(BF16)
    ts = n_sub * CHUNK
    cw = 3 * WIDTH
    nh = N_HEADS
    inst = [(b, c, h) for b in range(bsz) for c in range(n_sub) for h in range(nh)]
    seqs = [(b, h) for b in range(bsz) for h in range(nh)]
    log_gamma = [math.log1p(-2.0 ** (-5.0 - h)) for h in range(nh)]
    row = lax.broadcasted_iota(jnp.int32, (CHUNK, CHUNK), 0)
    col = lax.broadcasted_iota(jnp.int32, (CHUNK, CHUNK), 1)

    @pl.when(pl.program_id(0) == 0)
    def _():
        cbuf[:, 0:CONV_HALO, :] = jnp.zeros((bsz, CONV_HALO, cw), F32)
        s_ref[...] = jnp.zeros_like(s_ref)
        rs_ref[...] = jnp.zeros_like(rs_ref)
        pbuf[:, 0:POOL_HALO, :] = jnp.zeros((bsz, POOL_HALO, WIDTH), F32)
        rel = (row - col).astype(F32)
        rowf = row.astype(F32)
        for h in range(nh):
            rc_ref[h] = jnp.where(row >= col, jnp.exp(log_gamma[h] * jnp.maximum(rel, 0.0)), 0.0)
            rc_ref[nh + h] = jnp.exp(log_gamma[h] * (float(CHUNK - 1) - rowf))
            rc_ref[2 * nh + h] = jnp.exp(log_gamma[h] * (rowf + 1.0))

    def lanes(h):
        return slice(h * HEAD_DIM, (h + 1) * HEAD_DIM)

    def sub(c):
        return slice(c * CHUNK, (c + 1) * CHUNK)

    def flat(b, c):
        return slice(b * ts + c * CHUNK, b * ts + (c + 1) * CHUNK)

    hb = jnp.concatenate(
        [_norm_modulate(x_ref[b], g_ref[...], mod_ref[b, :, 0:d], mod_ref[b, :, d:2 * d]).astype(BF16)
         for b in range(bsz)], axis=0)

    rest = {}

    def rest_project(j):
        if j not in rest:
            rest[j] = _dot(hb, w_ref[:, (N_DN_CB + j) * WIDTH:(N_DN_CB + j + 1) * WIDTH])

    def pool_stage():
        count = (pl.program_id(0) * ts + 1 + lax.broadcasted_iota(jnp.int32, (ts, HEAD_DIM), 0)).astype(F32)
        for b in range(bsz):
            p = rest[CB_PL][b * ts:(b + 1) * ts]
            pbuf[b, POOL_HALO:POOL_HALO + ts, :] = p
            for g, win in enumerate(POOL_WINDOWS):
                acc = p[:, lanes(g)]
                for dlt in range(1, win):
                    acc = acc + pbuf[b, POOL_HALO - dlt:POOL_HALO - dlt + ts, lanes(g)]
                pooled = acc / jnp.minimum(count, float(win)) - p[:, lanes(g)]
                y = _dot(pooled.astype(BF16), pw_ref[g]) * psc_ref[:, lanes(g)]
                ob_ref[b, :, lanes(g)] = y.astype(BF16)
            pbuf[b, 0:POOL_HALO, :] = pbuf[b, ts:ts + POOL_HALO, :]

    def sgu_stage():
        u = _gelu(rest[CB_SU])
        v = _gelu(rest[CB_SV])
        mu = jnp.mean(v, axis=-1, keepdims=True)
        vc = v - mu
        var = jnp.mean(vc * vc, axis=-1, keepdims=True)
        vb = (vc * lax.rsqrt(var + EPS) * lng_ref[...] + lnb_ref[...]).astype(BF16)
        for g in range(nh):
            wg = jnp.where(row >= col, sgw_ref[g], 0.0).astype(BF16)
            for b in range(bsz):
                for c in range(n_sub):
                    mixed = _dot(wg, vb[flat(b, c), lanes(g)]) + sgb_ref[:, lanes(g)]
                    oc_ref[b, sub(c), lanes(g)] = (u[flat(b, c), lanes(g)] * mixed).astype(BF16)

    ret = {}

    def ret_project():
        lo = (N_DN_CB + N_REST_CB) * WIDTH
        ret["in"] = [_dot(hb, w_ref[:, lo + j * WIDTH:lo + (j + 1) * WIDTH]) for j in range(N_RET_CB)]

    def ret_part(j, b, c, h):
        return ret["in"][j][flat(b, c), lanes(h)]

    def ret_scores():
        def rotary(t, b, c):
            return t * cos_ref[b, sub(c), :] + pltpu.roll(t, HEAD_DIM // 2, axis=1) * sin_ref[b, sub(c), :]

        ret["q"] = [rotary(ret_part(0, b, c, h), b, c).astype(BF16) for b, c, h in inst]
        ret["k"] = [rotary(ret_part(1, b, c, h), b, c) * HEAD_DIM ** -0.5 for b, c, h in inst]
        ret["v"] = [ret_part(2, b, c, h).astype(BF16) for b, c, h in inst]
        ret["scores"] = [(_dot_nt(qr, kr.astype(BF16)) * rc_ref[h]).astype(BF16)
                         for qr, kr, (b, c, h) in zip(ret["q"], ret["k"], inst)]

    def ret_state():
        state = [rs_ref[n] for n in range(len(seqs))]
        ret["o"] = {}
        for c in range(n_sub):
            for n, (b, h) in enumerate(seqs):
                i = inst.index((b, c, h))
                ret["o"][i] = (_dot(ret["scores"][i], ret["v"][i])
                               + _dot(ret["q"][i], state[n].astype(BF16)) * rc_ref[2 * nh + h])
                kv = _dot_tn((ret["k"][i] * rc_ref[nh + h]).astype(BF16), ret["v"][i])
                state[n] = state[n] * math.exp(log_gamma[h] * CHUNK) + kv
        for n, s in enumerate(state):
            rs_ref[n] = s

    def ret_store():
        for i, (b, c, h) in enumerate(inst):
            mu = jnp.mean(ret["o"][i], axis=-1, keepdims=True)
            oc = ret["o"][i] - mu
            var = jnp.mean(oc * oc, axis=-1, keepdims=True)
            y = oc * lax.rsqrt(var + EPS) * gn_ref[:, lanes(h)]
            od_ref[b, sub(c), lanes(h)] = (_silu(ret_part(3, b, c, h)) * y).astype(BF16)

    ret_stages = [ret_scores, ret_state, ret_store]

    sm_all = _dot(hb, ws_ref[...])
    qkvz = [_dot(hb, w_ref[:, j * WIDTH:(j + 1) * WIDTH]) for j in range(N_DN_CB)]

    def same_block(size):
        return (row // size) == (col // size)

    mask_incl = jnp.logical_and(same_block(DN_CHUNK), row >= col)
    mask_strict = jnp.logical_and(same_block(DN_CHUNK), row > col)
    eye = jnp.where(row == col, 1.0, 0.0).astype(F32)
    row_in_chunk = lax.broadcasted_iota(jnp.int32, (ts, HEAD_DIM), 0) % DN_CHUNK
    first_chunk = row < DN_CHUNK

    qkv, g_all, beta_all = [], [], []
    for b in range(bsz):
        rows = slice(b * ts, (b + 1) * ts)
        for part in range(3):
            cbuf[b, CONV_HALO:CONV_HALO + ts, part * WIDTH:(part + 1) * WIDTH] = qkvz[part][rows]
        acc = cbuf[b, CONV_HALO:CONV_HALO + ts, :] * cw_ref[DN_CONV - 1:DN_CONV, :]
        for kk in range(DN_CONV - 1):
            off = CONV_HALO - (DN_CONV - 1) + kk
            acc = acc + cbuf[b, off:off + ts, :] * cw_ref[kk:kk + 1, :]
        cbuf[b, 0:CONV_HALO, :] = cbuf[b, ts:ts + CONV_HALO, :]
        qkv.append(_silu(acc))

        sm = sm_all[rows]
        beta_all.append(_sigmoid(sm))
        g = -jnp.exp(ab_ref[0:1, :]) * _softplus(sm + ab_ref[1:2, :])
        for s in (1, 2, 4, 8, 16, 32):
            g = g + jnp.where(row_in_chunk >= s, pltpu.roll(g, s, axis=0), 0.0)
        g_all.append(g)
        rest_project(min(b, N_REST_CB - 1))

    gb = [jnp.broadcast_to(g_all[b][sub(c), N_HEADS + h:N_HEADS + h + 1], (CHUNK, CHUNK))
          for b, c, h in inst]
    bb = [jnp.broadcast_to(beta_all[b][sub(c), h:h + 1], (CHUNK, CHUNK)) for b, c, h in inst]
    dec = [jnp.exp(jnp.where(mask_incl, g - g.T, -jnp.inf)) for g in gb]
    for j in range(N_REST_CB):
        rest_project(j)
    e_g = [jnp.exp(g) for g in gb]
    gl = [(g[DN_CHUNK - 1:DN_CHUNK, :], g[2 * DN_CHUNK - 1:2 * DN_CHUNK, :]) for g in gb]
    e_gl = [jnp.exp(jnp.where(first_chunk, l0, l1) - g) for g, (l0, l1) in zip(gb, gl)]

    def head(b, c, h, part):
        lo = part * WIDTH + h * HEAD_DIM
        return qkv[b][sub(c), lo:lo + HEAD_DIM]

    def l2n(t):
        return t * lax.rsqrt(jnp.sum(t * t, axis=-1, keepdims=True) + EPS)

    qn = [l2n(head(b, c, h, 0)) * HEAD_DIM ** -0.5 for b, c, h in inst]
    kn = [l2n(head(b, c, h, 1)) for b, c, h in inst]
    kb = [k * bt for k, bt in zip(kn, bb)]
    knb = [k.astype(BF16) for k in kn]
    lm = [jnp.where(mask_strict, _dot_nt(a.astype(BF16), k) * dc, 0.0) for a, k, dc in zip(kb, knb, dec)]
    qk = [(_dot_nt(q.astype(BF16), k) * dc).astype(BF16) for q, k, dc in zip(qn, knb, dec)]

    l8 = [jnp.where(same_block(8), m, 0.0) for m in lm]
    l8b = [m.astype(BF16) for m in l8]
    p1b = [_dot(m, m).astype(BF16) for m in l8b]
    tm = [eye - m for m in l8]
    tm = [t + _dot(t.astype(BF16), p) for t, p in zip(tm, p1b)]
    p2b = [_dot(p, p).astype(BF16) for p in p1b]
    tm = [t + _dot(t.astype(BF16), p) for t, p in zip(tm, p2b)]
    ret_project()
    for size, ret_stage in zip((8, 16, 32), ret_stages):
        lower_left = jnp.logical_and(same_block(2 * size), jnp.logical_not(same_block(size)))
        cm = [jnp.where(lower_left, m, 0.0).astype(BF16) for m in lm]
        tb = [t.astype(BF16) for t in tm]
        xm = [_dot(c_, t).astype(BF16) for c_, t in zip(cm, tb)]
        tm = [t - _dot(t16, x) for t, t16, x in zip(tm, tb, xm)]
        ret_stage()

    rhs = [jnp.concatenate([head(b, c, h, 2) * bt, k * e], axis=1).astype(BF16)
           for (b, c, h), bt, k, e in zip(inst, bb, kb, e_g)]
    uw = [_dot(t.astype(BF16), r) for t, r in zip(tm, rhs)]
    pool_stage()
    u = [m[:, 0:HEAD_DIM] for m in uw]
    w = [m[:, HEAD_DIM:2 * HEAD_DIM].astype(BF16) for m in uw]
    qs = [(q * e).astype(BF16) for q, e in zip(qn, e_g)]
    ks = [(k * e).astype(BF16) for k, e in zip(kn, e_gl)]

    state = [s_ref[n] for n in range(len(seqs))]
    vns = [[None, None] for _ in inst]
    outs = [[None, None] for _ in inst]
    for c in range(n_sub):
        ids = [inst.index((b, c, h)) for b, h in seqs]
        for half in range(2):
            rows = slice(half * DN_CHUNK, (half + 1) * DN_CHUNK)
            sb = [s.astype(BF16) for s in state]
            for n, i in enumerate(ids):
                vns[i][half] = u[i][rows] - _dot(w[i][rows], sb[n])
            for n, i in enumerate(ids):
                outs[i][half] = _dot(qs[i][rows], sb[n])
            state = [s * jnp.exp(gl[i][half]) + _dot_tn(ks[i][rows], vns[i][half].astype(BF16))
                     for s, i in zip(state, ids)]
    for n, s in enumerate(state):
        s_ref[n] = s
    sgu_stage()

    for i, (b, c, h) in enumerate(inst):
        vn = jnp.concatenate(vns[i], axis=0).astype(BF16)
        o = jnp.concatenate(outs[i], axis=0) + _dot(qk[i], vn)
        y = o * lax.rsqrt(jnp.mean(o * o, axis=-1, keepdims=True) + EPS) * ng_ref[...]
        y = y * _silu(qkvz[3][flat(b, c), lanes(h)])
        o_ref[b, sub(c), lanes(h)] = y.astype(BF16)


def _proj_mixers(x3, mod_l, layer, p, cos2, sin2, w_in_t):
    bsz, seq, d = x3.shape
    n_sub = 2 if seq % (2 * CHUNK) == 0 else 1
    ts = n_sub * CHUNK
    n_steps = seq // ts
    d_ff = p["mlp_w1"].shape[-1]
    assert d % (16 * n_steps) == 0 and d_ff % (16 * n_steps) == 0, (d, d_ff, n_steps)
    n_gates = N_BRANCH * d
    o_gates = w_in_t.shape[1] - n_gates
    gate_slab = n_gates // n_steps
    assert gate_slab % HEAD_DIM == 0 and o_gates % 8 == 0, (gate_slab, o_gates)
    n_in = (N_DN_CB + N_REST_CB + N_RET_CB) * WIDTH
    tab = pl.BlockSpec((bsz, ts, HEAD_DIM), lambda s: (0, s, 0))
    branch = jax.ShapeDtypeStruct((bsz, seq, WIDTH), BF16)
    state = pltpu.VMEM((bsz * N_HEADS, HEAD_DIM, HEAD_DIM), F32)

    def layer_spec(*shape):
        return pl.BlockSpec((None,) + shape, lambda s: (layer,) + (0,) * len(shape))

    row_vec = layer_spec(1, WIDTH)
    group_mat = layer_spec(N_HEADS, CHUNK, CHUNK)
    branch_spec = pl.BlockSpec((bsz, ts, WIDTH), lambda s: (0, s, 0))
    return pl.pallas_call(
        functools.partial(_proj_mixers_kernel, bsz=bsz, d=d, n_sub=n_sub),
        out_shape=(branch, branch, branch, branch,
                   jax.ShapeDtypeStruct((d, d_ff), BF16), jax.ShapeDtypeStruct((d_ff, d), BF16),
                   jax.ShapeDtypeStruct((d, n_gates), BF16)),
        grid=(n_steps,),
        in_specs=[pl.BlockSpec((bsz, ts, d), lambda s: (0, s, 0)),
                  pl.BlockSpec((bsz, 1, N_MOD * d), lambda s: (0, 0, 0)),
                  layer_spec(1, d),
                  layer_spec(d, n_in),
                  layer_spec(d, HEAD_DIM),
                  layer_spec(DN_CONV, 3 * WIDTH),
                  layer_spec(2, HEAD_DIM),
                  layer_spec(1, HEAD_DIM),
                  tab, tab, row_vec,
                  group_mat, row_vec, row_vec, row_vec, group_mat,
                  layer_spec(CHUNK, WIDTH),
                  pl.BlockSpec((None, d // n_steps, d_ff), lambda s: (layer, s, 0)),
                  pl.BlockSpec((None, d_ff // n_steps, d), lambda s: (layer, s, 0)),
                  pl.BlockSpec((pl.Element(1), pl.Element(gate_slab), pl.Element(d)),
                               lambda s: (layer, pl.multiple_of(o_gates + s * gate_slab, 8), 0))],
        out_specs=(branch_spec, branch_spec, branch_spec, branch_spec,
                   pl.BlockSpec((d // n_steps, d_ff), lambda s: (s, 0)),
                   pl.BlockSpec((d_ff // n_steps, d), lambda s: (s, 0)),
                   pl.BlockSpec((d, gate_slab), lambda s: (0, s))),
        scratch_shapes=[pltpu.VMEM((bsz, ts + CONV_HALO, 3 * WIDTH), F32), state, state,
                        pltpu.VMEM((3 * N_HEADS, CHUNK, CHUNK), F32),
                        pltpu.VMEM((bsz, ts + POOL_HALO, WIDTH), F32)],
        compiler_params=_params(1),
        name="proj_mixers",
    )(x3, mod_l, p["norm1_g"], p["w_main"], p["w_small"], p["dn_conv_w"], p["dn_ab"], p["dn_norm_g"],
      cos2, sin2, p["ret_gn_g"],
      p["pool_w"], p["pool_scale"], p["sg_ln_g"], p["sg_ln_b"], p["sg_w"], p["sg_bias"],
      p["mlp_w1"], p["mlp_w2"], w_in_t)


def _merge_kernel(ya_ref, yb_ref, yc_ref, yd_ref, x_ref, mod_ref, g_ref, wg_ref, wa_ref, wb_ref, wc_ref, wd_ref,
                  wo_ref, o_ref, *, d):
    x = x_ref[...]
    hb = _norm_modulate(x, g_ref[...], mod_ref[:, 0:d], mod_ref[:, d:2 * d]).astype(BF16)
    merged = None
    branches = ((ya_ref, wa_ref), (yb_ref, wb_ref), (yc_ref, wc_ref), (yd_ref, wd_ref))
    for i, (y_ref, wbr_ref) in enumerate(branches):
        gate = _sigmoid(_dot(hb, wg_ref[:, i * d:(i + 1) * d]))
        term = gate * _dot(y_ref[...], wbr_ref[...])
        merged = term if merged is None else merged + term
    r = _dot(merged.astype(BF16), wo_ref[...])
    o_ref[...] = x + mod_ref[:, 2 * d:3 * d] * r


def _merge(ys, x2, mod_l, norm_g, w_gates, w_br, w_out, layer, seq):
    n_tok, d = x2.shape
    ts = min(512, seq)
    per_seq = seq // ts
    yspec = pl.BlockSpec((ts, WIDTH), lambda i: (i, 0))
    wspec = pl.BlockSpec((None, WIDTH, d), lambda i: (layer, 0, 0))
    return pl.pallas_call(
        functools.partial(_merge_kernel, d=d),
        out_shape=jax.ShapeDtypeStruct((n_tok, d), F32),
        grid=(n_tok // ts,),
        in_specs=[yspec, yspec, yspec, yspec,
                  pl.BlockSpec((ts, d), lambda i: (i, 0)),
                  pl.BlockSpec((None, 1, N_MOD * d), lambda i: (i // per_seq, 0, 0)),
                  pl.BlockSpec((None, 1, d), lambda i: (layer, 0, 0)),
                  pl.BlockSpec((d, N_BRANCH * d), lambda i: (0, 0)),
                  wspec, wspec, wspec, wspec,
                  pl.BlockSpec((None, d, d), lambda i: (layer, 0, 0))],
        out_specs=pl.BlockSpec((ts, d), lambda i: (i, 0)),
        compiler_params=_params(1),
        name="merge_out",
    )(*ys, x2, mod_l, norm_g, w_gates, *w_br, w_out)


def _mlp_kernel(x_ref, mod_ref, g_ref, w1_ref, w2_ref, fg_ref, o_ref, *, d, tf, final_norm):
    x = x_ref[...]
    hb = _norm_modulate(x, g_ref[...], mod_ref[:, 3 * d:4 * d], mod_ref[:, 4 * d:5 * d]).astype(BF16)
    acc = None
    for f in range(w1_ref.shape[1] // tf):
        a = jnp.maximum(_dot(hb, w1_ref[:, f * tf:(f + 1) * tf]), 0.0)
        part = _dot((a * a).astype(BF16), w2_ref[f * tf:(f + 1) * tf, :])
        acc = part if acc is None else acc + part
    out = x + mod_ref[:, 5 * d:6 * d] * acc
    if final_norm:
        out = out * lax.rsqrt(jnp.mean(out * out, axis=-1, keepdims=True) + EPS) * fg_ref[...]
    o_ref[...] = out


def _mlp(x2, mod_l, norm_g, w1, w2, final_g, layer, seq, final_norm):
    n_tok, d = x2.shape
    d_ff = w1.shape[-1]
    ts = min(512, seq)
    per_seq = seq // ts
    return pl.pallas_call(
        functools.partial(_mlp_kernel, d=d, tf=min(1024, d_ff), final_norm=final_norm),
        out_shape=jax.ShapeDtypeStruct((n_tok, d), F32),
        grid=(n_tok // ts,),
        in_specs=[pl.BlockSpec((ts, d), lambda i: (i, 0)),
                  pl.BlockSpec((None, 1, N_MOD * d), lambda i: (i // per_seq, 0, 0)),
                  pl.BlockSpec((None, 1, d), lambda i: (layer, 0, 0)),
                  pl.BlockSpec((d, d_ff), lambda i: (0, 0)),
                  pl.BlockSpec((d_ff, d), lambda i: (0, 0)),
                  pl.BlockSpec((1, d), lambda i: (0, 0))],
        out_specs=pl.BlockSpec((ts, d), lambda i: (i, 0)),
        compiler_params=_params(1),
        name="mlp",
    )(x2, mod_l, norm_g, w1, w2, final_g.reshape(1, d))


def _split_w_in_kernel(wt_ref, wts_ref, main_ref, small_ref):
    main_ref[...] = wt_ref[0].T.astype(BF16)

    @pl.when(pl.program_id(1) == 0)
    def _():
        lane = lax.broadcasted_iota(jnp.int32, small_ref.shape, 1)
        small_ref[...] = jnp.where(lane < 2 * N_HEADS, wts_ref[0].T, 0.0).astype(BF16)


def _split_w_in(w_t):
    n_layers, _, d = w_t.shape
    n_main_blocks = N_DN_CB + N_REST_CB + N_RET_CB
    o_small = N_DN_CB * WIDTH
    o_rest = o_small + 2 * N_HEADS

    def first_row(j):
        return jnp.where(j < N_DN_CB, j * WIDTH, o_rest + (j - N_DN_CB) * WIDTH)

    main, small = pl.pallas_call(
        _split_w_in_kernel,
        out_shape=(jax.ShapeDtypeStruct((n_layers, d, n_main_blocks * WIDTH), BF16),
                   jax.ShapeDtypeStruct((n_layers, d, HEAD_DIM), BF16)),
        grid=(n_layers, n_main_blocks),
        in_specs=[pl.BlockSpec((pl.Element(1), pl.Element(WIDTH), pl.Element(d)),
                               lambda l, j: (l, pl.multiple_of(first_row(j), 8), 0)),
                  pl.BlockSpec((pl.Element(1), pl.Element(HEAD_DIM), pl.Element(d)), lambda l, j: (l, o_small, 0))],
        out_specs=(pl.BlockSpec((None, d, WIDTH), lambda l, j: (l, 0, j)),
                   pl.BlockSpec((None, d, HEAD_DIM), lambda l, j: (l, 0, 0))),
        compiler_params=_params(2),
        name="split_w_in",
    )(w_t, w_t)
    return main, small


def kernel(x, c, positions, norm1_g, norm2_g, ada_w, ada_b, w_in, dn_conv_w, dn_a_log, dn_dt_bias, dn_norm_g, pool_w, pool_scale, sg_ln_g, sg_ln_b, sg_w, sg_b, ret_gn_g, w_br_dn, w_br_pool, w_br_sg, w_br_ret, w_out, mlp_w1, mlp_w2, final_g):
    bsz, seq, d = x.shape
    n_layers = w_in.shape[0]
    n_tok = bsz * seq
    mod = _modulation(c, ada_w, ada_b)
    cos2, sin2 = _rope_tables(positions)
    x2 = x.reshape(n_tok, d)
    w_in_t = jnp.swapaxes(w_in, 1, 2)
    w_main, w_small = _split_w_in(w_in_t)
    w_br = [w.astype(BF16) for w in (w_br_dn, w_br_pool, w_br_sg, w_br_ret)]
    w_out_b = w_out.astype(BF16)
    dn_ab = jnp.zeros((n_layers, 2, HEAD_DIM), F32)
    dn_ab = dn_ab.at[:, 0, N_HEADS:2 * N_HEADS].set(dn_a_log).at[:, 1, N_HEADS:2 * N_HEADS].set(dn_dt_bias)
    p = dict(norm1_g=norm1_g.reshape(n_layers, 1, d), w_main=w_main, w_small=w_small,
             dn_conv_w=dn_conv_w, dn_ab=dn_ab, dn_norm_g=dn_norm_g.reshape(n_layers, 1, HEAD_DIM),
             ret_gn_g=ret_gn_g.reshape(n_layers, 1, WIDTH),
             pool_w=pool_w.astype(BF16), pool_scale=pool_scale.reshape(n_layers, 1, WIDTH),
             sg_ln_g=sg_ln_g.reshape(n_layers, 1, WIDTH), sg_ln_b=sg_ln_b.reshape(n_layers, 1, WIDTH), sg_w=sg_w,
             sg_bias=jnp.repeat(jnp.swapaxes(sg_b, 1, 2), HEAD_DIM, axis=2),
             mlp_w1=mlp_w1, mlp_w2=mlp_w2)
    norm2 = norm2_g.reshape(n_layers, 1, d)
    for l in range(n_layers):
        mod_l = mod[l].reshape(bsz, 1, N_MOD * d)
        *ys, w1_b, w2_b, w_gates = _proj_mixers(x2.reshape(bsz, seq, d), mod_l, l, p, cos2, sin2, w_in_t)
        ys = [y.reshape(n_tok, WIDTH) for y in ys]
        x2 = _merge(ys, x2, mod_l, p["norm1_g"], w_gates, w_br, w_out_b, l, seq)
        x2 = _mlp(x2, mod_l, norm2, w1_b, w2_b, final_g, l, seq, final_norm=(l == n_layers - 1))
    return x2.reshape(bsz, seq, d)
```

```python
import functools
import math

import jax
import jax.numpy as jnp
from jax import lax
from jax.experimental import pallas as pl
from jax.experimental.pallas import tpu as pltpu

F32 = jnp.float32
BF16 = jnp.bfloat16

EPS = 1e-6
N_HEADS = 4
HEAD_DIM = 128
WIDTH = N_HEADS * HEAD_DIM
N_BRANCH = 4
N_MOD = 6
DN_CONV = 4
DN_CHUNK = 64
CONV_HALO = 8
CHUNK = 128
POOL_WINDOWS = (2, 4, 8, 16)
POOL_HALO = 16
assert all(w & (w - 1) == 0 and w - 1 <= POOL_HALO for w in POOL_WINDOWS)
ROPE_BASE = 10000.0
VMEM_LIMIT = 52 * 1024 * 1024

N_DN_CB = 4
CB_PL, CB_SU, CB_SV = range(3)
N_REST_CB = 3
N_RET_CB = 4


def _sigmoid(x):
    return 1.0 / (1.0 + jnp.exp(-x))


def _silu(x):
    return x * _sigmoid(x)


def _softplus(x):
    return jnp.maximum(x, 0.0) + jnp.log1p(jnp.exp(-jnp.abs(x)))


def _gelu(x):
    return 0.5 * x * (1.0 + lax.erf(x * (1.0 / math.sqrt(2.0))))


def _dot(a, b):
    return jnp.dot(a, b, preferred_element_type=F32)


def _dot_nt(a, b):
    return lax.dot_general(a, b, (((1,), (1,)), ((), ())), preferred_element_type=F32)


def _dot_tn(a, b):
    return lax.dot_general(a, b, (((0,), (0,)), ((), ())), preferred_element_type=F32)


def _params(n_axes):
    return pltpu.CompilerParams(dimension_semantics=("arbitrary",) * n_axes,
                                vmem_limit_bytes=VMEM_LIMIT)


def _mod_kernel(c_ref, w_ref, b_ref, o_ref):
    cond = _silu(c_ref[...])
    o_ref[...] = _dot(cond.astype(BF16), w_ref[...].astype(BF16)) + b_ref[...]


def _modulation(c, ada_w, ada_b):
    n_layers, d, n_out = ada_w.shape
    bsz = c.shape[0]
    rows = 8
    c_pad = jnp.zeros((rows, d), F32).at[:bsz].set(c)
    tn = n_out // 4
    out = pl.pallas_call(
        _mod_kernel,
        out_shape=jax.ShapeDtypeStruct((n_layers, rows, n_out), F32),
        grid=(n_layers, n_out // tn),
        in_specs=[pl.BlockSpec((rows, d), lambda l, j: (0, 0)),
                  pl.BlockSpec((None, d, tn), lambda l, j: (l, 0, j)),
                  pl.BlockSpec((None, 1, tn), lambda l, j: (l, 0, j))],
        out_specs=pl.BlockSpec((None, rows, tn), lambda l, j: (l, 0, j)),
        compiler_params=_params(2),
        name="adaln_mod",
    )(c_pad, ada_w, ada_b.reshape(n_layers, 1, n_out))
    return out[:, :bsz]


def _rope_kernel(pos_ref, inv_ref, cos_ref, sin_ref, *, bsz):
    half = HEAD_DIM // 2
    ts = pos_ref.shape[0]
    low = lax.broadcasted_iota(jnp.int32, (ts, HEAD_DIM), 1) < half
    pos = pos_ref[...].astype(F32)
    for b0 in range(0, bsz, 2):
        b1 = min(b0 + 1, bsz - 1)
        ang = jnp.where(low, pos[:, b0:b0 + 1], pos[:, b1:b1 + 1]) * inv_ref[...]
        c = jnp.cos(ang)
        sn = jnp.sin(ang)
        c_swapped = pltpu.roll(c, half, axis=1)
        s_swapped = pltpu.roll(sn, half, axis=1)
        cos_ref[b0] = jnp.where(low, c, c_swapped)
        sin_ref[b0] = jnp.where(low, -sn, s_swapped)
        if b1 != b0:
            cos_ref[b1] = jnp.where(low, c_swapped, c)
            sin_ref[b1] = jnp.where(low, -s_swapped, sn)


def _rope_tables(positions):
    bsz, seq = positions.shape
    inv = ROPE_BASE ** (-jnp.arange(0, HEAD_DIM, 2, dtype=F32) / HEAD_DIM)
    inv2 = jnp.concatenate([inv, inv]).reshape(1, HEAD_DIM)
    ts = min(1024, seq)
    shp = jax.ShapeDtypeStruct((bsz, seq, HEAD_DIM), F32)
    return pl.pallas_call(
        functools.partial(_rope_kernel, bsz=bsz),
        out_shape=(shp, shp),
        grid=(seq // ts,),
        in_specs=[pl.BlockSpec((ts, bsz), lambda i: (i, 0)),
                  pl.BlockSpec((1, HEAD_DIM), lambda i: (0, 0))],
        out_specs=(pl.BlockSpec((bsz, ts, HEAD_DIM), lambda i: (0, i, 0)),
                   pl.BlockSpec((bsz, ts, HEAD_DIM), lambda i: (0, i, 0))),
        compiler_params=_params(1),
        name="rope_tables",
    )(positions.T, inv2)


def _norm_modulate(x, gain, shift, scale):
    y = x * lax.rsqrt(jnp.mean(x * x, axis=-1, keepdims=True) + EPS)
    return (y * gain) * (1.0 + scale) + shift


def _proj_mixers_kernel(x_ref, mod_ref, g_ref, w_ref, ws_ref, cw_ref, ab_ref, ng_ref, cos_ref, sin_ref, gn_ref,
                        pw_ref, psc_ref, lng_ref, lnb_ref, sgw_ref, sgb_ref, w1_ref, w2_ref, wgt_ref,
                        wba_ref, wbb_ref, wbc_ref, wbd_ref, wo_ref,
                        o_ref, ob_ref, oc_ref, od_ref, w1b_ref, w2b_ref, wgb_ref,
                        wbab_ref, wbbb_ref, wbcb_ref, wbdb_ref, wob_ref,
                        cbuf, s_ref, rs_ref, rc_ref, pbuf, *, bsz, d, n_sub):
    for src, dst in ((w1_ref, w1b_ref), (w2_ref, w2b_ref), (wba_ref, wbab_ref), (wbb_ref, wbbb_ref),
                     (wbc_ref, wbcb_ref), (wbd_ref, wbdb_ref), (wo_ref, wob_ref)):
        dst[...] = src[...].astype(BF16)
    wgb_ref[...] = wgt_ref[0].T.astype(BF16)
    ts = n_sub * CHUNK
    cw = 3 * WIDTH
    nh = N_HEADS
    inst = [(b, c, h) for b in range(bsz) for c in range(n_sub) for h in range(nh)]
    seqs = [(b, h) for b in range(bsz) for h in range(nh)]
    log_gamma = [math.log1p(-2.0 ** (-5.0 - h)) for h in range(nh)]
    row = lax.broadcasted_iota(jnp.int32, (CHUNK, CHUNK), 0)
    col = lax.broadcasted_iota(jnp.int32, (CHUNK, CHUNK), 1)

    @pl.when(pl.program_id(0) == 0)
    def _():
        cbuf[:, 0:CONV_HALO, :] = jnp.zeros((bsz, CONV_HALO, cw), F32)
        s_ref[...] = jnp.zeros_like(s_ref)
        rs_ref[...] = jnp.zeros_like(rs_ref)
        pbuf[:, 0:POOL_HALO, :] = jnp.zeros((bsz, POOL_HALO, WIDTH), F32)
        rel = (row - col).astype(F32)
        rowf = row.astype(F32)
        for h in range(nh):
            rc_ref[h] = jnp.where(row >= col, jnp.exp(log_gamma[h] * jnp.maximum(rel, 0.0)), 0.0)
            rc_ref[nh + h] = jnp.exp(log_gamma[h] * (float(CHUNK - 1) - rowf))
            rc_ref[2 * nh + h] = jnp.exp(log_gamma[h] * (rowf + 1.0))

    def lanes(h):
        return slice(h * HEAD_DIM, (h + 1) * HEAD_DIM)

    def sub(c):
        return slice(c * CHUNK, (c + 1) * CHUNK)

    def flat(b, c):
        return slice(b * ts + c * CHUNK, b * ts + (c + 1) * CHUNK)

    hb = jnp.concatenate(
        [_norm_modulate(x_ref[b], g_ref[...], mod_ref[b, :, 0:d], mod_ref[b, :, d:2 * d]).astype(BF16)
         for b in range(bsz)], axis=0)

    rest = {}

    def rest_project(j):
        if j not in rest:
            rest[j] = _dot(hb, w_ref[:, (N_DN_CB + j) * WIDTH:(N_DN_CB + j + 1) * WIDTH])

    def pool_stage():
        count = (pl.program_id(0) * ts + 1 + lax.broadcasted_iota(jnp.int32, (ts, HEAD_DIM), 0)).astype(F32)
        for b in range(bsz):
            p = rest[CB_PL][b * ts:(b + 1) * ts]
            pbuf[b, POOL_HALO:POOL_HALO + ts, :] = p
            for g, win in enumerate(POOL_WINDOWS):
                cur = pbuf[b, :, lanes(g)]
                span = 1
                while span < win:
                    cur = cur + pltpu.roll(cur, span, axis=0)
                    span *= 2
                acc = cur[POOL_HALO:POOL_HALO + ts]
                pooled = acc / jnp.minimum(count, float(win)) - p[:, lanes(g)]
                y = _dot(pooled.astype(BF16), pw_ref[g]) * psc_ref[:, lanes(g)]
                ob_ref[b, :, lanes(g)] = y.astype(BF16)
            pbuf[b, 0:POOL_HALO, :] = pbuf[b, ts:ts + POOL_HALO, :]

    def sgu_stage():
        u = _gelu(rest[CB_SU])
        v = _gelu(rest[CB_SV])
        mu = jnp.mean(v, axis=-1, keepdims=True)
        vc = v - mu
        var = jnp.mean(vc * vc, axis=-1, keepdims=True)
        vb = (vc * lax.rsqrt(var + EPS) * lng_ref[...] + lnb_ref[...]).astype(BF16)
        for g in range(nh):
            wg = jnp.where(row >= col, sgw_ref[g], 0.0).astype(BF16)
            for b in range(bsz):
                for c in range(n_sub):
                    mixed = _dot(wg, vb[flat(b, c), lanes(g)]) + sgb_ref[:, lanes(g)]
                    oc_ref[b, sub(c), lanes(g)] = (u[flat(b, c), lanes(g)] * mixed).astype(BF16)

    ret = {}

    def ret_project():
        lo = (N_DN_CB + N_REST_CB) * WIDTH
        ret["in"] = [_dot(hb, w_ref[:, lo + j * WIDTH:lo + (j + 1) * WIDTH]) for j in range(N_RET_CB)]

    def ret_part(j, b, c, h):
        return ret["in"][j][flat(b, c), lanes(h)]

    def ret_scores():
        def rotary(t, b, c):
            return t * cos_ref[b, sub(c), :] + pltpu.roll(t, HEAD_DIM // 2, axis=1) * sin_ref[b, sub(c), :]

        ret["q"] = [rotary(ret_part(0, b, c, h), b, c).astype(BF16) for b, c, h in inst]
        ret["k"] = [rotary(ret_part(1, b, c, h), b, c) * HEAD_DIM ** -0.5 for b, c, h in inst]
        ret["v"] = [ret_part(2, b, c, h).astype(BF16) for b, c, h in inst]
        ret["scores"] = [(_dot_nt(qr, kr.astype(BF16)) * rc_ref[h]).astype(BF16)
                         for qr, kr, (b, c, h) in zip(ret["q"], ret["k"], inst)]

    def ret_state():
        state = [rs_ref[n] for n in range(len(seqs))]
        ret["o"] = {}
        for c in range(n_sub):
            for n, (b, h) in enumerate(seqs):
                i = inst.index((b, c, h))
                ret["o"][i] = (_dot(ret["scores"][i], ret["v"][i])
                               + _dot(ret["q"][i], state[n].astype(BF16)) * rc_ref[2 * nh + h])
                kv = _dot_tn((ret["k"][i] * rc_ref[nh + h]).astype(BF16), ret["v"][i])
                state[n] = state[n] * math.exp(log_gamma[h] * CHUNK) + kv
        for n, s in enumerate(state):
            rs_ref[n] = s

    def ret_store():
        for i, (b, c, h) in enumerate(inst):
            mu = jnp.mean(ret["o"][i], axis=-1, keepdims=True)
            oc = ret["o"][i] - mu
            var = jnp.mean(oc * oc, axis=-1, keepdims=True)
            y = oc * lax.rsqrt(var + EPS) * gn_ref[:, lanes(h)]
            od_ref[b, sub(c), lanes(h)] = (_silu(ret_part(3, b, c, h)) * y).astype(BF16)

    ret_stages = [ret_scores, ret_state, ret_store]

    sm_all = _dot(hb, ws_ref[...])
    qkvz = [_dot(hb, w_ref[:, j * WIDTH:(j + 1) * WIDTH]) for j in range(N_DN_CB)]

    def same_block(size):
        return (row // size) == (col // size)

    mask_incl = jnp.logical_and(same_block(DN_CHUNK), row >= col)
    mask_strict = jnp.logical_and(same_block(DN_CHUNK), row > col)
    eye = jnp.where(row == col, 1.0, 0.0).astype(F32)
    row_in_chunk = lax.broadcasted_iota(jnp.int32, (ts, HEAD_DIM), 0) % DN_CHUNK
    first_chunk = row < DN_CHUNK

    qkv, g_all, beta_all = [], [], []
    for b in range(bsz):
        rows = slice(b * ts, (b + 1) * ts)
        for part in range(3):
            cbuf[b, CONV_HALO:CONV_HALO + ts, part * WIDTH:(part + 1) * WIDTH] = qkvz[part][rows]
        acc = cbuf[b, CONV_HALO:CONV_HALO + ts, :] * cw_ref[DN_CONV - 1:DN_CONV, :]
        for kk in range(DN_CONV - 1):
            off = CONV_HALO - (DN_CONV - 1) + kk
            acc = acc + cbuf[b, off:off + ts, :] * cw_ref[kk:kk + 1, :]
        cbuf[b, 0:CONV_HALO, :] = cbuf[b, ts:ts + CONV_HALO, :]
        qkv.append(_silu(acc))

        sm = sm_all[rows]
        beta_all.append(_sigmoid(sm))
        g = -jnp.exp(ab_ref[0:1, :]) * _softplus(sm + ab_ref[1:2, :])
        for s in (1, 2, 4, 8, 16, 32):
            g = g + jnp.where(row_in_chunk >= s, pltpu.roll(g, s, axis=0), 0.0)
        g_all.append(g)
        rest_project(min(b, N_REST_CB - 1))

    gb = [jnp.broadcast_to(g_all[b][sub(c), N_HEADS + h:N_HEADS + h + 1], (CHUNK, CHUNK))
          for b, c, h in inst]
    bb = [jnp.broadcast_to(beta_all[b][sub(c), h:h + 1], (CHUNK, CHUNK)) for b, c, h in inst]
    dec = [jnp.exp(jnp.where(mask_incl, g - g.T, -jnp.inf)) for g in gb]
    for j in range(N_REST_CB):
        rest_project(j)
    e_g = [jnp.exp(g) for g in gb]
    gl = [(g[DN_CHUNK - 1:DN_CHUNK, :], g[2 * DN_CHUNK - 1:2 * DN_CHUNK, :]) for g in gb]
    e_gl = [jnp.exp(jnp.where(first_chunk, l0, l1) - g) for g, (l0, l1) in zip(gb, gl)]

    def head(b, c, h, part):
        lo = part * WIDTH + h * HEAD_DIM
        return qkv[b][sub(c), lo:lo + HEAD_DIM]

    def l2n(t):
        return t * lax.rsqrt(jnp.sum(t * t, axis=-1, keepdims=True) + EPS)

    qn = [l2n(head(b, c, h, 0)) * HEAD_DIM ** -0.5 for b, c, h in inst]
    kn = [l2n(head(b, c, h, 1)) for b, c, h in inst]
    kb = [k * bt for k, bt in zip(kn, bb)]
    knb = [k.astype(BF16) for k in kn]
    lm = [jnp.where(mask_strict, _dot_nt(a.astype(BF16), k) * dc, 0.0) for a, k, dc in zip(kb, knb, dec)]
    qk = [(_dot_nt(q.astype(BF16), k) * dc).astype(BF16) for q, k, dc in zip(qn, knb, dec)]

    l8 = [jnp.where(same_block(8), m, 0.0) for m in lm]
    l8b = [m.astype(BF16) for m in l8]
    p1b = [_dot(m, m).astype(BF16) for m in l8b]
    tm = [eye - m for m in l8]
    tm = [t + _dot(t.astype(BF16), p) for t, p in zip(tm, p1b)]
    p2b = [_dot(p, p).astype(BF16) for p in p1b]
    tm = [t + _dot(t.astype(BF16), p) for t, p in zip(tm, p2b)]
    ret_project()
    for size, ret_stage in zip((8, 16, 32), ret_stages):
        lower_left = jnp.logical_and(same_block(2 * size), jnp.logical_not(same_block(size)))
        cm = [jnp.where(lower_left, m, 0.0).astype(BF16) for m in lm]
        tb = [t.astype(BF16) for t in tm]
        xm = [_dot(c_, t).astype(BF16) for c_, t in zip(cm, tb)]
        tm = [t - _dot(t16, x) for t, t16, x in zip(tm, tb, xm)]
        ret_stage()

    rhs = [jnp.concatenate([head(b, c, h, 2) * bt, k * e], axis=1).astype(BF16)
           for (b, c, h), bt, k, e in zip(inst, bb, kb, e_g)]
    uw = [_dot(t.astype(BF16), r) for t, r in zip(tm, rhs)]
    pool_stage()
    u = [m[:, 0:HEAD_DIM] for m in uw]
    w = [m[:, HEAD_DIM:2 * HEAD_DIM].astype(BF16) for m in uw]
    qs = [(q * e).astype(BF16) for q, e in zip(qn, e_g)]
    ks = [(k * e).astype(BF16) for k, e in zip(kn, e_gl)]

    state = [s_ref[n] for n in range(len(seqs))]
    vns = [[None, None] for _ in inst]
    outs = [[None, None] for _ in inst]
    for c in range(n_sub):
        ids = [inst.index((b, c, h)) for b, h in seqs]
        for half in range(2):
            rows = slice(half * DN_CHUNK, (half + 1) * DN_CHUNK)
            sb = [s.astype(BF16) for s in state]
            for n, i in enumerate(ids):
                vns[i][half] = u[i][rows] - _dot(w[i][rows], sb[n])
            for n, i in enumerate(ids):
                outs[i][half] = _dot(qs[i][rows], sb[n])
            state = [s * jnp.exp(gl[i][half]) + _dot_tn(ks[i][rows], vns[i][half].astype(BF16))
                     for s, i in zip(state, ids)]
    for n, s in enumerate(state):
        s_ref[n] = s
    sgu_stage()

    for i, (b, c, h) in enumerate(inst):
        vn = jnp.concatenate(vns[i], axis=0).astype(BF16)
        o = jnp.concatenate(outs[i], axis=0) + _dot(qk[i], vn)
        y = o * lax.rsqrt(jnp.mean(o * o, axis=-1, keepdims=True) + EPS) * ng_ref[...]
        y = y * _silu(qkvz[3][flat(b, c), lanes(h)])
        o_ref[b, sub(c), lanes(h)] = y.astype(BF16)


def _proj_mixers(x3, mod_l, layer, p, cos2, sin2, w_in_t):
    bsz, seq, d = x3.shape
    n_sub = 2 if seq % (2 * CHUNK) == 0 else 1
    ts = n_sub * CHUNK
    n_steps = seq // ts
    d_ff = p["mlp_w1"].shape[-1]
    assert d % (16 * n_steps) == 0 and d_ff % (16 * n_steps) == 0 and WIDTH % (16 * n_steps) == 0, (d, d_ff, n_steps)
    n_gates = N_BRANCH * d
    o_gates = w_in_t.shape[1] - n_gates
    gate_slab = n_gates // n_steps
    assert gate_slab % HEAD_DIM == 0 and o_gates % 8 == 0, (gate_slab, o_gates)
    n_in = (N_DN_CB + N_REST_CB + N_RET_CB) * WIDTH
    tab = pl.BlockSpec((bsz, ts, HEAD_DIM), lambda s: (0, s, 0))
    branch = jax.ShapeDtypeStruct((bsz, seq, WIDTH), BF16)
    state = pltpu.VMEM((bsz * N_HEADS, HEAD_DIM, HEAD_DIM), F32)

    def layer_spec(*shape):
        return pl.BlockSpec((None,) + shape, lambda s: (layer,) + (0,) * len(shape))

    row_vec = layer_spec(1, WIDTH)
    group_mat = layer_spec(N_HEADS, CHUNK, CHUNK)
    branch_spec = pl.BlockSpec((bsz, ts, WIDTH), lambda s: (0, s, 0))
    return pl.pallas_call(
        functools.partial(_proj_mixers_kernel, bsz=bsz, d=d, n_sub=n_sub),
        out_shape=(branch, branch, branch, branch,
                   jax.ShapeDtypeStruct((d, d_ff), BF16), jax.ShapeDtypeStruct((d_ff, d), BF16),
                   jax.ShapeDtypeStruct((d, n_gates), BF16),
                   *[jax.ShapeDtypeStruct((WIDTH, d), BF16)] * N_BRANCH, jax.ShapeDtypeStruct((d, d), BF16)),
        grid=(n_steps,),
        in_specs=[pl.BlockSpec((bsz, ts, d), lambda s: (0, s, 0)),
                  pl.BlockSpec((bsz, 1, N_MOD * d), lambda s: (0, 0, 0)),
                  layer_spec(1, d),
                  layer_spec(d, n_in),
                  layer_spec(d, HEAD_DIM),
                  layer_spec(DN_CONV, 3 * WIDTH),
                  layer_spec(2, HEAD_DIM),
                  layer_spec(1, HEAD_DIM),
                  tab, tab, row_vec,
                  group_mat, row_vec, row_vec, row_vec, group_mat,
                  layer_spec(CHUNK, WIDTH),
                  pl.BlockSpec((None, d // n_steps, d_ff), lambda s: (layer, s, 0)),
                  pl.BlockSpec((None, d_ff // n_steps, d), lambda s: (layer, s, 0)),
                  pl.BlockSpec((pl.Element(1), pl.Element(gate_slab), pl.Element(d)),
                               lambda s: (layer, pl.multiple_of(o_gates + s * gate_slab, 8), 0)),
                  *[pl.BlockSpec((None, WIDTH // n_steps, d), lambda s: (layer, s, 0))] * N_BRANCH,
                  pl.BlockSpec((None, d // n_steps, d), lambda s: (layer, s, 0))],
        out_specs=(branch_spec, branch_spec, branch_spec, branch_spec,
                   pl.BlockSpec((d // n_steps, d_ff), lambda s: (s, 0)),
                   pl.BlockSpec((d_ff // n_steps, d), lambda s: (s, 0)),
                   pl.BlockSpec((d, gate_slab), lambda s: (0, s)),
                   *[pl.BlockSpec((WIDTH // n_steps, d), lambda s: (s, 0))] * N_BRANCH,
                   pl.BlockSpec((d // n_steps, d), lambda s: (s, 0))),
        scratch_shapes=[pltpu.VMEM((bsz, ts + CONV_HALO, 3 * WIDTH), F32), state, state,
                        pltpu.VMEM((3 * N_HEADS, CHUNK, CHUNK), F32),
                        pltpu.VMEM((bsz, ts + POOL_HALO, WIDTH), F32)],
        compiler_params=_params(1),
        name="proj_mixers",
    )(x3, mod_l, p["norm1_g"], p["w_main"], p["w_small"], p["dn_conv_w"], p["dn_ab"], p["dn_norm_g"],
      cos2, sin2, p["ret_gn_g"],
      p["pool_w"], p["pool_scale"], p["sg_ln_g"], p["sg_ln_b"], p["sg_w"], p["sg_bias"],
      p["mlp_w1"], p["mlp_w2"], w_in_t, *p["w_br"], p["w_out"])


def _merge_kernel(ya_ref, yb_ref, yc_ref, yd_ref, x_ref, mod_ref, g_ref, wg_ref, wa_ref, wb_ref, wc_ref, wd_ref,
                  wo_ref, o_ref, *, d):
    x = x_ref[...]
    hb = _norm_modulate(x, g_ref[...], mod_ref[:, 0:d], mod_ref[:, d:2 * d]).astype(BF16)
    merged = None
    branches = ((ya_ref, wa_ref), (yb_ref, wb_ref), (yc_ref, wc_ref), (yd_ref, wd_ref))
    for i, (y_ref, wbr_ref) in enumerate(branches):
        gate = _sigmoid(_dot(hb, wg_ref[:, i * d:(i + 1) * d]))
        term = gate * _dot(y_ref[...], wbr_ref[...])
        merged = term if merged is None else merged + term
    r = _dot(merged.astype(BF16), wo_ref[...])
    o_ref[...] = x + mod_ref[:, 2 * d:3 * d] * r


def _merge(ys, x2, mod_l, norm_g, w_gates, w_br, w_out, layer, seq):
    n_tok, d = x2.shape
    ts = min(512, seq)
    per_seq = seq // ts
    yspec = pl.BlockSpec((ts, WIDTH), lambda i: (i, 0))
    wspec = pl.BlockSpec((WIDTH, d), lambda i: (0, 0))
    return pl.pallas_call(
        functools.partial(_merge_kernel, d=d),
        out_shape=jax.ShapeDtypeStruct((n_tok, d), F32),
        grid=(n_tok // ts,),
        in_specs=[yspec, yspec, yspec, yspec,
                  pl.BlockSpec((ts, d), lambda i: (i, 0)),
                  pl.BlockSpec((None, 1, N_MOD * d), lambda i: (i // per_seq, 0, 0)),
                  pl.BlockSpec((None, 1, d), lambda i: (layer, 0, 0)),
                  pl.BlockSpec((d, N_BRANCH * d), lambda i: (0, 0)),
                  wspec, wspec, wspec, wspec,
                  pl.BlockSpec((d, d), lambda i: (0, 0))],
        out_specs=pl.BlockSpec((ts, d), lambda i: (i, 0)),
        compiler_params=_params(1),
        name="merge_out",
    )(*ys, x2, mod_l, norm_g, w_gates, *w_br, w_out)


def _mlp_kernel(x_ref, mod_ref, g_ref, w1_ref, w2_ref, fg_ref, o_ref, *, d, tf, final_norm):
    x = x_ref[...]
    hb = _norm_modulate(x, g_ref[...], mod_ref[:, 3 * d:4 * d], mod_ref[:, 4 * d:5 * d]).astype(BF16)
    acc = None
    for f in range(w1_ref.shape[1] // tf):
        a = jnp.maximum(_dot(hb, w1_ref[:, f * tf:(f + 1) * tf]), 0.0)
        part = _dot((a * a).astype(BF16), w2_ref[f * tf:(f + 1) * tf, :])
        acc = part if acc is None else acc + part
    out = x + mod_ref[:, 5 * d:6 * d] * acc
    if final_norm:
        out = out * lax.rsqrt(jnp.mean(out * out, axis=-1, keepdims=True) + EPS) * fg_ref[...]
    o_ref[...] = out


def _mlp(x2, mod_l, norm_g, w1, w2, final_g, layer, seq, final_norm):
    n_tok, d = x2.shape
    d_ff = w1.shape[-1]
    ts = min(512, seq)
    per_seq = seq // ts
    return pl.pallas_call(
        functools.partial(_mlp_kernel, d=d, tf=min(1024, d_ff), final_norm=final_norm),
        out_shape=jax.ShapeDtypeStruct((n_tok, d), F32),
        grid=(n_tok // ts,),
        in_specs=[pl.BlockSpec((ts, d), lambda i: (i, 0)),
                  pl.BlockSpec((None, 1, N_MOD * d), lambda i: (i // per_seq, 0, 0)),
                  pl.BlockSpec((None, 1, d), lambda i: (layer, 0, 0)),
                  pl.BlockSpec((d, d_ff), lambda i: (0, 0)),
                  pl.BlockSpec((d_ff, d), lambda i: (0, 0)),
                  pl.BlockSpec((1, d), lambda i: (0, 0))],
        out_specs=pl.BlockSpec((ts, d), lambda i: (i, 0)),
        compiler_params=_params(1),
        name="mlp",
    )(x2, mod_l, norm_g, w1, w2, final_g.reshape(1, d))


def _split_w_in_kernel(wt_ref, wts_ref, main_ref, small_ref):
    main_ref[...] = wt_ref[0].T.astype(BF16)

    @pl.when(pl.program_id(1) == 0)
    def _():
        lane = lax.broadcasted_iota(jnp.int32, small_ref.shape, 1)
        small_ref[...] = jnp.where(lane < 2 * N_HEADS, wts_ref[0].T, 0.0).astype(BF16)


def _split_w_in(w_t):
    n_layers, _, d = w_t.shape
    n_main_blocks = N_DN_CB + N_REST_CB + N_RET_CB
    o_small = N_DN_CB * WIDTH
    o_rest = o_small + 2 * N_HEADS

    def first_row(j):
        return jnp.where(j < N_DN_CB, j * WIDTH, o_rest + (j - N_DN_CB) * WIDTH)

    main, small = pl.pallas_call(
        _split_w_in_kernel,
        out_shape=(jax.ShapeDtypeStruct((n_layers, d, n_main_blocks * WIDTH), BF16),
                   jax.ShapeDtypeStruct((n_layers, d, HEAD_DIM), BF16)),
        grid=(n_layers, n_main_blocks),
        in_specs=[pl.BlockSpec((pl.Element(1), pl.Element(WIDTH), pl.Element(d)),
                               lambda l, j: (l, pl.multiple_of(first_row(j), 8), 0)),
                  pl.BlockSpec((pl.Element(1), pl.Element(HEAD_DIM), pl.Element(d)), lambda l, j: (l, o_small, 0))],
        out_specs=(pl.BlockSpec((None, d, WIDTH), lambda l, j: (l, 0, j)),
                   pl.BlockSpec((None, d, HEAD_DIM), lambda l, j: (l, 0, 0))),
        compiler_params=_params(2),
        name="split_w_in",
    )(w_t, w_t)
    return main, small


def kernel(x, c, positions, norm1_g, norm2_g, ada_w, ada_b, w_in, dn_conv_w, dn_a_log, dn_dt_bias, dn_norm_g, pool_w, pool_scale, sg_ln_g, sg_ln_b, sg_w, sg_b, ret_gn_g, w_br_dn, w_br_pool, w_br_sg, w_br_ret, w_out, mlp_w1, mlp_w2, final_g):
    bsz, seq, d = x.shape
    n_layers = w_in.shape[0]
    n_tok = bsz * seq
    mod = _modulation(c, ada_w, ada_b)
    cos2, sin2 = _rope_tables(positions)
    x2 = x.reshape(n_tok, d)
    w_in_t = jnp.swapaxes(w_in, 1, 2)
    w_main, w_small = _split_w_in(w_in_t)
    dn_ab = jnp.zeros((n_layers, 2, HEAD_DIM), F32)
    dn_ab = dn_ab.at[:, 0, N_HEADS:2 * N_HEADS].set(dn_a_log).at[:, 1, N_HEADS:2 * N_HEADS].set(dn_dt_bias)
    p = dict(norm1_g=norm1_g.reshape(n_layers, 1, d), w_main=w_main, w_small=w_small,
             dn_conv_w=dn_conv_w, dn_ab=dn_ab, dn_norm_g=dn_norm_g.reshape(n_layers, 1, HEAD_DIM),
             ret_gn_g=ret_gn_g.reshape(n_layers, 1, WIDTH),
             pool_w=pool_w.astype(BF16), pool_scale=pool_scale.reshape(n_layers, 1, WIDTH),
             sg_ln_g=sg_ln_g.reshape(n_layers, 1, WIDTH), sg_ln_b=sg_ln_b.reshape(n_layers, 1, WIDTH), sg_w=sg_w,
             sg_bias=jnp.repeat(jnp.swapaxes(sg_b, 1, 2), HEAD_DIM, axis=2),
             mlp_w1=mlp_w1, mlp_w2=mlp_w2, w_br=(w_br_dn, w_br_pool, w_br_sg, w_br_ret), w_out=w_out)
    norm2 = norm2_g.reshape(n_layers, 1, d)
    for l in range(n_layers):
        mod_l = mod[l].reshape(bsz, 1, N_MOD * d)
        outs = _proj_mixers(x2.reshape(bsz, seq, d), mod_l, l, p, cos2, sin2, w_in_t)
        ys = [y.reshape(n_tok, WIDTH) for y in outs[:N_BRANCH]]
        w1_b, w2_b, w_gates, *w_br, w_out_b = outs[N_BRANCH:]
        x2 = _merge(ys, x2, mod_l, p["norm1_g"], w_gates, w_br, w_out_b, l, seq)
        x2 = _mlp(x2, mod_l, norm2, w1_b, w2_b, final_g, l, seq, final_norm=(l == n_layers - 1))
    return x2.reshape(bsz, seq, d)
```

```python
import functools
import math

import jax
import jax.numpy as jnp
from jax import lax
from jax.experimental import pallas as pl
from jax.experimental.pallas import tpu as pltpu

F32 = jnp.float32
BF16 = jnp.bfloat16

EPS = 1e-6
N_HEADS = 4
HEAD_DIM = 128
WIDTH = N_HEADS * HEAD_DIM
N_BRANCH = 4
N_MOD = 6
DN_CONV = 4
DN_CHUNK = 64
CONV_HALO = 8
CHUNK = 128
POOL_WINDOWS = (2, 4, 8, 16)
POOL_HALO = 16
assert all(w & (w - 1) == 0 and w - 1 <= POOL_HALO for w in POOL_WINDOWS)
ROPE_BASE = 10000.0
VMEM_LIMIT = 52 * 1024 * 1024

N_DN_CB = 4
CB_PL, CB_SU, CB_SV = range(3)
N_REST_CB = 3
N_RET_CB = 4


def _sigmoid(x):
    return 1.0 / (1.0 + jnp.exp(-x))


def _silu(x):
    return x * _sigmoid(x)


def _softplus(x):
    return jnp.maximum(x, 0.0) + jnp.log1p(jnp.exp(-jnp.abs(x)))


def _gelu(x):
    return 0.5 * x * (1.0 + lax.erf(x * (1.0 / math.sqrt(2.0))))


def _dot(a, b):
    return jnp.dot(a, b, preferred_element_type=F32)


def _dot_nt(a, b):
    return lax.dot_general(a, b, (((1,), (1,)), ((), ())), preferred_element_type=F32)


def _dot_tn(a, b):
    return lax.dot_general(a, b, (((0,), (0,)), ((), ())), preferred_element_type=F32)


def _params(n_axes):
    return pltpu.CompilerParams(dimension_semantics=("arbitrary",) * n_axes,
                                vmem_limit_bytes=VMEM_LIMIT)


def _mod_kernel(c_ref, w_ref, b_ref, o_ref):
    cond = _silu(c_ref[...])
    o_ref[...] = _dot(cond.astype(BF16), w_ref[...].astype(BF16)) + b_ref[...]


def _modulation(c, ada_w, ada_b):
    n_layers, d, n_out = ada_w.shape
    bsz = c.shape[0]
    rows = 8
    c_pad = jnp.zeros((rows, d), F32).at[:bsz].set(c)
    tn = n_out // 4
    out = pl.pallas_call(
        _mod_kernel,
        out_shape=jax.ShapeDtypeStruct((n_layers, rows, n_out), F32),
        grid=(n_layers, n_out // tn),
        in_specs=[pl.BlockSpec((rows, d), lambda l, j: (0, 0)),
                  pl.BlockSpec((None, d, tn), lambda l, j: (l, 0, j)),
                  pl.BlockSpec((None, 1, tn), lambda l, j: (l, 0, j))],
        out_specs=pl.BlockSpec((None, rows, tn), lambda l, j: (l, 0, j)),
        compiler_params=_params(2),
        name="adaln_mod",
    )(c_pad, ada_w, ada_b.reshape(n_layers, 1, n_out))
    return out[:, :bsz]


def _rope_kernel(pos_ref, inv_ref, cos_ref, sin_ref, *, bsz):
    half = HEAD_DIM // 2
    ts = pos_ref.shape[0]
    low = lax.broadcasted_iota(jnp.int32, (ts, HEAD_DIM), 1) < half
    pos = pos_ref[...].astype(F32)
    for b0 in range(0, bsz, 2):
        b1 = min(b0 + 1, bsz - 1)
        ang = jnp.where(low, pos[:, b0:b0 + 1], pos[:, b1:b1 + 1]) * inv_ref[...]
        c = jnp.cos(ang)
        sn = jnp.sin(ang)
        c_swapped = pltpu.roll(c, half, axis=1)
        s_swapped = pltpu.roll(sn, half, axis=1)
        cos_ref[b0] = jnp.where(low, c, c_swapped)
        sin_ref[b0] = jnp.where(low, -sn, s_swapped)
        if b1 != b0:
            cos_ref[b1] = jnp.where(low, c_swapped, c)
            sin_ref[b1] = jnp.where(low, -s_swapped, sn)


def _rope_tables(positions):
    bsz, seq = positions.shape
    inv = ROPE_BASE ** (-jnp.arange(0, HEAD_DIM, 2, dtype=F32) / HEAD_DIM)
    inv2 = jnp.concatenate([inv, inv]).reshape(1, HEAD_DIM)
    ts = min(1024, seq)
    shp = jax.ShapeDtypeStruct((bsz, seq, HEAD_DIM), F32)
    return pl.pallas_call(
        functools.partial(_rope_kernel, bsz=bsz),
        out_shape=(shp, shp),
        grid=(seq // ts,),
        in_specs=[pl.BlockSpec((ts, bsz), lambda i: (i, 0)),
                  pl.BlockSpec((1, HEAD_DIM), lambda i: (0, 0))],
        out_specs=(pl.BlockSpec((bsz, ts, HEAD_DIM), lambda i: (0, i, 0)),
                   pl.BlockSpec((bsz, ts, HEAD_DIM), lambda i: (0, i, 0))),
        compiler_params=_params(1),
        name="rope_tables",
    )(positions.T, inv2)


def _norm_modulate(x, gain, shift, scale):
    y = x * lax.rsqrt(jnp.mean(x * x, axis=-1, keepdims=True) + EPS)
    return (y * gain) * (1.0 + scale) + shift


def _proj_mixers_kernel(x_ref, mod_ref, g_ref, w_ref, ws_ref, cw_ref, ab_ref, ng_ref, cos_ref, sin_ref, gn_ref,
                        pw_ref, psc_ref, lng_ref, lnb_ref, sgw_ref, sgb_ref, w1_ref, w2_ref, wgt_ref,
                        wba_ref, wbb_ref, wbc_ref, wbd_ref, wo_ref,
                        o_ref, ob_ref, oc_ref, od_ref, w1b_ref, w2b_ref, wgb_ref,
                        wbab_ref, wbbb_ref, wbcb_ref, wbdb_ref, wob_ref,
                        cbuf, s_ref, rs_ref, rc_ref, pbuf, *, bsz, d, n_sub):
    for src, dst in ((w1_ref, w1b_ref), (w2_ref, w2b_ref), (wba_ref, wbab_ref), (wbb_ref, wbbb_ref),
                     (wbc_ref, wbcb_ref), (wbd_ref, wbdb_ref), (wo_ref, wob_ref)):
        dst[...] = src[...].astype(BF16)
    wgb_ref[...] = wgt_ref[0].T.astype(BF16)
    ts = n_sub * CHUNK
    cw = 3 * WIDTH
    nh = N_HEADS
    inst = [(b, c, h) for b in range(bsz) for c in range(n_sub) for h in range(nh)]
    seqs = [(b, h) for b in range(bsz) for h in range(nh)]
    log_gamma = [math.log1p(-2.0 ** (-5.0 - h)) for h in range(nh)]
    row = lax.broadcasted_iota(jnp.int32, (CHUNK, CHUNK), 0)
    col = lax.broadcasted_iota(jnp.int32, (CHUNK, CHUNK), 1)

    @pl.when(pl.program_id(0) == 0)
    def _():
        cbuf[:, 0:CONV_HALO, :] = jnp.zeros((bsz, CONV_HALO, cw), F32)
        s_ref[...] = jnp.zeros_like(s_ref)
        rs_ref[...] = jnp.zeros_like(rs_ref)
        pbuf[:, 0:POOL_HALO, :] = jnp.zeros((bsz, POOL_HALO, WIDTH), F32)
        rel = (row - col).astype(F32)
        rowf = row.astype(F32)
        for h in range(nh):
            rc_ref[h] = jnp.where(row >= col, jnp.exp(log_gamma[h] * jnp.maximum(rel, 0.0)), 0.0)
            rc_ref[nh + h] = jnp.exp(log_gamma[h] * (float(CHUNK - 1) - rowf))
            rc_ref[2 * nh + h] = jnp.exp(log_gamma[h] * (rowf + 1.0))

    def lanes(h):
        return slice(h * HEAD_DIM, (h + 1) * HEAD_DIM)

    def sub(c):
        return slice(c * CHUNK, (c + 1) * CHUNK)

    def flat(b, c):
        return slice(b * ts + c * CHUNK, b * ts + (c + 1) * CHUNK)

    hb = jnp.concatenate(
        [_norm_modulate(x_ref[b], g_ref[...], mod_ref[b, :, 0:d], mod_ref[b, :, d:2 * d]).astype(BF16)
         for b in range(bsz)], axis=0)

    rest = {}

    def rest_project(j):
        if j not in rest:
            rest[j] = _dot(hb, w_ref[:, (N_DN_CB + j) * WIDTH:(N_DN_CB + j + 1) * WIDTH])

    def pool_stage():
        count = (pl.program_id(0) * ts + 1 + lax.broadcasted_iota(jnp.int32, (ts, HEAD_DIM), 0)).astype(F32)
        for b in range(bsz):
            p = rest[CB_PL][b * ts:(b + 1) * ts]
            pbuf[b, POOL_HALO:POOL_HALO + ts, :] = p
            for g, win in enumerate(POOL_WINDOWS):
                cur = pbuf[b, :, lanes(g)]
                span = 1
                while span < win:
                    cur = cur + pltpu.roll(cur, span, axis=0)
                    span *= 2
                acc = cur[POOL_HALO:POOL_HALO + ts]
                pooled = acc / jnp.minimum(count, float(win)) - p[:, lanes(g)]
                y = _dot(pooled.astype(BF16), pw_ref[g]) * psc_ref[:, lanes(g)]
                ob_ref[b, :, lanes(g)] = y.astype(BF16)
            pbuf[b, 0:POOL_HALO, :] = pbuf[b, ts:ts + POOL_HALO, :]

    def sgu_stage():
        u = _gelu(rest[CB_SU])
        v = _gelu(rest[CB_SV])
        mu = jnp.mean(v, axis=-1, keepdims=True)
        vc = v - mu
        var = jnp.mean(vc * vc, axis=-1, keepdims=True)
        vb = (vc * lax.rsqrt(var + EPS) * lng_ref[...] + lnb_ref[...]).astype(BF16)
        for g in range(nh):
            wg = jnp.where(row >= col, sgw_ref[g], 0.0).astype(BF16)
            for b in range(bsz):
                for c in range(n_sub):
                    mixed = _dot(wg, vb[flat(b, c), lanes(g)]) + sgb_ref[:, lanes(g)]
                    oc_ref[b, sub(c), lanes(g)] = (u[flat(b, c), lanes(g)] * mixed).astype(BF16)

    ret = {}

    def ret_project():
        lo = (N_DN_CB + N_REST_CB) * WIDTH
        ret["in"] = [_dot(hb, w_ref[:, lo + j * WIDTH:lo + (j + 1) * WIDTH]) for j in range(N_RET_CB)]

    def ret_part(j, b, c, h):
        return ret["in"][j][flat(b, c), lanes(h)]

    def ret_scores():
        def rotary(t, b, c):
            return t * cos_ref[b, sub(c), :] + pltpu.roll(t, HEAD_DIM // 2, axis=1) * sin_ref[b, sub(c), :]

        ret["q"] = [rotary(ret_part(0, b, c, h), b, c).astype(BF16) for b, c, h in inst]
        ret["k"] = [rotary(ret_part(1, b, c, h), b, c) * HEAD_DIM ** -0.5 for b, c, h in inst]
        ret["v"] = [ret_part(2, b, c, h).astype(BF16) for b, c, h in inst]
        ret["scores"] = [(_dot_nt(qr, kr.astype(BF16)) * rc_ref[h]).astype(BF16)
                         for qr, kr, (b, c, h) in zip(ret["q"], ret["k"], inst)]

    def ret_state():
        state = [rs_ref[n] for n in range(len(seqs))]
        ret["o"] = {}
        for c in range(n_sub):
            for n, (b, h) in enumerate(seqs):
                i = inst.index((b, c, h))
                ret["o"][i] = (_dot(ret["scores"][i], ret["v"][i])
                               + _dot(ret["q"][i], state[n].astype(BF16)) * rc_ref[2 * nh + h])
                kv = _dot_tn((ret["k"][i] * rc_ref[nh + h]).astype(BF16), ret["v"][i])
                state[n] = state[n] * math.exp(log_gamma[h] * CHUNK) + kv
        for n, s in enumerate(state):
            rs_ref[n] = s

    def ret_store():
        for i, (b, c, h) in enumerate(inst):
            mu = jnp.mean(ret["o"][i], axis=-1, keepdims=True)
            oc = ret["o"][i] - mu
            var = jnp.mean(oc * oc, axis=-1, keepdims=True)
            y = oc * lax.rsqrt(var + EPS) * gn_ref[:, lanes(h)]
            od_ref[b, sub(c), lanes(h)] = (_silu(ret_part(3, b, c, h)) * y).astype(BF16)

    ret_stages = [ret_scores, ret_state, ret_store]

    sm_all = _dot(hb, ws_ref[...])
    qkvz = [_dot(hb, w_ref[:, j * WIDTH:(j + 1) * WIDTH]) for j in range(N_DN_CB)]

    def same_block(size):
        return (row // size) == (col // size)

    mask_incl = jnp.logical_and(same_block(DN_CHUNK), row >= col)
    mask_strict = jnp.logical_and(same_block(DN_CHUNK), row > col)
    eye = jnp.where(row == col, 1.0, 0.0).astype(F32)
    row_in_chunk = lax.broadcasted_iota(jnp.int32, (ts, HEAD_DIM), 0) % DN_CHUNK
    first_chunk = row < DN_CHUNK

    qkv, g_all, beta_all = [], [], []
    for b in range(bsz):
        rows = slice(b * ts, (b + 1) * ts)
        for part in range(3):
            cbuf[b, CONV_HALO:CONV_HALO + ts, part * WIDTH:(part + 1) * WIDTH] = qkvz[part][rows]
        acc = cbuf[b, CONV_HALO:CONV_HALO + ts, :] * cw_ref[DN_CONV - 1:DN_CONV, :]
        for kk in range(DN_CONV - 1):
            off = CONV_HALO - (DN_CONV - 1) + kk
            acc = acc + cbuf[b, off:off + ts, :] * cw_ref[kk:kk + 1, :]
        cbuf[b, 0:CONV_HALO, :] = cbuf[b, ts:ts + CONV_HALO, :]
        qkv.append(_silu(acc))

        sm = sm_all[rows]
        beta_all.append(_sigmoid(sm))
        g = -jnp.exp(ab_ref[0:1, :]) * _softplus(sm + ab_ref[1:2, :])
        for s in (1, 2, 4, 8, 16, 32):
            g = g + jnp.where(row_in_chunk >= s, pltpu.roll(g, s, axis=0), 0.0)
        g_all.append(g)
        rest_project(min(b, N_REST_CB - 1))

    gb = [jnp.broadcast_to(g_all[b][sub(c), N_HEADS + h:N_HEADS + h + 1], (CHUNK, CHUNK))
          for b, c, h in inst]
    bb = [jnp.broadcast_to(beta_all[b][sub(c), h:h + 1], (CHUNK, CHUNK)) for b, c, h in inst]
    dec = [jnp.exp(jnp.where(mask_incl, g - g.T, -jnp.inf)) for g in gb]
    for j in range(N_REST_CB):
        rest_project(j)
    e_g = [jnp.exp(g) for g in gb]
    gl = [(g[DN_CHUNK - 1:DN_CHUNK, :], g[2 * DN_CHUNK - 1:2 * DN_CHUNK, :]) for g in gb]
    e_gl = [jnp.exp(jnp.where(first_chunk, l0, l1) - g) for g, (l0, l1) in zip(gb, gl)]

    def head(b, c, h, part):
        lo = part * WIDTH + h * HEAD_DIM
        return qkv[b][sub(c), lo:lo + HEAD_DIM]

    def l2n(t):
        return t * lax.rsqrt(jnp.sum(t * t, axis=-1, keepdims=True) + EPS)

    qn = [l2n(head(b, c, h, 0)) * HEAD_DIM ** -0.5 for b, c, h in inst]
    kn = [l2n(head(b, c, h, 1)) for b, c, h in inst]
    kb = [k * bt for k, bt in zip(kn, bb)]
    knb = [k.astype(BF16) for k in kn]
    lm = [jnp.where(mask_strict, _dot_nt(a.astype(BF16), k) * dc, 0.0) for a, k, dc in zip(kb, knb, dec)]
    qk = [(_dot_nt(q.astype(BF16), k) * dc).astype(BF16) for q, k, dc in zip(qn, knb, dec)]

    l8 = [jnp.where(same_block(8), m, 0.0) for m in lm]
    l8b = [m.astype(BF16) for m in l8]
    p1b = [_dot(m, m).astype(BF16) for m in l8b]
    tm = [eye - m for m in l8]
    tm = [t + _dot(t.astype(BF16), p) for t, p in zip(tm, p1b)]
    p2b = [_dot(p, p).astype(BF16) for p in p1b]
    tm = [t + _dot(t.astype(BF16), p) for t, p in zip(tm, p2b)]
    ret_project()
    for size, ret_stage in zip((8, 16, 32), ret_stages):
        lower_left = jnp.logical_and(same_block(2 * size), jnp.logical_not(same_block(size)))
        cm = [jnp.where(lower_left, m, 0.0).astype(BF16) for m in lm]
        tb = [t.astype(BF16) for t in tm]
        xm = [_dot(c_, t).astype(BF16) for c_, t in zip(cm, tb)]
        tm = [t - _dot(t16, x) for t, t16, x in zip(tm, tb, xm)]
        ret_stage()

    rhs = [jnp.concatenate([head(b, c, h, 2) * bt, k * e], axis=1).astype(BF16)
           for (b, c, h), bt, k, e in zip(inst, bb, kb, e_g)]
    uw = [_dot(t.astype(BF16), r) for t, r in zip(tm, rhs)]
    pool_stage()
    u = [m[:, 0:HEAD_DIM] for m in uw]
    w = [m[:, HEAD_DIM:2 * HEAD_DIM].astype(BF16) for m in uw]
    qs = [(q * e).astype(BF16) for q, e in zip(qn, e_g)]
    ks = [(k * e).astype(BF16) for k, e in zip(kn, e_gl)]

    state = [s_ref[n] for n in range(len(seqs))]
    vns = [[None, None] for _ in inst]
    outs = [[None, None] for _ in inst]
    for c in range(n_sub):
        ids = [inst.index((b, c, h)) for b, h in seqs]
        for half in range(2):
            rows = slice(half * DN_CHUNK, (half + 1) * DN_CHUNK)
            sb = [s.astype(BF16) for s in state]
            for n, i in enumerate(ids):
                vns[i][half] = u[i][rows] - _dot(w[i][rows], sb[n])
            for n, i in enumerate(ids):
                outs[i][half] = _dot(qs[i][rows], sb[n])
            state = [s * jnp.exp(gl[i][half]) + _dot_tn(ks[i][rows], vns[i][half].astype(BF16))
                     for s, i in zip(state, ids)]
    for n, s in enumerate(state):
        s_ref[n] = s
    sgu_stage()

    for i, (b, c, h) in enumerate(inst):
        vn = jnp.concatenate(vns[i], axis=0).astype(BF16)
        o = jnp.concatenate(outs[i], axis=0) + _dot(qk[i], vn)
        y = o * lax.rsqrt(jnp.mean(o * o, axis=-1, keepdims=True) + EPS) * ng_ref[...]
        y = y * _silu(qkvz[3][flat(b, c), lanes(h)])
        o_ref[b, sub(c), lanes(h)] = y.astype(BF16)


def _proj_mixers(x3, mod_l, layer, p, cos2, sin2, w_in_t):
    bsz, seq, d = x3.shape
    n_sub = 2 if seq % (2 * CHUNK) == 0 else 1
    ts = n_sub * CHUNK
    n_steps = seq // ts
    d_ff = p["mlp_w1"].shape[-1]
    assert d % (16 * n_steps) == 0 and d_ff % (16 * n_steps) == 0 and WIDTH % (16 * n_steps) == 0, (d, d_ff, n_steps)
    n_gates = N_BRANCH * d
    o_gates = w_in_t.shape[1] - n_gates
    gate_slab = n_gates // n_steps
    assert gate_slab % HEAD_DIM == 0 and o_gates % 8 == 0, (gate_slab, o_gates)
    n_in = (N_DN_CB + N_REST_CB + N_RET_CB) * WIDTH
    tab = pl.BlockSpec((bsz, ts, HEAD_DIM), lambda s: (0, s, 0))
    branch = jax.ShapeDtypeStruct((bsz, seq, WIDTH), BF16)
    state = pltpu.VMEM((bsz * N_HEADS, HEAD_DIM, HEAD_DIM), F32)

    def layer_spec(*shape):
        return pl.BlockSpec((None,) + shape, lambda s: (layer,) + (0,) * len(shape))

    row_vec = layer_spec(1, WIDTH)
    group_mat = layer_spec(N_HEADS, CHUNK, CHUNK)
    branch_spec = pl.BlockSpec((bsz, ts, WIDTH), lambda s: (0, s, 0))
    return pl.pallas_call(
        functools.partial(_proj_mixers_kernel, bsz=bsz, d=d, n_sub=n_sub),
        out_shape=(branch, branch, branch, branch,
                   jax.ShapeDtypeStruct((d, d_ff), BF16), jax.ShapeDtypeStruct((d_ff, d), BF16),
                   jax.ShapeDtypeStruct((d, n_gates), BF16),
                   *[jax.ShapeDtypeStruct((WIDTH, d), BF16)] * N_BRANCH, jax.ShapeDtypeStruct((d, d), BF16)),
        grid=(n_steps,),
        in_specs=[pl.BlockSpec((bsz, ts, d), lambda s: (0, s, 0)),
                  pl.BlockSpec((bsz, 1, N_MOD * d), lambda s: (0, 0, 0)),
                  layer_spec(1, d),
                  layer_spec(d, n_in),
                  layer_spec(d, HEAD_DIM),
                  layer_spec(DN_CONV, 3 * WIDTH),
                  layer_spec(2, HEAD_DIM),
                  layer_spec(1, HEAD_DIM),
                  tab, tab, row_vec,
                  group_mat, row_vec, row_vec, row_vec, group_mat,
                  layer_spec(CHUNK, WIDTH),
                  pl.BlockSpec((None, d // n_steps, d_ff), lambda s: (layer, s, 0)),
                  pl.BlockSpec((None, d_ff // n_steps, d), lambda s: (layer, s, 0)),
                  pl.BlockSpec((pl.Element(1), pl.Element(gate_slab), pl.Element(d)),
                               lambda s: (layer, pl.multiple_of(o_gates + s * gate_slab, 8), 0)),
                  *[pl.BlockSpec((None, WIDTH // n_steps, d), lambda s: (layer, s, 0))] * N_BRANCH,
                  pl.BlockSpec((None, d // n_steps, d), lambda s: (layer, s, 0))],
        out_specs=(branch_spec, branch_spec, branch_spec, branch_spec,
                   pl.BlockSpec((d // n_steps, d_ff), lambda s: (s, 0)),
                   pl.BlockSpec((d_ff // n_steps, d), lambda s: (s, 0)),
                   pl.BlockSpec((d, gate_slab), lambda s: (0, s)),
                   *[pl.BlockSpec((WIDTH // n_steps, d), lambda s: (s, 0))] * N_BRANCH,
                   pl.BlockSpec((d // n_steps, d), lambda s: (s, 0))),
        scratch_shapes=[pltpu.VMEM((bsz, ts + CONV_HALO, 3 * WIDTH), F32), state, state,
                        pltpu.VMEM((3 * N_HEADS, CHUNK, CHUNK), F32),
                        pltpu.VMEM((bsz, ts + POOL_HALO, WIDTH), F32)],
        compiler_params=_params(1),
        name="proj_mixers",
    )(x3, mod_l, p["norm1_g"], p["w_main"], p["w_small"], p["dn_conv_w"], p["dn_ab"], p["dn_norm_g"],
      cos2, sin2, p["ret_gn_g"],
      p["pool_w"], p["pool_scale"], p["sg_ln_g"], p["sg_ln_b"], p["sg_w"], p["sg_bias"],
      p["mlp_w1"], p["mlp_w2"], w_in_t, *p["w_br"], p["w_out"])


def _merge_kernel(ya_ref, yb_ref, yc_ref, yd_ref, x_ref, mod_ref, g_ref, wg_ref, wa_ref, wb_ref, wc_ref, wd_ref,
                  wo_ref, o_ref, *, d):
    x = x_ref[...]
    hb = _norm_modulate(x, g_ref[...], mod_ref[:, 0:d], mod_ref[:, d:2 * d]).astype(BF16)
    merged = None
    branches = ((ya_ref, wa_ref), (yb_ref, wb_ref), (yc_ref, wc_ref), (yd_ref, wd_ref))
    for i, (y_ref, wbr_ref) in enumerate(branches):
        gate = _sigmoid(_dot(hb, wg_ref[:, i * d:(i + 1) * d]))
        term = gate * _dot(y_ref[...], wbr_ref[...])
        merged = term if merged is None else merged + term
    r = _dot(merged.astype(BF16), wo_ref[...])
    o_ref[...] = x + mod_ref[:, 2 * d:3 * d] * r


def _merge(ys, x2, mod_l, norm_g, w_gates, w_br, w_out, layer, seq):
    n_tok, d = x2.shape
    ts = min(1024, seq)
    per_seq = seq // ts
    yspec = pl.BlockSpec((ts, WIDTH), lambda i: (i, 0))
    wspec = pl.BlockSpec((WIDTH, d), lambda i: (0, 0))
    return pl.pallas_call(
        functools.partial(_merge_kernel, d=d),
        out_shape=jax.ShapeDtypeStruct((n_tok, d), F32),
        grid=(n_tok // ts,),
        in_specs=[yspec, yspec, yspec, yspec,
                  pl.BlockSpec((ts, d), lambda i: (i, 0)),
                  pl.BlockSpec((None, 1, N_MOD * d), lambda i: (i // per_seq, 0, 0)),
                  pl.BlockSpec((None, 1, d), lambda i: (layer, 0, 0)),
                  pl.BlockSpec((d, N_BRANCH * d), lambda i: (0, 0)),
                  wspec, wspec, wspec, wspec,
                  pl.BlockSpec((d, d), lambda i: (0, 0))],
        out_specs=pl.BlockSpec((ts, d), lambda i: (i, 0)),
        compiler_params=_params(1),
        name="merge_out",
    )(*ys, x2, mod_l, norm_g, w_gates, *w_br, w_out)


def _mlp_kernel(x_ref, mod_ref, g_ref, w1_ref, w2_ref, fg_ref, o_ref, *, d, tf, final_norm):
    x = x_ref[...]
    hb = _norm_modulate(x, g_ref[...], mod_ref[:, 3 * d:4 * d], mod_ref[:, 4 * d:5 * d]).astype(BF16)
    acc = None
    for f in range(w1_ref.shape[1] // tf):
        a = jnp.maximum(_dot(hb, w1_ref[:, f * tf:(f + 1) * tf]), 0.0)
        part = _dot((a * a).astype(BF16), w2_ref[f * tf:(f + 1) * tf, :])
        acc = part if acc is None else acc + part
    out = x + mod_ref[:, 5 * d:6 * d] * acc
    if final_norm:
        out = out * lax.rsqrt(jnp.mean(out * out, axis=-1, keepdims=True) + EPS) * fg_ref[...]
    o_ref[...] = out


def _mlp(x2, mod_l, norm_g, w1, w2, final_g, layer, seq, final_norm):
    n_tok, d = x2.shape
    d_ff = w1.shape[-1]
    ts = min(1024, seq)
    per_seq = seq // ts
    return pl.pallas_call(
        functools.partial(_mlp_kernel, d=d, tf=min(1024, d_ff), final_norm=final_norm),
        out_shape=jax.ShapeDtypeStruct((n_tok, d), F32),
        grid=(n_tok // ts,),
        in_specs=[pl.BlockSpec((ts, d), lambda i: (i, 0)),
                  pl.BlockSpec((None, 1, N_MOD * d), lambda i: (i // per_seq, 0, 0)),
                  pl.BlockSpec((None, 1, d), lambda i: (layer, 0, 0)),
                  pl.BlockSpec((d, d_ff), lambda i: (0, 0)),
                  pl.BlockSpec((d_ff, d), lambda i: (0, 0)),
                  pl.BlockSpec((1, d), lambda i: (0, 0))],
        out_specs=pl.BlockSpec((ts, d), lambda i: (i, 0)),
        compiler_params=_params(1),
        name="mlp",
    )(x2, mod_l, norm_g, w1, w2, final_g.reshape(1, d))


def _split_w_in_kernel(wt_ref, wts_ref, main_ref, small_ref):
    main_ref[...] = wt_ref[0].T.astype(BF16)

    @pl.when(pl.program_id(1) == 0)
    def _():
        lane = lax.broadcasted_iota(jnp.int32, small_ref.shape, 1)
        small_ref[...] = jnp.where(lane < 2 * N_HEADS, wts_ref[0].T, 0.0).astype(BF16)


def _split_w_in(w_t):
    n_layers, _, d = w_t.shape
    n_main_blocks = N_DN_CB + N_REST_CB + N_RET_CB
    o_small = N_DN_CB * WIDTH
    o_rest = o_small + 2 * N_HEADS

    def first_row(j):
        return jnp.where(j < N_DN_CB, j * WIDTH, o_rest + (j - N_DN_CB) * WIDTH)

    main, small = pl.pallas_call(
        _split_w_in_kernel,
        out_shape=(jax.ShapeDtypeStruct((n_layers, d, n_main_blocks * WIDTH), BF16),
                   jax.ShapeDtypeStruct((n_layers, d, HEAD_DIM), BF16)),
        grid=(n_layers, n_main_blocks),
        in_specs=[pl.BlockSpec((pl.Element(1), pl.Element(WIDTH), pl.Element(d)),
                               lambda l, j: (l, pl.multiple_of(first_row(j), 8), 0)),
                  pl.BlockSpec((pl.Element(1), pl.Element(HEAD_DIM), pl.Element(d)), lambda l, j: (l, o_small, 0))],
        out_specs=(pl.BlockSpec((None, d, WIDTH), lambda l, j: (l, 0, j)),
                   pl.BlockSpec((None, d, HEAD_DIM), lambda l, j: (l, 0, 0))),
        compiler_params=_params(2),
        name="split_w_in",
    )(w_t, w_t)
    return main, small


def kernel(x, c, positions, norm1_g, norm2_g, ada_w, ada_b, w_in, dn_conv_w, dn_a_log, dn_dt_bias, dn_norm_g, pool_w, pool_scale, sg_ln_g, sg_ln_b, sg_w, sg_b, ret_gn_g, w_br_dn, w_br_pool, w_br_sg, w_br_ret, w_out, mlp_w1, mlp_w2, final_g):
    bsz, seq, d = x.shape
    n_layers = w_in.shape[0]
    n_tok = bsz * seq
    mod = _modulation(c, ada_w, ada_b)
    cos2, sin2 = _rope_tables(positions)
    x2 = x.reshape(n_tok, d)
    w_in_t = jnp.swapaxes(w_in, 1, 2)
    w_main, w_small = _split_w_in(w_in_t)
    dn_ab = jnp.zeros((n_layers, 2, HEAD_DIM), F32)
    dn_ab = dn_ab.at[:, 0, N_HEADS:2 * N_HEADS].set(dn_a_log).at[:, 1, N_HEADS:2 * N_HEADS].set(dn_dt_bias)
    p = dict(norm1_g=norm1_g.reshape(n_layers, 1, d), w_main=w_main, w_small=w_small,
             dn_conv_w=dn_conv_w, dn_ab=dn_ab, dn_norm_g=dn_norm_g.reshape(n_layers, 1, HEAD_DIM),
             ret_gn_g=ret_gn_g.reshape(n_layers, 1, WIDTH),
             pool_w=pool_w.astype(BF16), pool_scale=pool_scale.reshape(n_layers, 1, WIDTH),
             sg_ln_g=sg_ln_g.reshape(n_layers, 1, WIDTH), sg_ln_b=sg_ln_b.reshape(n_layers, 1, WIDTH), sg_w=sg_w,
             sg_bias=jnp.repeat(jnp.swapaxes(sg_b, 1, 2), HEAD_DIM, axis=2),
             mlp_w1=mlp_w1, mlp_w2=mlp_w2, w_br=(w_br_dn, w_br_pool, w_br_sg, w_br_ret), w_out=w_out)
    norm2 = norm2_g.reshape(n_layers, 1, d)
    for l in range(n_layers):
        mod_l = mod[l].reshape(bsz, 1, N_MOD * d)
        outs = _proj_mixers(x2.reshape(bsz, seq, d), mod_l, l, p, cos2, sin2, w_in_t)
        ys = [y.reshape(n_tok, WIDTH) for y in outs[:N_BRANCH]]
        w1_b, w2_b, w_gates, *w_br, w_out_b = outs[N_BRANCH:]
        x2 = _merge(ys, x2, mod_l, p["norm1_g"], w_gates, w_br, w_out_b, l, seq)
        x2 = _mlp(x2, mod_l, norm2, w1_b, w2_b, final_g, l, seq, final_norm=(l == n_layers - 1))
    return x2.reshape(bsz, seq, d)
```

```python
import functools
import math

import jax
import jax.numpy as jnp
from jax import lax
from jax.experimental import pallas as pl
from jax.experimental.pallas import tpu as pltpu

F32 = jnp.float32
BF16 = jnp.bfloat16

EPS = 1e-6
N_HEADS = 4
HEAD_DIM = 128
WIDTH = N_HEADS * HEAD_DIM
N_BRANCH = 4
N_MOD = 6
DN_CONV = 4
DN_CHUNK = 64
CONV_HALO = 8
CHUNK = 128
POOL_WINDOWS = (2, 4, 8, 16)
POOL_HALO = 16
assert all(w & (w - 1) == 0 and w - 1 <= POOL_HALO for w in POOL_WINDOWS)
ROPE_BASE = 10000.0
VMEM_LIMIT = 52 * 1024 * 1024

N_DN_CB = 4
CB_PL, CB_SU, CB_SV = range(3)
N_REST_CB = 3
N_RET_CB = 4


def _sigmoid(x):
    return 1.0 / (1.0 + jnp.exp(-x))


def _silu(x):
    return x * _sigmoid(x)


def _softplus(x):
    return jnp.maximum(x, 0.0) + jnp.log1p(jnp.exp(-jnp.abs(x)))


def _gelu(x):
    return 0.5 * x * (1.0 + lax.erf(x * (1.0 / math.sqrt(2.0))))


def _dot(a, b):
    return jnp.dot(a, b, preferred_element_type=F32)


def _dot_nt(a, b):
    return lax.dot_general(a, b, (((1,), (1,)), ((), ())), preferred_element_type=F32)


def _dot_tn(a, b):
    return lax.dot_general(a, b, (((0,), (0,)), ((), ())), preferred_element_type=F32)


def _params(n_axes):
    return pltpu.CompilerParams(dimension_semantics=("arbitrary",) * n_axes,
                                vmem_limit_bytes=VMEM_LIMIT)


def _mod_kernel(c_ref, w_ref, b_ref, o_ref):
    cond = _silu(c_ref[...])
    o_ref[...] = _dot(cond.astype(BF16), w_ref[...].astype(BF16)) + b_ref[...]


def _modulation(c, ada_w, ada_b):
    n_layers, d, n_out = ada_w.shape
    bsz = c.shape[0]
    rows = 8
    c_pad = jnp.zeros((rows, d), F32).at[:bsz].set(c)
    tn = n_out // 4
    out = pl.pallas_call(
        _mod_kernel,
        out_shape=jax.ShapeDtypeStruct((n_layers, rows, n_out), F32),
        grid=(n_layers, n_out // tn),
        in_specs=[pl.BlockSpec((rows, d), lambda l, j: (0, 0)),
                  pl.BlockSpec((None, d, tn), lambda l, j: (l, 0, j)),
                  pl.BlockSpec((None, 1, tn), lambda l, j: (l, 0, j))],
        out_specs=pl.BlockSpec((None, rows, tn), lambda l, j: (l, 0, j)),
        compiler_params=_params(2),
        name="adaln_mod",
    )(c_pad, ada_w, ada_b.reshape(n_layers, 1, n_out))
    return out[:, :bsz]


def _rope_kernel(pos_ref, inv_ref, cos_ref, sin_ref, *, bsz):
    half = HEAD_DIM // 2
    ts = pos_ref.shape[0]
    low = lax.broadcasted_iota(jnp.int32, (ts, HEAD_DIM), 1) < half
    pos = pos_ref[...].astype(F32)
    for b0 in range(0, bsz, 2):
        b1 = min(b0 + 1, bsz - 1)
        ang = jnp.where(low, pos[:, b0:b0 + 1], pos[:, b1:b1 + 1]) * inv_ref[...]
        c = jnp.cos(ang)
        sn = jnp.sin(ang)
        c_swapped = pltpu.roll(c, half, axis=1)
        s_swapped = pltpu.roll(sn, half, axis=1)
        cos_ref[b0] = jnp.where(low, c, c_swapped)
        sin_ref[b0] = jnp.where(low, -sn, s_swapped)
        if b1 != b0:
            cos_ref[b1] = jnp.where(low, c_swapped, c)
            sin_ref[b1] = jnp.where(low, -s_swapped, sn)


def _rope_tables(positions):
    bsz, seq = positions.shape
    inv = ROPE_BASE ** (-jnp.arange(0, HEAD_DIM, 2, dtype=F32) / HEAD_DIM)
    inv2 = jnp.concatenate([inv, inv]).reshape(1, HEAD_DIM)
    ts = min(1024, seq)
    shp = jax.ShapeDtypeStruct((bsz, seq, HEAD_DIM), F32)
    return pl.pallas_call(
        functools.partial(_rope_kernel, bsz=bsz),
        out_shape=(shp, shp),
        grid=(seq // ts,),
        in_specs=[pl.BlockSpec((ts, bsz), lambda i: (i, 0)),
                  pl.BlockSpec((1, HEAD_DIM), lambda i: (0, 0))],
        out_specs=(pl.BlockSpec((bsz, ts, HEAD_DIM), lambda i: (0, i, 0)),
                   pl.BlockSpec((bsz, ts, HEAD_DIM), lambda i: (0, i, 0))),
        compiler_params=_params(1),
        name="rope_tables",
    )(positions.T, inv2)


def _norm_modulate(x, gain, shift, scale):
    y = x * lax.rsqrt(jnp.mean(x * x, axis=-1, keepdims=True) + EPS)
    return (y * gain) * (1.0 + scale) + shift


def _proj_mixers_kernel(x_ref, mod_ref, g_ref, w_ref, ws_ref, cw_ref, ab_ref, ng_ref, cos_ref, sin_ref, gn_ref,
                        pw_ref, psc_ref, lng_ref, lnb_ref, sgw_ref, sgb_ref, w1_ref, w2_ref, wgt_ref,
                        wba_ref, wbb_ref, wbc_ref, wbd_ref, wo_ref,
                        o_ref, ob_ref, oc_ref, od_ref, w1b_ref, w2b_ref, wgb_ref,
                        wbab_ref, wbbb_ref, wbcb_ref, wbdb_ref, wob_ref,
                        cbuf, s_ref, rs_ref, rc_ref, pbuf, *, bsz, d, n_sub):
    for src, dst in ((w1_ref, w1b_ref), (w2_ref, w2b_ref), (wba_ref, wbab_ref), (wbb_ref, wbbb_ref),
                     (wbc_ref, wbcb_ref), (wbd_ref, wbdb_ref), (wo_ref, wob_ref)):
        dst[...] = src[...].astype(BF16)
    wgb_ref[...] = wgt_ref[0].T.astype(BF16)
    ts = n_sub * CHUNK
    cw = 3 * WIDTH
    nh = N_HEADS
    inst = [(b, c, h) for b in range(bsz) for c in range(n_sub) for h in range(nh)]
    seqs = [(b, h) for b in range(bsz) for h in range(nh)]
    log_gamma = [math.log1p(-2.0 ** (-5.0 - h)) for h in range(nh)]
    row = lax.broadcasted_iota(jnp.int32, (CHUNK, CHUNK), 0)
    col = lax.broadcasted_iota(jnp.int32, (CHUNK, CHUNK), 1)

    @pl.when(pl.program_id(0) == 0)
    def _():
        cbuf[:, 0:CONV_HALO, :] = jnp.zeros((bsz, CONV_HALO, cw), F32)
        s_ref[...] = jnp.zeros_like(s_ref)
        rs_ref[...] = jnp.zeros_like(rs_ref)
        pbuf[:, 0:POOL_HALO, :] = jnp.zeros((bsz, POOL_HALO, WIDTH), F32)
        rel = (row - col).astype(F32)
        rowf = row.astype(F32)
        for h in range(nh):
            rc_ref[h] = jnp.where(row >= col, jnp.exp(log_gamma[h] * jnp.maximum(rel, 0.0)), 0.0)
            rc_ref[nh + h] = jnp.exp(log_gamma[h] * (float(CHUNK - 1) - rowf))
            rc_ref[2 * nh + h] = jnp.exp(log_gamma[h] * (rowf + 1.0))

    def lanes(h):
        return slice(h * HEAD_DIM, (h + 1) * HEAD_DIM)

    def sub(c):
        return slice(c * CHUNK, (c + 1) * CHUNK)

    def flat(b, c):
        return slice(b * ts + c * CHUNK, b * ts + (c + 1) * CHUNK)

    hb = jnp.concatenate(
        [_norm_modulate(x_ref[b], g_ref[...], mod_ref[b, :, 0:d], mod_ref[b, :, d:2 * d]).astype(BF16)
         for b in range(bsz)], axis=0)

    rest = {}

    def rest_project(j):
        if j not in rest:
            rest[j] = _dot(hb, w_ref[:, (N_DN_CB + j) * WIDTH:(N_DN_CB + j + 1) * WIDTH])

    def pool_stage():
        count = (pl.program_id(0) * ts + 1 + lax.broadcasted_iota(jnp.int32, (ts, HEAD_DIM), 0)).astype(F32)
        for b in range(bsz):
            p = rest[CB_PL][b * ts:(b + 1) * ts]
            pbuf[b, POOL_HALO:POOL_HALO + ts, :] = p
            for g, win in enumerate(POOL_WINDOWS):
                cur = pbuf[b, :, lanes(g)]
                span = 1
                while span < win:
                    cur = cur + pltpu.roll(cur, span, axis=0)
                    span *= 2
                acc = cur[POOL_HALO:POOL_HALO + ts]
                pooled = acc / jnp.minimum(count, float(win)) - p[:, lanes(g)]
                y = _dot(pooled.astype(BF16), pw_ref[g]) * psc_ref[:, lanes(g)]
                ob_ref[b, :, lanes(g)] = y.astype(BF16)
            pbuf[b, 0:POOL_HALO, :] = pbuf[b, ts:ts + POOL_HALO, :]

    def sgu_stage():
        u = _gelu(rest[CB_SU])
        v = _gelu(rest[CB_SV])
        mu = jnp.mean(v, axis=-1, keepdims=True)
        vc = v - mu
        var = jnp.mean(vc * vc, axis=-1, keepdims=True)
        vb = (vc * lax.rsqrt(var + EPS) * lng_ref[...] + lnb_ref[...]).astype(BF16)
        for g in range(nh):
            wg = jnp.where(row >= col, sgw_ref[g], 0.0).astype(BF16)
            for b in range(bsz):
                for c in range(n_sub):
                    mixed = _dot(wg, vb[flat(b, c), lanes(g)]) + sgb_ref[:, lanes(g)]
                    oc_ref[b, sub(c), lanes(g)] = (u[flat(b, c), lanes(g)] * mixed).astype(BF16)

    ret = {}

    def ret_project():
        lo = (N_DN_CB + N_REST_CB) * WIDTH
        ret["in"] = [_dot(hb, w_ref[:, lo + j * WIDTH:lo + (j + 1) * WIDTH]) for j in range(N_RET_CB)]

    def ret_part(j, b, c, h):
        return ret["in"][j][flat(b, c), lanes(h)]

    def ret_scores():
        def rotary(t, b, c):
            return t * cos_ref[b, sub(c), :] + pltpu.roll(t, HEAD_DIM // 2, axis=1) * sin_ref[b, sub(c), :]

        ret["q"] = [rotary(ret_part(0, b, c, h), b, c).astype(BF16) for b, c, h in inst]
        ret["k"] = [rotary(ret_part(1, b, c, h), b, c) * HEAD_DIM ** -0.5 for b, c, h in inst]
        ret["v"] = [ret_part(2, b, c, h).astype(BF16) for b, c, h in inst]
        ret["scores"] = [(_dot_nt(qr, kr.astype(BF16)) * rc_ref[h]).astype(BF16)
                         for qr, kr, (b, c, h) in zip(ret["q"], ret["k"], inst)]

    def ret_state():
        state = [rs_ref[n] for n in range(len(seqs))]
        ret["o"] = {}
        for c in range(n_sub):
            for n, (b, h) in enumerate(seqs):
                i = inst.index((b, c, h))
                ret["o"][i] = (_dot(ret["scores"][i], ret["v"][i])
                               + _dot(ret["q"][i], state[n].astype(BF16)) * rc_ref[2 * nh + h])
                kv = _dot_tn((ret["k"][i] * rc_ref[nh + h]).astype(BF16), ret["v"][i])
                state[n] = state[n] * math.exp(log_gamma[h] * CHUNK) + kv
        for n, s in enumerate(state):
            rs_ref[n] = s

    def ret_store():
        for i, (b, c, h) in enumerate(inst):
            mu = jnp.mean(ret["o"][i], axis=-1, keepdims=True)
            oc = ret["o"][i] - mu
            var = jnp.mean(oc * oc, axis=-1, keepdims=True)
            y = oc * lax.rsqrt(var + EPS) * gn_ref[:, lanes(h)]
            od_ref[b, sub(c), lanes(h)] = (_silu(ret_part(3, b, c, h)) * y).astype(BF16)

    ret_stages = [ret_scores, ret_state, ret_store]

    sm_all = _dot(hb, ws_ref[...])
    qkvz = [_dot(hb, w_ref[:, j * WIDTH:(j + 1) * WIDTH]) for j in range(N_DN_CB)]

    def same_block(size):
        return (row // size) == (col // size)

    mask_incl = jnp.logical_and(same_block(DN_CHUNK), row >= col)
    mask_strict = jnp.logical_and(same_block(DN_CHUNK), row > col)
    eye = jnp.where(row == col, 1.0, 0.0).astype(F32)
    row_in_chunk = lax.broadcasted_iota(jnp.int32, (ts, HEAD_DIM), 0) % DN_CHUNK
    first_chunk = row < DN_CHUNK

    qkv, g_all, beta_all = [], [], []
    for b in range(bsz):
        rows = slice(b * ts, (b + 1) * ts)
        for part in range(3):
            cbuf[b, CONV_HALO:CONV_HALO + ts, part * WIDTH:(part + 1) * WIDTH] = qkvz[part][rows]
        acc = cbuf[b, CONV_HALO:CONV_HALO + ts, :] * cw_ref[DN_CONV - 1:DN_CONV, :]
        for kk in range(DN_CONV - 1):
            off = CONV_HALO - (DN_CONV - 1) + kk
            acc = acc + cbuf[b, off:off + ts, :] * cw_ref[kk:kk + 1, :]
        cbuf[b, 0:CONV_HALO, :] = cbuf[b, ts:ts + CONV_HALO, :]
        qkv.append(_silu(acc))

        sm = sm_all[rows]
        beta_all.append(_sigmoid(sm))
        g = -jnp.exp(ab_ref[0:1, :]) * _softplus(sm + ab_ref[1:2, :])
        for s in (1, 2, 4, 8, 16, 32):
            g = g + jnp.where(row_in_chunk >= s, pltpu.roll(g, s, axis=0), 0.0)
        g_all.append(g)
        rest_project(min(b, N_REST_CB - 1))

    gb = [jnp.broadcast_to(g_all[b][sub(c), N_HEADS + h:N_HEADS + h + 1], (CHUNK, CHUNK))
          for b, c, h in inst]
    bb = [jnp.broadcast_to(beta_all[b][sub(c), h:h + 1], (CHUNK, CHUNK)) for b, c, h in inst]
    dec = [jnp.exp(jnp.where(mask_incl, g - g.T, -jnp.inf)) for g in gb]
    for j in range(N_REST_CB):
        rest_project(j)
    e_g = [jnp.exp(g) for g in gb]
    gl = [(g[DN_CHUNK - 1:DN_CHUNK, :], g[2 * DN_CHUNK - 1:2 * DN_CHUNK, :]) for g in gb]
    e_gl = [jnp.exp(jnp.where(first_chunk, l0, l1) - g) for g, (l0, l1) in zip(gb, gl)]

    def head(b, c, h, part):
        lo = part * WIDTH + h * HEAD_DIM
        return qkv[b][sub(c), lo:lo + HEAD_DIM]

    def l2n(t):
        return t * lax.rsqrt(jnp.sum(t * t, axis=-1, keepdims=True) + EPS)

    qn = [l2n(head(b, c, h, 0)) * HEAD_DIM ** -0.5 for b, c, h in inst]
    kn = [l2n(head(b, c, h, 1)) for b, c, h in inst]
    kb = [k * bt for k, bt in zip(kn, bb)]
    knb = [k.astype(BF16) for k in kn]
    lm = [jnp.where(mask_strict, _dot_nt(a.astype(BF16), k) * dc, 0.0) for a, k, dc in zip(kb, knb, dec)]
    qk = [(_dot_nt(q.astype(BF16), k) * dc).astype(BF16) for q, k, dc in zip(qn, knb, dec)]

    l8 = [jnp.where(same_block(8), m, 0.0) for m in lm]
    l8b = [m.astype(BF16) for m in l8]
    p1b = [_dot(m, m).astype(BF16) for m in l8b]
    tm = [eye - m for m in l8]
    tm = [t + _dot(t.astype(BF16), p) for t, p in zip(tm, p1b)]
    p2b = [_dot(p, p).astype(BF16) for p in p1b]
    tm = [t + _dot(t.astype(BF16), p) for t, p in zip(tm, p2b)]
    ret_project()
    for size, ret_stage in zip((8, 16, 32), ret_stages):
        lower_left = jnp.logical_and(same_block(2 * size), jnp.logical_not(same_block(size)))
        cm = [jnp.where(lower_left, m, 0.0).astype(BF16) for m in lm]
        tb = [t.astype(BF16) for t in tm]
        xm = [_dot(c_, t).astype(BF16) for c_, t in zip(cm, tb)]
        tm = [t - _dot(t16, x) for t, t16, x in zip(tm, tb, xm)]
        ret_stage()

    rhs = [jnp.concatenate([head(b, c, h, 2) * bt, k * e], axis=1).astype(BF16)
           for (b, c, h), bt, k, e in zip(inst, bb, kb, e_g)]
    uw = [_dot(t.astype(BF16), r) for t, r in zip(tm, rhs)]
    pool_stage()
    u = [m[:, 0:HEAD_DIM] for m in uw]
    w = [m[:, HEAD_DIM:2 * HEAD_DIM].astype(BF16) for m in uw]
    qs = [(q * e).astype(BF16) for q, e in zip(qn, e_g)]
    ks = [(k * e).astype(BF16) for k, e in zip(kn, e_gl)]

    state = [s_ref[n] for n in range(len(seqs))]
    vns = [[None, None] for _ in inst]
    outs = [[None, None] for _ in inst]
    for c in range(n_sub):
        ids = [inst.index((b, c, h)) for b, h in seqs]
        for half in range(2):
            rows = slice(half * DN_CHUNK, (half + 1) * DN_CHUNK)
            sb = [s.astype(BF16) for s in state]
            for n, i in enumerate(ids):
                vns[i][half] = u[i][rows] - _dot(w[i][rows], sb[n])
            for n, i in enumerate(ids):
                outs[i][half] = _dot(qs[i][rows], sb[n])
            state = [s * jnp.exp(gl[i][half]) + _dot_tn(ks[i][rows], vns[i][half].astype(BF16))
                     for s, i in zip(state, ids)]
    for n, s in enumerate(state):
        s_ref[n] = s
    sgu_stage()

    for i, (b, c, h) in enumerate(inst):
        vn = jnp.concatenate(vns[i], axis=0).astype(BF16)
        o = jnp.concatenate(outs[i], axis=0) + _dot(qk[i], vn)
        y = o * lax.rsqrt(jnp.mean(o * o, axis=-1, keepdims=True) + EPS) * ng_ref[...]
        y = y * _silu(qkvz[3][flat(b, c), lanes(h)])
        o_ref[b, sub(c), lanes(h)] = y.astype(BF16)


def _proj_mixers(x3, mod_l, layer, p, cos2, sin2, w_in_t):
    bsz, seq, d = x3.shape
    n_sub = 2 if seq % (2 * CHUNK) == 0 else 1
    ts = n_sub * CHUNK
    n_steps = seq // ts
    d_ff = p["mlp_w1"].shape[-1]
    assert d % (16 * n_steps) == 0 and d_ff % (16 * n_steps) == 0 and WIDTH % (16 * n_steps) == 0, (d, d_ff, n_steps)
    n_gates = N_BRANCH * d
    o_gates = w_in_t.shape[1] - n_gates
    gate_slab = n_gates // n_steps
    assert gate_slab % HEAD_DIM == 0 and o_gates % 8 == 0, (gate_slab, o_gates)
    n_in = (N_DN_CB + N_REST_CB + N_RET_CB) * WIDTH
    tab = pl.BlockSpec((bsz, ts, HEAD_DIM), lambda s: (0, s, 0))
    branch = jax.ShapeDtypeStruct((bsz, seq, WIDTH), BF16)
    state = pltpu.VMEM((bsz * N_HEADS, HEAD_DIM, HEAD_DIM), F32)

    def layer_spec(*shape):
        return pl.BlockSpec((None,) + shape, lambda s: (layer,) + (0,) * len(shape))

    row_vec = layer_spec(1, WIDTH)
    group_mat = layer_spec(N_HEADS, CHUNK, CHUNK)
    branch_spec = pl.BlockSpec((bsz, ts, WIDTH), lambda s: (0, s, 0))
    return pl.pallas_call(
        functools.partial(_proj_mixers_kernel, bsz=bsz, d=d, n_sub=n_sub),
        out_shape=(branch, branch, branch, branch,
                   jax.ShapeDtypeStruct((d, d_ff), BF16), jax.ShapeDtypeStruct((d_ff, d), BF16),
                   jax.ShapeDtypeStruct((d, n_gates), BF16),
                   *[jax.ShapeDtypeStruct((WIDTH, d), BF16)] * N_BRANCH, jax.ShapeDtypeStruct((d, d), BF16)),
        grid=(n_steps,),
        in_specs=[pl.BlockSpec((bsz, ts, d), lambda s: (0, s, 0)),
                  pl.BlockSpec((bsz, 1, N_MOD * d), lambda s: (0, 0, 0)),
                  layer_spec(1, d),
                  layer_spec(d, n_in),
                  layer_spec(d, HEAD_DIM),
                  layer_spec(DN_CONV, 3 * WIDTH),
                  layer_spec(2, HEAD_DIM),
                  layer_spec(1, HEAD_DIM),
                  tab, tab, row_vec,
                  group_mat, row_vec, row_vec, row_vec, group_mat,
                  layer_spec(CHUNK, WIDTH),
                  pl.BlockSpec((None, d // n_steps, d_ff), lambda s: (layer, s, 0)),
                  pl.BlockSpec((None, d_ff // n_steps, d), lambda s: (layer, s, 0)),
                  pl.BlockSpec((pl.Element(1), pl.Element(gate_slab), pl.Element(d)),
                               lambda s: (layer, pl.multiple_of(o_gates + s * gate_slab, 8), 0)),
                  *[pl.BlockSpec((None, WIDTH // n_steps, d), lambda s: (layer, s, 0))] * N_BRANCH,
                  pl.BlockSpec((None, d // n_steps, d), lambda s: (layer, s, 0))],
        out_specs=(branch_spec, branch_spec, branch_spec, branch_spec,
                   pl.BlockSpec((d // n_steps, d_ff), lambda s: (s, 0)),
                   pl.BlockSpec((d_ff // n_steps, d), lambda s: (s, 0)),
                   pl.BlockSpec((d, gate_slab), lambda s: (0, s)),
                   *[pl.BlockSpec((WIDTH // n_steps, d), lambda s: (s, 0))] * N_BRANCH,
                   pl.BlockSpec((d // n_steps, d), lambda s: (s, 0))),
        scratch_shapes=[pltpu.VMEM((bsz, ts + CONV_HALO, 3 * WIDTH), F32), state, state,
                        pltpu.VMEM((3 * N_HEADS, CHUNK, CHUNK), F32),
                        pltpu.VMEM((bsz, ts + POOL_HALO, WIDTH), F32)],
        compiler_params=_params(1),
        name="proj_mixers",
    )(x3, mod_l, p["norm1_g"], p["w_main"], p["w_small"], p["dn_conv_w"], p["dn_ab"], p["dn_norm_g"],
      cos2, sin2, p["ret_gn_g"],
      p["pool_w"], p["pool_scale"], p["sg_ln_g"], p["sg_ln_b"], p["sg_w"], p["sg_bias"],
      p["mlp_w1"], p["mlp_w2"], w_in_t, *p["w_br"], p["w_out"])


def _merge_kernel(ya_ref, yb_ref, yc_ref, yd_ref, x_ref, mod_ref, g_ref, wg_ref, wa_ref, wb_ref, wc_ref, wd_ref,
                  wo_ref, o_ref, *, d):
    x = x_ref[...]
    hb = _norm_modulate(x, g_ref[...], mod_ref[:, 0:d], mod_ref[:, d:2 * d]).astype(BF16)
    merged = None
    branches = ((ya_ref, wa_ref), (yb_ref, wb_ref), (yc_ref, wc_ref), (yd_ref, wd_ref))
    for i, (y_ref, wbr_ref) in enumerate(branches):
        gate = _sigmoid(_dot(hb, wg_ref[:, i * d:(i + 1) * d]))
        term = gate * _dot(y_ref[...], wbr_ref[...])
        merged = term if merged is None else merged + term
    r = _dot(merged.astype(BF16), wo_ref[...])
    o_ref[...] = x + mod_ref[:, 2 * d:3 * d] * r


def _merge(ys, x2, mod_l, norm_g, w_gates, w_br, w_out, layer, seq):
    n_tok, d = x2.shape
    ts = min(512, seq)
    per_seq = seq // ts
    yspec = pl.BlockSpec((ts, WIDTH), lambda i: (i, 0))
    wspec = pl.BlockSpec((WIDTH, d), lambda i: (0, 0))
    return pl.pallas_call(
        functools.partial(_merge_kernel, d=d),
        out_shape=jax.ShapeDtypeStruct((n_tok, d), F32),
        grid=(n_tok // ts,),
        in_specs=[yspec, yspec, yspec, yspec,
                  pl.BlockSpec((ts, d), lambda i: (i, 0)),
                  pl.BlockSpec((None, 1, N_MOD * d), lambda i: (i // per_seq, 0, 0)),
                  pl.BlockSpec((None, 1, d), lambda i: (layer, 0, 0)),
                  pl.BlockSpec((d, N_BRANCH * d), lambda i: (0, 0)),
                  wspec, wspec, wspec, wspec,
                  pl.BlockSpec((d, d), lambda i: (0, 0))],
        out_specs=pl.BlockSpec((ts, d), lambda i: (i, 0)),
        compiler_params=_params(1),
        name="merge_out",
    )(*ys, x2, mod_l, norm_g, w_gates, *w_br, w_out)


def _mlp_kernel(x_ref, mod_ref, g_ref, w1_ref, w2_ref, fg_ref, o_ref, *, d, tf, final_norm):
    x = x_ref[...]
    hb = _norm_modulate(x, g_ref[...], mod_ref[:, 3 * d:4 * d], mod_ref[:, 4 * d:5 * d]).astype(BF16)
    acc = None
    for f in range(w1_ref.shape[1] // tf):
        a = jnp.maximum(_dot(hb, w1_ref[:, f * tf:(f + 1) * tf]), 0.0)
        part = _dot((a * a).astype(BF16), w2_ref[f * tf:(f + 1) * tf, :])
        acc = part if acc is None else acc + part
    out = x + mod_ref[:, 5 * d:6 * d] * acc
    if final_norm:
        out = out * lax.rsqrt(jnp.mean(out * out, axis=-1, keepdims=True) + EPS) * fg_ref[...]
    o_ref[...] = out


def _mlp(x2, mod_l, norm_g, w1, w2, final_g, layer, seq, final_norm):
    n_tok, d = x2.shape
    d_ff = w1.shape[-1]
    ts = min(512, seq)
    per_seq = seq // ts
    return pl.pallas_call(
        functools.partial(_mlp_kernel, d=d, tf=min(1024, d_ff), final_norm=final_norm),
        out_shape=jax.ShapeDtypeStruct((n_tok, d), F32),
        grid=(n_tok // ts,),
        in_specs=[pl.BlockSpec((ts, d), lambda i: (i, 0)),
                  pl.BlockSpec((None, 1, N_MOD * d), lambda i: (i // per_seq, 0, 0)),
                  pl.BlockSpec((None, 1, d), lambda i: (layer, 0, 0)),
                  pl.BlockSpec((d, d_ff), lambda i: (0, 0)),
                  pl.BlockSpec((d_ff, d), lambda i: (0, 0)),
                  pl.BlockSpec((1, d), lambda i: (0, 0))],
        out_specs=pl.BlockSpec((ts, d), lambda i: (i, 0)),
        compiler_params=_params(1),
        name="mlp",
    )(x2, mod_l, norm_g, w1, w2, final_g.reshape(1, d))


def _split_w_in_kernel(wt_ref, wts_ref, main_ref, small_ref):
    main_ref[...] = wt_ref[0].T.astype(BF16)

    @pl.when(pl.program_id(1) == 0)
    def _():
        lane = lax.broadcasted_iota(jnp.int32, small_ref.shape, 1)
        small_ref[...] = jnp.where(lane < 2 * N_HEADS, wts_ref[0].T, 0.0).astype(BF16)


def _split_w_in(w_t):
    n_layers, _, d = w_t.shape
    n_main_blocks = N_DN_CB + N_REST_CB + N_RET_CB
    o_small = N_DN_CB * WIDTH
    o_rest = o_small + 2 * N_HEADS

    def first_row(j):
        return jnp.where(j < N_DN_CB, j * WIDTH, o_rest + (j - N_DN_CB) * WIDTH)

    main, small = pl.pallas_call(
        _split_w_in_kernel,
        out_shape=(jax.ShapeDtypeStruct((n_layers, d, n_main_blocks * WIDTH), BF16),
                   jax.ShapeDtypeStruct((n_layers, d, HEAD_DIM), BF16)),
        grid=(n_layers, n_main_blocks),
        in_specs=[pl.BlockSpec((pl.Element(1), pl.Element(WIDTH), pl.Element(d)),
                               lambda l, j: (l, pl.multiple_of(first_row(j), 8), 0)),
                  pl.BlockSpec((pl.Element(1), pl.Element(HEAD_DIM), pl.Element(d)), lambda l, j: (l, o_small, 0))],
        out_specs=(pl.BlockSpec((None, d, WIDTH), lambda l, j: (l, 0, j)),
                   pl.BlockSpec((None, d, HEAD_DIM), lambda l, j: (l, 0, 0))),
        compiler_params=_params(2),
        name="split_w_in",
    )(w_t, w_t)
    return main, small


def kernel(x, c, positions, norm1_g, norm2_g, ada_w, ada_b, w_in, dn_conv_w, dn_a_log, dn_dt_bias, dn_norm_g, pool_w, pool_scale, sg_ln_g, sg_ln_b, sg_w, sg_b, ret_gn_g, w_br_dn, w_br_pool, w_br_sg, w_br_ret, w_out, mlp_w1, mlp_w2, final_g):
    bsz, seq, d = x.shape
    n_layers = w_in.shape[0]
    n_tok = bsz * seq
    mod = _modulation(c, ada_w, ada_b)
    cos2, sin2 = _rope_tables(positions)
    x2 = x.reshape(n_tok, d)
    w_in_t = jnp.swapaxes(w_in, 1, 2)
    w_main, w_small = _split_w_in(w_in_t)
    dn_ab = jnp.zeros((n_layers, 2, HEAD_DIM), F32)
    dn_ab = dn_ab.at[:, 0, N_HEADS:2 * N_HEADS].set(dn_a_log).at[:, 1, N_HEADS:2 * N_HEADS].set(dn_dt_bias)
    p = dict(norm1_g=norm1_g.reshape(n_layers, 1, d), w_main=w_main, w_small=w_small,
             dn_conv_w=dn_conv_w, dn_ab=dn_ab, dn_norm_g=dn_norm_g.reshape(n_layers, 1, HEAD_DIM),
             ret_gn_g=ret_gn_g.reshape(n_layers, 1, WIDTH),
             pool_w=pool_w.astype(BF16), pool_scale=pool_scale.reshape(n_layers, 1, WIDTH),
             sg_ln_g=sg_ln_g.reshape(n_layers, 1, WIDTH), sg_ln_b=sg_ln_b.reshape(n_layers, 1, WIDTH), sg_w=sg_w,
             sg_bias=jnp.repeat(jnp.swapaxes(sg_b, 1, 2), HEAD_DIM, axis=2),
             mlp_w1=mlp_w1, mlp_w2=mlp_w2, w_br=(w_br_dn, w_br_pool, w_br_sg, w_br_ret), w_out=w_out)
    norm2 = norm2_g.reshape(n_layers, 1, d)
    for l in range(n_layers):
        mod_l = mod[l].reshape(bsz, 1, N_MOD * d)
        outs = _proj_mixers(x2.reshape(bsz, seq, d), mod_l, l, p, cos2, sin2, w_in_t)
        ys = [y.reshape(n_tok, WIDTH) for y in outs[:N_BRANCH]]
        w1_b, w2_b, w_gates, *w_br, w_out_b = outs[N_BRANCH:]
        x2 = _merge(ys, x2, mod_l, p["norm1_g"], w_gates, w_br, w_out_b, l, seq)
        x2 = _mlp(x2, mod_l, norm2, w1_b, w2_b, final_g, l, seq, final_norm=(l == n_layers - 1))
    return x2.reshape(bsz, seq, d)
```

```python
import functools
import math

import jax
import jax.numpy as jnp
from jax import lax
from jax.experimental import pallas as pl
from jax.experimental.pallas import tpu as pltpu

F32 = jnp.float32
BF16 = jnp.bfloat16

EPS = 1e-6
N_HEADS = 4
HEAD_DIM = 128
WIDTH = N_HEADS * HEAD_DIM
N_BRANCH = 4
N_MOD = 6
DN_CONV = 4
DN_CHUNK = 64
CONV_HALO = 8
CHUNK = 128
POOL_WINDOWS = (2, 4, 8, 16)
POOL_HALO = 16
assert all(w & (w - 1) == 0 and w - 1 <= POOL_HALO for w in POOL_WINDOWS)
ROPE_BASE = 10000.0
VMEM_LIMIT = 52 * 1024 * 1024

N_DN_CB = 4
CB_PL, CB_SU, CB_SV = range(3)
N_REST_CB = 3
N_RET_CB = 4


def _sigmoid(x):
    return 1.0 / (1.0 + jnp.exp(-x))


def _silu(x):
    return x * _sigmoid(x)


def _softplus(x):
    return jnp.maximum(x, 0.0) + jnp.log1p(jnp.exp(-jnp.abs(x)))


def _gelu(x):
    return 0.5 * x * (1.0 + lax.erf(x * (1.0 / math.sqrt(2.0))))


def _dot(a, b):
    return jnp.dot(a, b, preferred_element_type=F32)


def _dot_nt(a, b):
    return lax.dot_general(a, b, (((1,), (1,)), ((), ())), preferred_element_type=F32)


def _dot_tn(a, b):
    return lax.dot_general(a, b, (((0,), (0,)), ((), ())), preferred_element_type=F32)


def _params(n_axes):
    return pltpu.CompilerParams(dimension_semantics=("arbitrary",) * n_axes,
                                vmem_limit_bytes=VMEM_LIMIT)


def _mod_kernel(c_ref, w_ref, b_ref, o_ref):
    cond = _silu(c_ref[...])
    o_ref[...] = _dot(cond.astype(BF16), w_ref[...].astype(BF16)) + b_ref[...]


def _modulation(c, ada_w, ada_b):
    n_layers, d, n_out = ada_w.shape
    bsz = c.shape[0]
    rows = 8
    c_pad = jnp.zeros((rows, d), F32).at[:bsz].set(c)
    tn = n_out // 4
    out = pl.pallas_call(
        _mod_kernel,
        out_shape=jax.ShapeDtypeStruct((n_layers, rows, n_out), F32),
        grid=(n_layers, n_out // tn),
        in_specs=[pl.BlockSpec((rows, d), lambda l, j: (0, 0)),
                  pl.BlockSpec((None, d, tn), lambda l, j: (l, 0, j)),
                  pl.BlockSpec((None, 1, tn), lambda l, j: (l, 0, j))],
        out_specs=pl.BlockSpec((None, rows, tn), lambda l, j: (l, 0, j)),
        compiler_params=_params(2),
        name="adaln_mod",
    )(c_pad, ada_w, ada_b.reshape(n_layers, 1, n_out))
    return out[:, :bsz]


def _rope_kernel(pos_ref, inv_ref, cos_ref, sin_ref, *, bsz):
    half = HEAD_DIM // 2
    ts = pos_ref.shape[0]
    low = lax.broadcasted_iota(jnp.int32, (ts, HEAD_DIM), 1) < half
    pos = pos_ref[...].astype(F32)
    for b0 in range(0, bsz, 2):
        b1 = min(b0 + 1, bsz - 1)
        ang = jnp.where(low, pos[:, b0:b0 + 1], pos[:, b1:b1 + 1]) * inv_ref[...]
        c = jnp.cos(ang)
        sn = jnp.sin(ang)
        c_swapped = pltpu.roll(c, half, axis=1)
        s_swapped = pltpu.roll(sn, half, axis=1)
        cos_ref[b0] = jnp.where(low, c, c_swapped)
        sin_ref[b0] = jnp.where(low, -sn, s_swapped)
        if b1 != b0:
            cos_ref[b1] = jnp.where(low, c_swapped, c)
            sin_ref[b1] = jnp.where(low, -s_swapped, sn)


def _rope_tables(positions):
    bsz, seq = positions.shape
    inv = ROPE_BASE ** (-jnp.arange(0, HEAD_DIM, 2, dtype=F32) / HEAD_DIM)
    inv2 = jnp.concatenate([inv, inv]).reshape(1, HEAD_DIM)
    ts = min(1024, seq)
    shp = jax.ShapeDtypeStruct((bsz, seq, HEAD_DIM), F32)
    return pl.pallas_call(
        functools.partial(_rope_kernel, bsz=bsz),
        out_shape=(shp, shp),
        grid=(seq // ts,),
        in_specs=[pl.BlockSpec((ts, bsz), lambda i: (i, 0)),
                  pl.BlockSpec((1, HEAD_DIM), lambda i: (0, 0))],
        out_specs=(pl.BlockSpec((bsz, ts, HEAD_DIM), lambda i: (0, i, 0)),
                   pl.BlockSpec((bsz, ts, HEAD_DIM), lambda i: (0, i, 0))),
        compiler_params=_params(1),
        name="rope_tables",
    )(positions.T, inv2)


def _norm_modulate(x, gain, shift, scale):
    y = x * lax.rsqrt(jnp.mean(x * x, axis=-1, keepdims=True) + EPS)
    return (y * gain) * (1.0 + scale) + shift


def _proj_mixers_kernel(x_ref, mod_ref, g_ref, w_ref, ws_ref, cw_ref, ab_ref, ng_ref, cos_ref, sin_ref, gn_ref,
                        pw_ref, psc_ref, lng_ref, lnb_ref, sgw_ref, sgb_ref, w1_ref, w2_ref, wgt_ref,
                        wba_ref, wbb_ref, wbc_ref, wbd_ref, wo_ref,
                        o_ref, ob_ref, oc_ref, od_ref, w1b_ref, w2b_ref, wgb_ref,
                        wbab_ref, wbbb_ref, wbcb_ref, wbdb_ref, wob_ref,
                        cbuf, s_ref, rs_ref, rc_ref, pbuf, *, bsz, d, n_sub):
    for src, dst in ((w1_ref, w1b_ref), (w2_ref, w2b_ref), (wba_ref, wbab_ref), (wbb_ref, wbbb_ref),
                     (wbc_ref, wbcb_ref), (wbd_ref, wbdb_ref), (wo_ref, wob_ref)):
        dst[...] = src[...].astype(BF16)
    wgb_ref[...] = wgt_ref[0].T.astype(BF16)
    ts = n_sub * CHUNK
    cw = 3 * WIDTH
    nh = N_HEADS
    inst = [(b, c, h) for b in range(bsz) for c in range(n_sub) for h in range(nh)]
    seqs = [(b, h) for b in range(bsz) for h in range(nh)]
    log_gamma = [math.log1p(-2.0 ** (-5.0 - h)) for h in range(nh)]
    row = lax.broadcasted_iota(jnp.int32, (CHUNK, CHUNK), 0)
    col = lax.broadcasted_iota(jnp.int32, (CHUNK, CHUNK), 1)

    @pl.when(pl.program_id(0) == 0)
    def _():
        cbuf[:, 0:CONV_HALO, :] = jnp.zeros((bsz, CONV_HALO, cw), F32)
        s_ref[...] = jnp.zeros_like(s_ref)
        rs_ref[...] = jnp.zeros_like(rs_ref)
        pbuf[:, 0:POOL_HALO, :] = jnp.zeros((bsz, POOL_HALO, WIDTH), F32)
        rel = (row - col).astype(F32)
        rowf = row.astype(F32)
        for h in range(nh):
            rc_ref[h] = jnp.where(row >= col, jnp.exp(log_gamma[h] * jnp.maximum(rel, 0.0)), 0.0)
            rc_ref[nh + h] = jnp.exp(log_gamma[h] * (float(CHUNK - 1) - rowf))
            rc_ref[2 * nh + h] = jnp.exp(log_gamma[h] * (rowf + 1.0))

    def lanes(h):
        return slice(h * HEAD_DIM, (h + 1) * HEAD_DIM)

    def sub(c):
        return slice(c * CHUNK, (c + 1) * CHUNK)

    def flat(b, c):
        return slice(b * ts + c * CHUNK, b * ts + (c + 1) * CHUNK)

    hb = jnp.concatenate(
        [_norm_modulate(x_ref[b], g_ref[...], mod_ref[b, :, 0:d], mod_ref[b, :, d:2 * d]).astype(BF16)
         for b in range(bsz)], axis=0)

    rest = {}

    def rest_project(j):
        if j not in rest:
            rest[j] = _dot(hb, w_ref[:, (N_DN_CB + j) * WIDTH:(N_DN_CB + j + 1) * WIDTH])

    def pool_stage():
        count = (pl.program_id(0) * ts + 1 + lax.broadcasted_iota(jnp.int32, (ts, HEAD_DIM), 0)).astype(F32)
        for b in range(bsz):
            p = rest[CB_PL][b * ts:(b + 1) * ts]
            pbuf[b, POOL_HALO:POOL_HALO + ts, :] = p
            for g, win in enumerate(POOL_WINDOWS):
                cur = pbuf[b, :, lanes(g)]
                span = 1
                while span < win:
                    cur = cur + pltpu.roll(cur, span, axis=0)
                    span *= 2
                acc = cur[POOL_HALO:POOL_HALO + ts]
                pooled = acc / jnp.minimum(count, float(win)) - p[:, lanes(g)]
                y = _dot(pooled.astype(BF16), pw_ref[g]) * psc_ref[:, lanes(g)]
                ob_ref[b, :, lanes(g)] = y.astype(BF16)
            pbuf[b, 0:POOL_HALO, :] = pbuf[b, ts:ts + POOL_HALO, :]

    def sgu_stage():
        u = _gelu(rest[CB_SU])
        v = _gelu(rest[CB_SV])
        mu = jnp.mean(v, axis=-1, keepdims=True)
        vc = v - mu
        var = jnp.mean(vc * vc, axis=-1, keepdims=True)
        vb = (vc * lax.rsqrt(var + EPS) * lng_ref[...] + lnb_ref[...]).astype(BF16)
        for g in range(nh):
            wg = jnp.where(row >= col, sgw_ref[g], 0.0).astype(BF16)
            for b in range(bsz):
                for c in range(n_sub):
                    mixed = _dot(wg, vb[flat(b, c), lanes(g)]) + sgb_ref[:, lanes(g)]
                    oc_ref[b, sub(c), lanes(g)] = (u[flat(b, c), lanes(g)] * mixed).astype(BF16)

    ret = {}

    def ret_project():
        lo = (N_DN_CB + N_REST_CB) * WIDTH
        ret["in"] = [_dot(hb, w_ref[:, lo + j * WIDTH:lo + (j + 1) * WIDTH]) for j in range(N_RET_CB)]

    def ret_part(j, b, c, h):
        return ret["in"][j][flat(b, c), lanes(h)]

    def ret_scores():
        def rotary(t, b, c):
            return t * cos_ref[b, sub(c), :] + pltpu.roll(t, HEAD_DIM // 2, axis=1) * sin_ref[b, sub(c), :]

        ret["q"] = [rotary(ret_part(0, b, c, h), b, c).astype(BF16) for b, c, h in inst]
        ret["k"] = [rotary(ret_part(1, b, c, h), b, c) * HEAD_DIM ** -0.5 for b, c, h in inst]
        ret["v"] = [ret_part(2, b, c, h).astype(BF16) for b, c, h in inst]
        ret["scores"] = [(_dot_nt(qr, kr.astype(BF16)) * rc_ref[h]).astype(BF16)
                         for qr, kr, (b, c, h) in zip(ret["q"], ret["k"], inst)]

    def ret_state():
        state = [rs_ref[n] for n in range(len(seqs))]
        ret["o"] = {}
        for c in range(n_sub):
            for n, (b, h) in enumerate(seqs):
                i = inst.index((b, c, h))
                ret["o"][i] = (_dot(ret["scores"][i], ret["v"][i])
                               + _dot(ret["q"][i], state[n].astype(BF16)) * rc_ref[2 * nh + h])
                kv = _dot_tn((ret["k"][i] * rc_ref[nh + h]).astype(BF16), ret["v"][i])
                state[n] = state[n] * math.exp(log_gamma[h] * CHUNK) + kv
        for n, s in enumerate(state):
            rs_ref[n] = s

    def ret_store():
        for i, (b, c, h) in enumerate(inst):
            mu = jnp.mean(ret["o"][i], axis=-1, keepdims=True)
            oc = ret["o"][i] - mu
            var = jnp.mean(oc * oc, axis=-1, keepdims=True)
            y = oc * lax.rsqrt(var + EPS) * gn_ref[:, lanes(h)]
            od_ref[b, sub(c), lanes(h)] = (_silu(ret_part(3, b, c, h)) * y).astype(BF16)

    ret_stages = [ret_scores, ret_state, ret_store]

    sm_all = _dot(hb, ws_ref[...])
    qkvz = [_dot(hb, w_ref[:, j * WIDTH:(j + 1) * WIDTH]) for j in range(N_DN_CB)]

    def same_block(size):
        return (row // size) == (col // size)

    mask_incl = jnp.logical_and(same_block(DN_CHUNK), row >= col)
    mask_strict = jnp.logical_and(same_block(DN_CHUNK), row > col)
    eye = jnp.where(row == col, 1.0, 0.0).astype(F32)
    row_in_chunk = lax.broadcasted_iota(jnp.int32, (ts, HEAD_DIM), 0) % DN_CHUNK
    first_chunk = row < DN_CHUNK

    qkv, g_all, beta_all = [], [], []
    for b in range(bsz):
        rows = slice(b * ts, (b + 1) * ts)
        for part in range(3):
            cbuf[b, CONV_HALO:CONV_HALO + ts, part * WIDTH:(part + 1) * WIDTH] = qkvz[part][rows]
        acc = cbuf[b, CONV_HALO:CONV_HALO + ts, :] * cw_ref[DN_CONV - 1:DN_CONV, :]
        for kk in range(DN_CONV - 1):
            off = CONV_HALO - (DN_CONV - 1) + kk
            acc = acc + cbuf[b, off:off + ts, :] * cw_ref[kk:kk + 1, :]
        cbuf[b, 0:CONV_HALO, :] = cbuf[b, ts:ts + CONV_HALO, :]
        qkv.append(_silu(acc))

        sm = sm_all[rows]
        beta_all.append(_sigmoid(sm))
        g = -jnp.exp(ab_ref[0:1, :]) * _softplus(sm + ab_ref[1:2, :])
        for s in (1, 2, 4, 8, 16, 32):
            g = g + jnp.where(row_in_chunk >= s, pltpu.roll(g, s, axis=0), 0.0)
        g_all.append(g)
        rest_project(min(b, N_REST_CB - 1))

    gb = [jnp.broadcast_to(g_all[b][sub(c), N_HEADS + h:N_HEADS + h + 1], (CHUNK, CHUNK))
          for b, c, h in inst]
    bb = [jnp.broadcast_to(beta_all[b][sub(c), h:h + 1], (CHUNK, CHUNK)) for b, c, h in inst]
    dec = [jnp.exp(jnp.where(mask_incl, g - g.T, -jnp.inf)) for g in gb]
    for j in range(N_REST_CB):
        rest_project(j)
    e_g = [jnp.exp(g) for g in gb]
    gl = [(g[DN_CHUNK - 1:DN_CHUNK, :], g[2 * DN_CHUNK - 1:2 * DN_CHUNK, :]) for g in gb]
    e_gl = [jnp.exp(jnp.where(first_chunk, l0, l1) - g) for g, (l0, l1) in zip(gb, gl)]

    def head(b, c, h, part):
        lo = part * WIDTH + h * HEAD_DIM
        return qkv[b][sub(c), lo:lo + HEAD_DIM]

    def l2n(t):
        return t * lax.rsqrt(jnp.sum(t * t, axis=-1, keepdims=True) + EPS)

    qn = [l2n(head(b, c, h, 0)) * HEAD_DIM ** -0.5 for b, c, h in inst]
    kn = [l2n(head(b, c, h, 1)) for b, c, h in inst]
    kb = [k * bt for k, bt in zip(kn, bb)]
    knb = [k.astype(BF16) for k in kn]
    lm = [jnp.where(mask_strict, _dot_nt(a.astype(BF16), k) * dc, 0.0) for a, k, dc in zip(kb, knb, dec)]
    qk = [(_dot_nt(q.astype(BF16), k) * dc).astype(BF16) for q, k, dc in zip(qn, knb, dec)]

    l8 = [jnp.where(same_block(8), m, 0.0) for m in lm]
    l8b = [m.astype(BF16) for m in l8]
    p1b = [_dot(m, m).astype(BF16) for m in l8b]
    tm = [eye - m for m in l8]
    tm = [t + _dot(t.astype(BF16), p) for t, p in zip(tm, p1b)]
    p2b = [_dot(p, p).astype(BF16) for p in p1b]
    tm = [t + _dot(t.astype(BF16), p) for t, p in zip(tm, p2b)]
    ret_project()
    for size, ret_stage in zip((8, 16, 32), ret_stages):
        lower_left = jnp.logical_and(same_block(2 * size), jnp.logical_not(same_block(size)))
        cm = [jnp.where(lower_left, m, 0.0).astype(BF16) for m in lm]
        tb = [t.astype(BF16) for t in tm]
        xm = [_dot(c_, t).astype(BF16) for c_, t in zip(cm, tb)]
        tm = [t - _dot(t16, x) for t, t16, x in zip(tm, tb, xm)]
        ret_stage()

    rhs = [jnp.concatenate([head(b, c, h, 2) * bt, k * e], axis=1).astype(BF16)
           for (b, c, h), bt, k, e in zip(inst, bb, kb, e_g)]
    uw = [_dot(t.astype(BF16), r) for t, r in zip(tm, rhs)]
    pool_stage()
    u = [m[:, 0:HEAD_DIM] for m in uw]
    w = [m[:, HEAD_DIM:2 * HEAD_DIM].astype(BF16) for m in uw]
    qs = [(q * e).astype(BF16) for q, e in zip(qn, e_g)]
    ks = [(k * e).astype(BF16) for k, e in zip(kn, e_gl)]

    state = [s_ref[n] for n in range(len(seqs))]
    vns = [[None, None] for _ in inst]
    outs = [[None, None] for _ in inst]
    for c in range(n_sub):
        ids = [inst.index((b, c, h)) for b, h in seqs]
        for half in range(2):
            rows = slice(half * DN_CHUNK, (half + 1) * DN_CHUNK)
            sb = [s.astype(BF16) for s in state]
            for n, i in enumerate(ids):
                vns[i][half] = u[i][rows] - _dot(w[i][rows], sb[n])
            for n, i in enumerate(ids):
                outs[i][half] = _dot(qs[i][rows], sb[n])
            state = [s * jnp.exp(gl[i][half]) + _dot_tn(ks[i][rows], vns[i][half].astype(BF16))
                     for s, i in zip(state, ids)]
    for n, s in enumerate(state):
        s_ref[n] = s
    sgu_stage()

    for i, (b, c, h) in enumerate(inst):
        vn = jnp.concatenate(vns[i], axis=0).astype(BF16)
        o = jnp.concatenate(outs[i], axis=0) + _dot(qk[i], vn)
        y = o * lax.rsqrt(jnp.mean(o * o, axis=-1, keepdims=True) + EPS) * ng_ref[...]
        y = y * _silu(qkvz[3][flat(b, c), lanes(h)])
        o_ref[b, sub(c), lanes(h)] = y.astype(BF16)


def _proj_mixers(x3, mod_l, layer, p, cos2, sin2, w_in_t):
    bsz, seq, d = x3.shape
    n_sub = 2 if seq % (2 * CHUNK) == 0 else 1
    ts = n_sub * CHUNK
    n_steps = seq // ts
    d_ff = p["mlp_w1"].shape[-1]
    assert d % (16 * n_steps) == 0 and d_ff % (16 * n_steps) == 0 and WIDTH % (16 * n_steps) == 0, (d, d_ff, n_steps)
    n_gates = N_BRANCH * d
    o_gates = w_in_t.shape[1] - n_gates
    gate_slab = n_gates // n_steps
    assert gate_slab % HEAD_DIM == 0 and o_gates % 8 == 0, (gate_slab, o_gates)
    n_in = (N_DN_CB + N_REST_CB + N_RET_CB) * WIDTH
    tab = pl.BlockSpec((bsz, ts, HEAD_DIM), lambda s: (0, s, 0))
    branch = jax.ShapeDtypeStruct((bsz, seq, WIDTH), BF16)
    state = pltpu.VMEM((bsz * N_HEADS, HEAD_DIM, HEAD_DIM), F32)

    def layer_spec(*shape):
        return pl.BlockSpec((None,) + shape, lambda s: (layer,) + (0,) * len(shape))

    row_vec = layer_spec(1, WIDTH)
    group_mat = layer_spec(N_HEADS, CHUNK, CHUNK)
    branch_spec = pl.BlockSpec((bsz, ts, WIDTH), lambda s: (0, s, 0))
    return pl.pallas_call(
        functools.partial(_proj_mixers_kernel, bsz=bsz, d=d, n_sub=n_sub),
        out_shape=(branch, branch, branch, branch,
                   jax.ShapeDtypeStruct((d, d_ff), BF16), jax.ShapeDtypeStruct((d_ff, d), BF16),
                   jax.ShapeDtypeStruct((d, n_gates), BF16),
                   *[jax.ShapeDtypeStruct((WIDTH, d), BF16)] * N_BRANCH, jax.ShapeDtypeStruct((d, d), BF16)),
        grid=(n_steps,),
        in_specs=[pl.BlockSpec((bsz, ts, d), lambda s: (0, s, 0)),
                  pl.BlockSpec((bsz, 1, N_MOD * d), lambda s: (0, 0, 0)),
                  layer_spec(1, d),
                  layer_spec(d, n_in),
                  layer_spec(d, HEAD_DIM),
                  layer_spec(DN_CONV, 3 * WIDTH),
                  layer_spec(2, HEAD_DIM),
                  layer_spec(1, HEAD_DIM),
                  tab, tab, row_vec,
                  group_mat, row_vec, row_vec, row_vec, group_mat,
                  layer_spec(CHUNK, WIDTH),
                  pl.BlockSpec((None, d // n_steps, d_ff), lambda s: (layer, s, 0)),
                  pl.BlockSpec((None, d_ff // n_steps, d), lambda s: (layer, s, 0)),
                  pl.BlockSpec((pl.Element(1), pl.Element(gate_slab), pl.Element(d)),
                               lambda s: (layer, pl.multiple_of(o_gates + s * gate_slab, 8), 0)),
                  *[pl.BlockSpec((None, WIDTH // n_steps, d), lambda s: (layer, s, 0))] * N_BRANCH,
                  pl.BlockSpec((None, d // n_steps, d), lambda s: (layer, s, 0))],
        out_specs=(branch_spec, branch_spec, branch_spec, branch_spec,
                   pl.BlockSpec((d // n_steps, d_ff), lambda s: (s, 0)),
                   pl.BlockSpec((d_ff // n_steps, d), lambda s: (s, 0)),
                   pl.BlockSpec((d, gate_slab), lambda s: (0, s)),
                   *[pl.BlockSpec((WIDTH // n_steps, d), lambda s: (s, 0))] * N_BRANCH,
                   pl.BlockSpec((d // n_steps, d), lambda s: (s, 0))),
        scratch_shapes=[pltpu.VMEM((bsz, ts + CONV_HALO, 3 * WIDTH), F32), state, state,
                        pltpu.VMEM((3 * N_HEADS, CHUNK, CHUNK), F32),
                        pltpu.VMEM((bsz, ts + POOL_HALO, WIDTH), F32)],
        compiler_params=_params(1),
        name="proj_mixers",
    )(x3, mod_l, p["norm1_g"], p["w_main"], p["w_small"], p["dn_conv_w"], p["dn_ab"], p["dn_norm_g"],
      cos2, sin2, p["ret_gn_g"],
      p["pool_w"], p["pool_scale"], p["sg_ln_g"], p["sg_ln_b"], p["sg_w"], p["sg_bias"],
      p["mlp_w1"], p["mlp_w2"], w_in_t, *p["w_br"], p["w_out"])


def _merge_mlp_kernel(ya_ref, yb_ref, yc_ref, yd_ref, x_ref, mod_ref, g_ref, g2_ref, wg_ref, wa_ref, wb_ref, wc_ref,
                      wd_ref, wo_ref, w1_ref, w2_ref, fg_ref, o_ref, *, d, tf, final_norm):
    x = x_ref[...]
    hb = _norm_modulate(x, g_ref[...], mod_ref[:, 0:d], mod_ref[:, d:2 * d]).astype(BF16)
    merged = None
    branches = ((ya_ref, wa_ref), (yb_ref, wb_ref), (yc_ref, wc_ref), (yd_ref, wd_ref))
    for i, (y_ref, wbr_ref) in enumerate(branches):
        gate = _sigmoid(_dot(hb, wg_ref[:, i * d:(i + 1) * d]))
        term = gate * _dot(y_ref[...], wbr_ref[...])
        merged = term if merged is None else merged + term
    r = _dot(merged.astype(BF16), wo_ref[...])
    x = x + mod_ref[:, 2 * d:3 * d] * r
    hb = _norm_modulate(x, g2_ref[...], mod_ref[:, 3 * d:4 * d], mod_ref[:, 4 * d:5 * d]).astype(BF16)
    acc = None
    for f in range(w1_ref.shape[1] // tf):
        a = jnp.maximum(_dot(hb, w1_ref[:, f * tf:(f + 1) * tf]), 0.0)
        part = _dot((a * a).astype(BF16), w2_ref[f * tf:(f + 1) * tf, :])
        acc = part if acc is None else acc + part
    out = x + mod_ref[:, 5 * d:6 * d] * acc
    if final_norm:
        out = out * lax.rsqrt(jnp.mean(out * out, axis=-1, keepdims=True) + EPS) * fg_ref[...]
    o_ref[...] = out


def _merge_mlp(ys, x2, mod_l, norm_g, norm2_g, w_gates, w_br, w_out, w1, w2, final_g, layer, seq, final_norm):
    n_tok, d = x2.shape
    d_ff = w1.shape[-1]
    ts = min(512, seq)
    per_seq = seq // ts
    yspec = pl.BlockSpec((ts, WIDTH), lambda i: (i, 0))
    wspec = pl.BlockSpec((WIDTH, d), lambda i: (0, 0))
    gspec = pl.BlockSpec((None, 1, d), lambda i: (layer, 0, 0))
    return pl.pallas_call(
        functools.partial(_merge_mlp_kernel, d=d, tf=min(1024, d_ff), final_norm=final_norm),
        out_shape=jax.ShapeDtypeStruct((n_tok, d), F32),
        grid=(n_tok // ts,),
        in_specs=[yspec, yspec, yspec, yspec,
                  pl.BlockSpec((ts, d), lambda i: (i, 0)),
                  pl.BlockSpec((None, 1, N_MOD * d), lambda i: (i // per_seq, 0, 0)),
                  gspec, gspec,
                  pl.BlockSpec((d, N_BRANCH * d), lambda i: (0, 0)),
                  wspec, wspec, wspec, wspec,
                  pl.BlockSpec((d, d), lambda i: (0, 0)),
                  pl.BlockSpec((d, d_ff), lambda i: (0, 0)),
                  pl.BlockSpec((d_ff, d), lambda i: (0, 0)),
                  pl.BlockSpec((1, d), lambda i: (0, 0))],
        out_specs=pl.BlockSpec((ts, d), lambda i: (i, 0)),
        compiler_params=_params(1),
        name="merge_mlp",
    )(*ys, x2, mod_l, norm_g, norm2_g, w_gates, *w_br, w_out, w1, w2, final_g.reshape(1, d))


def _split_w_in_kernel(wt_ref, wts_ref, main_ref, small_ref):
    main_ref[...] = wt_ref[0].T.astype(BF16)

    @pl.when(pl.program_id(1) == 0)
    def _():
        lane = lax.broadcasted_iota(jnp.int32, small_ref.shape, 1)
        small_ref[...] = jnp.where(lane < 2 * N_HEADS, wts_ref[0].T, 0.0).astype(BF16)


def _split_w_in(w_t):
    n_layers, _, d = w_t.shape
    n_main_blocks = N_DN_CB + N_REST_CB + N_RET_CB
    o_small = N_DN_CB * WIDTH
    o_rest = o_small + 2 * N_HEADS

    def first_row(j):
        return jnp.where(j < N_DN_CB, j * WIDTH, o_rest + (j - N_DN_CB) * WIDTH)

    main, small = pl.pallas_call(
        _split_w_in_kernel,
        out_shape=(jax.ShapeDtypeStruct((n_layers, d, n_main_blocks * WIDTH), BF16),
                   jax.ShapeDtypeStruct((n_layers, d, HEAD_DIM), BF16)),
        grid=(n_layers, n_main_blocks),
        in_specs=[pl.BlockSpec((pl.Element(1), pl.Element(WIDTH), pl.Element(d)),
                               lambda l, j: (l, pl.multiple_of(first_row(j), 8), 0)),
                  pl.BlockSpec((pl.Element(1), pl.Element(HEAD_DIM), pl.Element(d)), lambda l, j: (l, o_small, 0))],
        out_specs=(pl.BlockSpec((None, d, WIDTH), lambda l, j: (l, 0, j)),
                   pl.BlockSpec((None, d, HEAD_DIM), lambda l, j: (l, 0, 0))),
        compiler_params=_params(2),
        name="split_w_in",
    )(w_t, w_t)
    return main, small


def kernel(x, c, positions, norm1_g, norm2_g, ada_w, ada_b, w_in, dn_conv_w, dn_a_log, dn_dt_bias, dn_norm_g, pool_w, pool_scale, sg_ln_g, sg_ln_b, sg_w, sg_b, ret_gn_g, w_br_dn, w_br_pool, w_br_sg, w_br_ret, w_out, mlp_w1, mlp_w2, final_g):
    bsz, seq, d = x.shape
    n_layers = w_in.shape[0]
    n_tok = bsz * seq
    mod = _modulation(c, ada_w, ada_b)
    cos2, sin2 = _rope_tables(positions)
    x2 = x.reshape(n_tok, d)
    w_in_t = jnp.swapaxes(w_in, 1, 2)
    w_main, w_small = _split_w_in(w_in_t)
    dn_ab = jnp.zeros((n_layers, 2, HEAD_DIM), F32)
    dn_ab = dn_ab.at[:, 0, N_HEADS:2 * N_HEADS].set(dn_a_log).at[:, 1, N_HEADS:2 * N_HEADS].set(dn_dt_bias)
    p = dict(norm1_g=norm1_g.reshape(n_layers, 1, d), w_main=w_main, w_small=w_small,
             dn_conv_w=dn_conv_w, dn_ab=dn_ab, dn_norm_g=dn_norm_g.reshape(n_layers, 1, HEAD_DIM),
             ret_gn_g=ret_gn_g.reshape(n_layers, 1, WIDTH),
             pool_w=pool_w.astype(BF16), pool_scale=pool_scale.reshape(n_layers, 1, WIDTH),
             sg_ln_g=sg_ln_g.reshape(n_layers, 1, WIDTH), sg_ln_b=sg_ln_b.reshape(n_layers, 1, WIDTH), sg_w=sg_w,
             sg_bias=jnp.repeat(jnp.swapaxes(sg_b, 1, 2), HEAD_DIM, axis=2),
             mlp_w1=mlp_w1, mlp_w2=mlp_w2, w_br=(w_br_dn, w_br_pool, w_br_sg, w_br_ret), w_out=w_out)
    norm2 = norm2_g.reshape(n_layers, 1, d)
    for l in range(n_layers):
        mod_l = mod[l].reshape(bsz, 1, N_MOD * d)
        outs = _proj_mixers(x2.reshape(bsz, seq, d), mod_l, l, p, cos2, sin2, w_in_t)
        ys = [y.reshape(n_tok, WIDTH) for y in outs[:N_BRANCH]]
        w1_b, w2_b, w_gates, *w_br, w_out_b = outs[N_BRANCH:]
        x2 = _merge_mlp(ys, x2, mod_l, p["norm1_g"], norm2, w_gates, w_br, w_out_b, w1_b, w2_b, final_g, l, seq,
                        final_norm=(l == n_layers - 1))
    return x2.reshape(bsz, seq, d)
```
